```python
import math
import jax, jax.numpy as jnp
from jax import lax
import numpy as np

D_MODEL = 2048
BATCH = 2
SEQ = 16384
DEPTH = 2

CTX_LEN = 256
GRID_W = 64

S5_WIDTH = D_MODEL // 4
S5_CH = 16
S5_GROUPS = S5_WIDTH // S5_CH
S5_STATE = 64
RW_WIDTH = 3 * D_MODEL // 8
RW_HEAD = 64
RW_HEADS = RW_WIDTH // RW_HEAD
RW_LORA_W = 64
RW_LORA_A = 64
RW_LORA_G = 128
RW_GN_EPS = 64e-5
HG_WIDTH = 3 * D_MODEL // 8
HG_HEAD = 64
HG_HEADS = HG_WIDTH // HG_HEAD
HG_CHUNK = 64

D_MIX = S5_WIDTH + RW_WIDTH + HG_WIDTH
RW_IN = 3 * RW_WIDTH + 2 * RW_LORA_W + 2 * RW_LORA_A + RW_LORA_G
HG_IN = 5 * HG_WIDTH
D_IN = S5_WIDTH + RW_IN + HG_IN

N_GROUPS = 4
EXPERTS_PER_GROUP = 8
N_EXPERTS = N_GROUPS * EXPERTS_PER_GROUP
TOP_K = 2
D_EXPERT = 1024
MOE_BLOCK = 256

NORM_EPS = 1e-6
N_MOD = 6

kernel_name = 'hybrid_s5_rwkv7_hgrn2_hmoe_dit'


def rms_norm(x, g):
    xf = x.astype(jnp.float32)
    y = xf * lax.rsqrt(jnp.mean(xf * xf, axis=-1, keepdims=True) + NORM_EPS)
    return y.astype(x.dtype) * g


def seg_reverse(z, n_ctx):
    return jnp.concatenate([jnp.flip(z[:, :n_ctx], 1), jnp.flip(z[:, n_ctx:], 1)], axis=1)


def dir_stack(z_fwd, z_bwd, n_ctx):
    return jnp.stack([z_fwd, seg_reverse(z_bwd, n_ctx)], axis=0)


def dir_merge(y, n_ctx):
    return y[0] + seg_reverse(y[1], n_ctx)


def neighbours(z, axis):
    n = z.shape[axis]
    zero = jnp.zeros_like(lax.slice_in_dim(z, 0, 1, axis=axis))
    prev = jnp.concatenate([zero, lax.slice_in_dim(z, 0, n - 1, axis=axis)], axis)
    nxt = jnp.concatenate([lax.slice_in_dim(z, 1, n, axis=axis), zero], axis)
    return prev, nxt


def token_shift(p, mu, n_ctx):
    pc, pl = p[:, :n_ctx], p[:, n_ctx:]
    prev_c, next_c = neighbours(pc, 1)
    pc = pc + mu[0] * (prev_c - pc) + mu[1] * (next_c - pc)
    bsz, n_lat, ch = pl.shape
    rows = n_lat // GRID_W
    g = pl.reshape(bsz, rows, GRID_W, ch)
    left, right = neighbours(g, 2)
    up, down = neighbours(g, 1)
    g = g + mu[0] * (left - g) + mu[1] * (right - g) + mu[2] * (up - g) + mu[3] * (down - g)
    return jnp.concatenate([pc, g.reshape(bsz, n_lat, ch)], axis=1)


def complex_linear_combine(e1, e2):
    ar1, ai1, br1, bi1 = e1
    ar2, ai2, br2, bi2 = e2
    return (ar2 * ar1 - ai2 * ai1, ar2 * ai1 + ai2 * ar1,
            ar2 * br1 - ai2 * bi1 + br2, ar2 * bi1 + ai2 * br1 + bi2)


def s5_direction(u, lam_re, lam_im, log_dt, b_re, b_im, c_re, c_im):
    lam_re = lam_re.astype(jnp.float32)
    lam_im = lam_im.astype(jnp.float32)
    dt = jnp.exp(log_dt.astype(jnp.float32))[:, None]
    mag = jnp.exp(lam_re * dt)
    ang = lam_im * dt
    a_re, a_im = mag * jnp.cos(ang), mag * jnp.sin(ang)
    den = lam_re * lam_re + lam_im * lam_im
    f_re = ((a_re - 1.0) * lam_re + a_im * lam_im) / den
    f_im = (a_im * lam_re - (a_re - 1.0) * lam_im) / den
    bb_re = f_re[..., None] * b_re - f_im[..., None] * b_im
    bb_im = f_re[..., None] * b_im + f_im[..., None] * b_re
    bu_re = jnp.einsum('btgc,gpc->btgp', u, bb_re)
    bu_im = jnp.einsum('btgc,gpc->btgp', u, bb_im)
    t = u.shape[1]
    a_re_t = jnp.broadcast_to(a_re, (1, t) + a_re.shape)
    a_im_t = jnp.broadcast_to(a_im, (1, t) + a_im.shape)
    _, _, s_re, s_im = lax.associative_scan(complex_linear_combine, (a_re_t, a_im_t, bu_re, bu_im), axis=1)
    return jnp.einsum('btgp,gcp->btgc', s_re, c_re) - jnp.einsum('btgp,gcp->btgc', s_im, c_im)


def s5_mixer(p, n_ctx, lam_re, lam_im, log_dt, b_re, b_im, c_re, c_im, d_skip, w_glu, b_glu):
    dtype = p.dtype
    bsz, t, _ = p.shape
    u = p.astype(jnp.float32).reshape(bsz, t, S5_GROUPS, S5_CH)
    y_f = s5_direction(u, lam_re[0], lam_im[0], log_dt[0], b_re[0], b_im[0], c_re[0], c_im[0])
    y_b = seg_reverse(s5_direction(seg_reverse(u, n_ctx), lam_re[1], lam_im[1], log_dt[1],
                                   b_re[1], b_im[1], c_re[1], c_im[1]), n_ctx)
    y = (y_f + y_b + d_skip.reshape(S5_GROUPS, S5_CH) * u).reshape(bsz, t, S5_WIDTH)
    y = jax.nn.gelu(y).astype(dtype)
    return y * jax.nn.sigmoid(y @ w_glu + b_glu)


def rwkv7_step(state, inp):
    r, w, k, v, kk, a = inp
    sa = jnp.einsum('...vk,...k->...v', state, kk)
    state = state * w[..., None, :] - sa[..., :, None] * (kk * a)[..., None, :] + v[..., :, None] * k[..., None, :]
    return state, jnp.einsum('...vk,...k->...v', state, r)


def rwkv7_mixer(p, n_ctx, mu, w0, w2, a0, a2, g2, k_k, k_a, r_k, ln_w, ln_b):
    dtype = p.dtype
    bsz, t, _ = p.shape
    p = token_shift(p, mu, n_ctx).astype(jnp.float32)
    cuts = [int(v) for v in np.cumsum([RW_WIDTH, RW_WIDTH, RW_WIDTH, 2 * RW_LORA_W, 2 * RW_LORA_A])]
    r, k, v, wd, ad, gd = jnp.split(p, cuts, axis=-1)
    wd = wd.reshape(bsz, t, 2, RW_LORA_W)
    ad = ad.reshape(bsz, t, 2, RW_LORA_A)
    w_raw = w0[:, None, None, :] + jnp.einsum('btdr,drc->dbtc', jnp.tanh(wd), w2)
    decay = jnp.exp(-jnp.exp(-jax.nn.softplus(-w_raw) - 0.5))
    a = jax.nn.sigmoid(a0[:, None, None, :] + jnp.einsum('btdr,drc->dbtc', ad, a2))
    g = jax.nn.sigmoid(gd) @ g2

    def heads(z):
        return z.reshape(z.shape[:-1] + (RW_HEADS, RW_HEAD))

    kk = heads(k * k_k)
    kk = (kk / jnp.maximum(jnp.linalg.norm(kk, axis=-1, keepdims=True), 1e-12)).reshape(bsz, t, RW_WIDTH)
    k_dir = k * (1.0 + (a - 1.0) * k_a)

    def to_seq(z):
        return jnp.moveaxis(heads(z), 2, 0)

    xs = (to_seq(dir_stack(r, r, n_ctx)), to_seq(dir_stack(decay[0], decay[1], n_ctx)),
          to_seq(dir_stack(k_dir[0], k_dir[1], n_ctx)), to_seq(dir_stack(v, v, n_ctx)),
          to_seq(dir_stack(kk, kk, n_ctx)), to_seq(dir_stack(a[0], a[1], n_ctx)))
    s0 = jnp.zeros((2, bsz, RW_HEADS, RW_HEAD, RW_HEAD), jnp.float32)
    _, y = lax.scan(rwkv7_step, s0, xs)
    y = dir_merge(jnp.moveaxis(y, 0, 2), n_ctx)
    mean = jnp.mean(y, axis=-1, keepdims=True)
    var = jnp.mean(jnp.square(y - mean), axis=-1, keepdims=True)
    y = (y - mean) * lax.rsqrt(var + RW_GN_EPS) * heads(ln_w) + heads(ln_b)
    k_bonus = 0.5 * (k_dir[0] + k_dir[1])
    bonus = jnp.sum(heads(r) * heads(k_bonus) * heads(r_k), axis=-1, keepdims=True) * heads(v)
    return ((y + bonus).reshape(bsz, t, RW_WIDTH) * g).astype(dtype)


def hgrn2_chunk_step(state, inp):
    q, k, v, log_f = inp
    b = jnp.cumsum(log_f, axis=-2)
    tri = jnp.tril(jnp.ones((HG_CHUNK, HG_CHUNK), bool))[:, :, None]
    diff = b[..., :, None, :] - b[..., None, :, :]
    dec = jnp.where(tri, jnp.exp(jnp.where(tri, diff, 0.0)), 0.0)
    scores = jnp.einsum('...tc,...sc,...tsc->...ts', q, k, dec)
    o = scores @ v + jnp.einsum('...tc,...cv->...tv', q * jnp.exp(b), state)
    b_last = b[..., -1:, :]
    state = jnp.exp(b_last[..., 0, :])[..., :, None] * state + jnp.einsum('...sc,...sv->...cv', k * jnp.exp(b_last - b), v)
    return state, o


def hgrn2_mixer(p, n_ctx, lb, norm_g):
    dtype = p.dtype
    bsz, t, _ = p.shape
    pf = p.astype(jnp.float32)
    q, f, i, og = jnp.split(pf, [HG_WIDTH, 3 * HG_WIDTH, 4 * HG_WIDTH], axis=-1)
    f = jnp.moveaxis(f.reshape(bsz, t, 2, HG_WIDTH), 2, 0)
    log_f = jnp.logaddexp(jnp.log(lb), jnp.log1p(-lb) + jax.nn.log_sigmoid(f))
    key = (1.0 - lb) * jax.nn.sigmoid(-f)
    q = jax.nn.silu(q)
    n_chunks = t // HG_CHUNK

    def chunks(z):
        z = z.reshape(2, bsz, n_chunks, HG_CHUNK, HG_HEADS, HG_HEAD)
        return jnp.transpose(z, (2, 0, 1, 4, 3, 5))

    xs = (chunks(dir_stack(q, q, n_ctx)), chunks(dir_stack(key[0], key[1], n_ctx)),
          chunks(dir_stack(i, i, n_ctx)), chunks(dir_stack(log_f[0], log_f[1], n_ctx)))
    s0 = jnp.zeros((2, bsz, HG_HEADS, HG_HEAD, HG_HEAD), jnp.float32)
    _, o = lax.scan(hgrn2_chunk_step, s0, xs)
    o = jnp.transpose(o, (1, 2, 0, 4, 3, 5)).reshape(2, bsz, t, HG_HEADS, HG_HEAD)
    o = dir_merge(o, n_ctx)
    o = o * lax.rsqrt(jnp.mean(o * o, axis=-1, keepdims=True) + NORM_EPS) * norm_g.reshape(HG_HEADS, HG_HEAD)
    return (o.reshape(bsz, t, HG_WIDTH) * jax.nn.sigmoid(og)).astype(dtype)


def moe_ffn(h, w_group, b_group, w_expert, b_expert, w1, w3, w2):
    n, d = h.shape
    dtype = h.dtype
    group_logits = (h @ w_group).astype(jnp.float32) + b_group.astype(jnp.float32)
    g_sel = jnp.argmax(group_logits, axis=-1)
    g_gate = jnp.take_along_axis(jax.nn.softmax(group_logits, axis=-1), g_sel[:, None], axis=1)
    exp_logits = ((h @ w_expert).astype(jnp.float32) + b_expert.astype(jnp.float32)).reshape(n, N_GROUPS, EXPERTS_PER_GROUP)
    within = jnp.take_along_axis(exp_logits, g_sel[:, None, None], axis=1)[:, 0]
    top_val, top_idx = lax.top_k(within, TOP_K)
    gate = jax.nn.softmax(top_val, axis=-1) * g_gate
    e_flat = (g_sel[:, None] * EXPERTS_PER_GROUP + top_idx).reshape(-1).astype(jnp.int32)
    w_flat = gate.reshape(-1)
    n_assign = n * TOP_K
    tok_flat = jnp.arange(n_assign, dtype=jnp.int32) // TOP_K
    order = jnp.argsort(e_flat)
    e_s, tok_s, w_s = e_flat[order], tok_flat[order], w_flat[order]
    counts = jnp.zeros((N_EXPERTS,), jnp.int32).at[e_flat].add(1)
    starts = jnp.cumsum(counts) - counts
    padded = (counts + MOE_BLOCK - 1) // MOE_BLOCK * MOE_BLOCK
    pad_ends = jnp.cumsum(padded)
    pad_starts = pad_ends - padded
    dest = pad_starts[e_s] + jnp.arange(n_assign, dtype=jnp.int32) - starts[e_s]
    n_blocks = -(-n_assign // MOE_BLOCK) + N_EXPERTS
    n_rows = n_blocks * MOE_BLOCK
    row_tok = jnp.full((n_rows,), n, jnp.int32).at[dest].set(tok_s)
    row_w = jnp.zeros((n_rows,), jnp.float32).at[dest].set(w_s)
    block_exp = jnp.minimum(jnp.searchsorted(pad_ends, jnp.arange(n_blocks, dtype=jnp.int32) * MOE_BLOCK, side='right'), N_EXPERTS - 1)
    h_pad = jnp.concatenate([h, jnp.zeros((1, d), dtype)], axis=0)

    def expert_block(args):
        tok, e = args
        xb = h_pad[tok]
        return (jax.nn.silu(xb @ w1[e]) * (xb @ w3[e])) @ w2[e]

    y_rows = lax.map(expert_block, (row_tok.reshape(n_blocks, MOE_BLOCK), block_exp)).reshape(n_rows, d)
    y = jnp.zeros((n + 1, d), dtype).at[row_tok].add(y_rows * row_w[:, None].astype(dtype))
    return y[:n]


def setup_inputs(seed: int = 0) -> dict:
    key = jax.random.key(seed)
    keys = iter(jax.random.split(key, 64))
    f32 = jnp.float32

    def normal(shape, scale):
        return scale * jax.random.normal(next(keys), shape, f32)

    def uniform(shape, lo, hi):
        return jax.random.uniform(next(keys), shape, f32, lo, hi)

    L = DEPTH
    s5_shape = (L, 2, S5_GROUPS, S5_STATE)
    decay_speed = -7.0 + 5.0 * (jnp.arange(RW_WIDTH, dtype=f32) / (RW_WIDTH - 1)) ** 0.85 + 0.5
    return {
        'x': normal((BATCH, SEQ, D_MODEL), 1.0),
        'c': normal((BATCH, D_MODEL), 1.0),
        'ctx': normal((BATCH, CTX_LEN, D_MODEL), 1.0),
        'c_ctx': normal((D_MODEL,), 1.0),
        'ada_w': normal((L, D_MODEL, N_MOD * D_MODEL), D_MODEL ** -0.5),
        'ada_b': normal((L, N_MOD * D_MODEL), 0.02),
        'norm_mix': 1.0 + normal((L, D_MODEL), 0.02),
        'norm_ffn': 1.0 + normal((L, D_MODEL), 0.02),
        'w_in': normal((L, D_MODEL, D_IN), D_MODEL ** -0.5),
        'w_out': normal((L, D_MIX, D_MODEL), D_MIX ** -0.5),
        's5_lam_re': -0.5 + normal(s5_shape, 0.01),
        's5_lam_im': math.pi * jnp.arange(S5_STATE, dtype=f32) + normal(s5_shape, 0.01),
        's5_log_dt': uniform((L, 2, S5_GROUPS), math.log(1e-3), math.log(1e-1)),
        's5_b_re': normal((L, 2, S5_GROUPS, S5_STATE, S5_CH), (2 * S5_CH) ** -0.5),
        's5_b_im': normal((L, 2, S5_GROUPS, S5_STATE, S5_CH), (2 * S5_CH) ** -0.5),
        's5_c_re': normal((L, 2, S5_GROUPS, S5_CH, S5_STATE), S5_STATE ** -0.5),
        's5_c_im': normal((L, 2, S5_GROUPS, S5_CH, S5_STATE), S5_STATE ** -0.5),
        's5_d': normal((L, S5_WIDTH), 1.0),
        's5_w_glu': normal((L, S5_WIDTH, S5_WIDTH), S5_WIDTH ** -0.5),
        's5_b_glu': normal((L, S5_WIDTH), 0.02),
        'rw_mu': uniform((L, 4, RW_IN), 0.0, 0.25),
        'rw_w0': decay_speed + normal((L, 2, RW_WIDTH), 0.02),
        'rw_w2': normal((L, 2, RW_LORA_W, RW_WIDTH), 0.1 * RW_LORA_W ** -0.5),
        'rw_a0': normal((L, 2, RW_WIDTH), 0.1),
        'rw_a2': normal((L, 2, RW_LORA_A, RW_WIDTH), 0.1 * RW_LORA_A ** -0.5),
        'rw_g2': normal((L, RW_LORA_G, RW_WIDTH), RW_LORA_G ** -0.5),
        'rw_k_k': 0.85 + normal((L, RW_WIDTH), 0.02),
        'rw_k_a': 1.0 + normal((L, RW_WIDTH), 0.02),
        'rw_r_k': normal((L, RW_WIDTH), 0.1),
        'rw_ln_w': 1.0 + normal((L, RW_WIDTH), 0.02),
        'rw_ln_b': normal((L, RW_WIDTH), 0.02),
        'hg_lb_logits': normal((L, HG_WIDTH), 0.1),
        'hg_norm': 1.0 + normal((L, HG_WIDTH), 0.02),
        'moe_w_group': normal((L, D_MODEL, N_GROUPS), D_MODEL ** -0.5),
        'moe_b_group': normal((L, N_GROUPS), 0.01),
        'moe_w_expert': normal((L, D_MODEL, N_EXPERTS), D_MODEL ** -0.5),
        'moe_b_expert': normal((L, N_EXPERTS), 0.01),
        'moe_w1': normal((L, N_EXPERTS, D_MODEL, D_EXPERT), D_MODEL ** -0.5),
        'moe_w3': normal((L, N_EXPERTS, D_MODEL, D_EXPERT), D_MODEL ** -0.5),
        'moe_w2': normal((L, N_EXPERTS, D_EXPERT, D_MODEL), D_EXPERT ** -0.5),
        'final_norm': 1.0 + normal((D_MODEL,), 0.02),
    }


def reference(x, c, ctx, c_ctx, ada_w, ada_b, norm_mix, norm_ffn, w_in, w_out,
              s5_lam_re, s5_lam_im, s5_log_dt, s5_b_re, s5_b_im, s5_c_re, s5_c_im, s5_d, s5_w_glu, s5_b_glu,
              rw_mu, rw_w0, rw_w2, rw_a0, rw_a2, rw_g2, rw_k_k, rw_k_a, rw_r_k, rw_ln_w, rw_ln_b,
              hg_lb_logits, hg_norm,
              moe_w_group, moe_b_group, moe_w_expert, moe_b_expert, moe_w1, moe_w3, moe_w2,
              final_norm):
    bsz, n_lat, d = x.shape
    n_ctx = ctx.shape[1]
    silu_c = jax.nn.silu(c)
    silu_cc = jax.nn.silu(c_ctx)
    lb_all = jnp.cumsum(jax.nn.softmax(hg_lb_logits.astype(jnp.float32), axis=0), axis=0)
    lb_all = lb_all - lb_all[0:1]
    for l in range(DEPTH):
        last = l == DEPTH - 1
        sh_m, sc_m, gt_m, sh_f, sc_f, gt_f = jnp.split((silu_c @ ada_w[l] + ada_b[l])[:, None, :], N_MOD, axis=-1)
        csh_m, csc_m, cgt_m, csh_f, csc_f, cgt_f = jnp.split(silu_cc @ ada_w[l] + ada_b[l], N_MOD, axis=-1)
        h = jnp.concatenate([rms_norm(ctx, norm_mix[l]) * (1.0 + csc_m) + csh_m,
                             rms_norm(x, norm_mix[l]) * (1.0 + sc_m) + sh_m], axis=1)
        p = h @ w_in[l]
        p_s5, p_rw, p_hg = jnp.split(p, [S5_WIDTH, S5_WIDTH + RW_IN], axis=-1)
        y = jnp.concatenate([
            s5_mixer(p_s5, n_ctx, s5_lam_re[l], s5_lam_im[l], s5_log_dt[l], s5_b_re[l], s5_b_im[l],
                     s5_c_re[l], s5_c_im[l], s5_d[l], s5_w_glu[l], s5_b_glu[l]),
            rwkv7_mixer(p_rw, n_ctx, rw_mu[l], rw_w0[l], rw_w2[l], rw_a0[l], rw_a2[l], rw_g2[l],
                        rw_k_k[l], rw_k_a[l], rw_r_k[l], rw_ln_w[l], rw_ln_b[l]),
            hgrn2_mixer(p_hg, n_ctx, lb_all[l], hg_norm[l]),
        ], axis=-1)
        m = y @ w_out[l]
        x = x + gt_m * m[:, n_ctx:]
        h_lat = rms_norm(x, norm_ffn[l]) * (1.0 + sc_f) + sh_f
        moe_args = (moe_w_group[l], moe_b_group[l], moe_w_expert[l], moe_b_expert[l], moe_w1[l], moe_w3[l], moe_w2[l])
        if last:
            f = moe_ffn(h_lat.reshape(bsz * n_lat, d), *moe_args).reshape(bsz, n_lat, d)
            x = x + gt_f * f
        else:
            ctx = ctx + cgt_m * m[:, :n_ctx]
            h_ctx = rms_norm(ctx, norm_ffn[l]) * (1.0 + csc_f) + csh_f
            tok = jnp.concatenate([h_ctx, h_lat], axis=1)
            f = moe_ffn(tok.reshape(bsz * (n_ctx + n_lat), d), *moe_args).reshape(bsz, n_ctx + n_lat, d)
            ctx = ctx + cgt_f * f[:, :n_ctx]
            x = x + gt_f * f[:, n_ctx:]
    return rms_norm(x, final_norm)
```

```python
import functools
import math

import numpy as np
import jax
import jax.numpy as jnp
from jax import lax
from jax.experimental import pallas as pl
from jax.experimental.pallas import tpu as pltpu

F32 = jnp.float32
BF16 = jnp.bfloat16

GRID_W = 64
CH = GRID_W
HEAD = 64
S5_CH = 16
S5_STATE = 64
S5_CHUNK = 32
RW_LORA = 64
RW_GN_EPS = 64e-5
NORM_EPS = 1e-6
TOP_K = 2
MOE_BLOCK = 256
VMEM_LIMIT = 48 * 1024 * 1024


def _dot(a, b):
    return jnp.dot(a, b, preferred_element_type=F32)


def _dot_nt(a, b):
    return lax.dot_general(a, b, (((1,), (1,)), ((), ())), preferred_element_type=F32)


def _dot_tn(a, b):
    return lax.dot_general(a, b, (((0,), (0,)), ((), ())), preferred_element_type=F32)


def _split3(x):
    hi = x.astype(BF16)
    r1 = x - hi.astype(F32)
    mid = r1.astype(BF16)
    lo = (r1 - mid.astype(F32)).astype(BF16)
    return hi, mid, lo


def _dot_sel(m01, x):
    hi, mid, lo = _split3(x)
    return _dot(m01, hi) + _dot(m01, mid) + _dot(m01, lo)


def _dot_hp(a, b):
    a_hi = a.astype(BF16)
    a_lo = (a - a_hi.astype(F32)).astype(BF16)
    b_hi = b.astype(BF16)
    b_lo = (b - b_hi.astype(F32)).astype(BF16)
    return _dot(a_hi, b_hi) + _dot(a_hi, b_lo) + _dot(a_lo, b_hi)


def _sigmoid(x):
    return 1.0 / (1.0 + jnp.exp(-x))


def _scan_chunk(d, i, ncc, nc):
    back = jnp.where(i < ncc, ncc - 1 - i, nc + ncc - 1 - i)
    return jnp.where(d == 0, i, back)


def _half_block_masks(row, col, sgn):
    masks = []
    n = 2
    while n <= CH:
        same = (row // n) == (col // n)
        t_hi = ((row % n) >= n // 2).astype(jnp.int32)
        s_hi = ((col % n) >= n // 2).astype(jnp.int32)
        masks.append(jnp.logical_and(same, (t_hi - s_hi) * sgn == 1))
        n *= 2
    return masks


def _head_block_diag(width):
    idx = np.arange(width) // HEAD
    return jnp.asarray(idx[:, None] == idx[None, :], dtype=BF16)


def _rw_prep_kernel(cur_ref, prev_ref, next_ref, mu_ref, w0_ref, w2_ref, a0_ref, a2_ref, g2_ref,
                    kkw_ref, ka_ref, bd_ref,
                    r_out, v_out, kk_out, g_out, lw_out, kd_out, bb_out, *, ncc, nc, width):
    n = pl.program_id(1)
    cur = cur_ref[0]
    prv = prev_ref[0]
    nxt = next_ref[0]
    is_ctx = n < ncc
    row = lax.broadcasted_iota(jnp.int32, (CH, 1), 0)
    zero_row = jnp.zeros((1, cur.shape[1]), F32)
    first_fill = jnp.where(jnp.logical_and(is_ctx, n > 0), prv[CH - 1:CH, :], zero_row)
    last_fill = jnp.where(jnp.logical_and(is_ctx, n < ncc - 1), nxt[0:1, :], zero_row)
    nb_prev = jnp.where(row == 0, first_fill, pltpu.roll(cur, 1, 0))
    nb_next = jnp.where(row == CH - 1, last_fill, pltpu.roll(cur, CH - 1, 0))
    zeros = jnp.zeros_like(cur)
    up = jnp.where(is_ctx, cur, jnp.where(n > ncc, prv, zeros))
    down = jnp.where(is_ctx, cur, jnp.where(n < nc - 1, nxt, zeros))
    mu = mu_ref[...]
    x = (cur + mu[0:1] * (nb_prev - cur) + mu[1:2] * (nb_next - cur)
         + mu[2:3] * (up - cur) + mu[3:4] * (down - cur))

    w = width
    r = x[:, 0:w]
    k = x[:, w:2 * w]
    v = x[:, 2 * w:3 * w]
    wd = x[:, 3 * w:3 * w + 2 * RW_LORA]
    ad = x[:, 3 * w + 2 * RW_LORA:3 * w + 4 * RW_LORA]
    gd = x[:, 3 * w + 4 * RW_LORA:]

    r_out[0] = r
    v_out[0] = v
    g_out[0] = _dot_hp(_sigmoid(gd), g2_ref[...])
    kk = k * kkw_ref[...]
    kk2 = kk * kk
    kk2_hi = kk2.astype(BF16)
    kk2_lo = (kk2 - kk2_hi.astype(F32)).astype(BF16)
    ss = _dot(kk2_hi, bd_ref[...]) + _dot(kk2_lo, bd_ref[...])
    kk = kk / jnp.maximum(jnp.sqrt(ss), 1e-12)
    kk_out[0] = kk
    ka = ka_ref[...]
    for d in range(2):
        w_raw = w0_ref[d:d + 1, :] + _dot_hp(jnp.tanh(wd[:, d * RW_LORA:(d + 1) * RW_LORA]), w2_ref[d])
        nw = -w_raw
        softplus = jnp.maximum(nw, 0.0) + jnp.log(1.0 + jnp.exp(-jnp.abs(nw)))
        lw_out[d, 0] = -jnp.exp(-softplus - 0.5)
        a = _sigmoid(a0_ref[d:d + 1, :] + _dot_hp(ad[:, d * RW_LORA:(d + 1) * RW_LORA], a2_ref[d]))
        kd_out[d, 0] = k * (1.0 + (a - 1.0) * ka)
        bb_out[d, 0] = kk * a


def _rw_prep(p_rw, n_ctx, mu, w0, w2, a0, a2, g2, k_k, k_a):
    bsz, t, win = p_rw.shape
    width = k_k.shape[0]
    nc = t // CH
    ncc = n_ctx // CH
    tok = lambda b, n: (b, n, 0)
    full2 = lambda b, n: (0, 0)
    full3 = lambda b, n: (0, 0, 0)
    blk_in = (1, CH, win)
    out1 = jax.ShapeDtypeStruct((bsz, t, width), F32)
    out2 = jax.ShapeDtypeStruct((2, bsz, t, width), F32)
    spec1 = pl.BlockSpec((1, CH, width), tok)
    spec2 = pl.BlockSpec((2, 1, CH, width), lambda b, n: (0, b, n, 0))
    return pl.pallas_call(
        functools.partial(_rw_prep_kernel, ncc=ncc, nc=nc, width=width),
        grid=(bsz, nc),
        in_specs=[
            pl.BlockSpec(blk_in, tok),
            pl.BlockSpec(blk_in, lambda b, n: (b, jnp.maximum(n - 1, 0), 0)),
            pl.BlockSpec(blk_in, lambda b, n: (b, jnp.minimum(n + 1, nc - 1), 0)),
            pl.BlockSpec(mu.shape, full2),
            pl.BlockSpec(w0.shape, full2),
            pl.BlockSpec(w2.shape, full3),
            pl.BlockSpec(a0.shape, full2),
            pl.BlockSpec(a2.shape, full3),
            pl.BlockSpec(g2.shape, full2),
            pl.BlockSpec((1, width), full2),
            pl.BlockSpec((1, width), full2),
            pl.BlockSpec((width, width), full2),
        ],
        out_specs=[spec1, spec1, spec1, spec1, spec2, spec2, spec2],
        out_shape=[out1, out1, out1, out1, out2, out2, out2],
        compiler_params=pltpu.CompilerParams(
            dimension_semantics=("arbitrary", "arbitrary"), vmem_limit_bytes=VMEM_LIMIT),
        name="rwkv_prep",
    )(p_rw, p_rw, p_rw, mu, w0, w2, a0, a2, g2, k_k.reshape(1, width), k_a.reshape(1, width),
      _head_block_diag(width))


def _rw_scan_kernel(r_ref, lw_ref, kd_ref, v_ref, kk_ref, bb_ref, y_ref, s_ref, *, heads):
    d = pl.program_id(0)
    i = pl.program_id(2)

    @pl.when(i == 0)
    def _():
        s_ref[...] = jnp.zeros_like(s_ref)

    row = lax.broadcasted_iota(jnp.int32, (CH, CH), 0)
    col = lax.broadcasted_iota(jnp.int32, (CH, CH), 1)
    sgn = jnp.where(d == 0, 1, -1)
    strict = (row - col) * sgn > 0
    diag = row == col
    incl = jnp.logical_or(strict, diag)
    tri = jnp.where(incl, 1.0, 0.0).astype(BF16)
    eye = jnp.where(diag, 1.0, 0.0)
    halves = _half_block_masks(row, col, sgn)

    lw = lw_ref[0, 0]
    cw = _dot_sel(tri, lw)
    ctot = jnp.sum(lw, axis=0, keepdims=True)
    e_cw = jnp.exp(cw)
    e_ncw = jnp.exp(-cw)
    r_t = (r_ref[0] * e_cw).astype(BF16)
    kk_t = (kk_ref[0] * jnp.exp(cw - lw)).astype(BF16)
    bb = bb_ref[0, 0]
    kd = kd_ref[0, 0]
    b_t = (bb * e_ncw).astype(BF16)
    k_t = (kd * e_ncw).astype(BF16)
    e_rem = jnp.exp(ctot - cw)
    b_w = (bb * e_rem).astype(BF16)
    k_w = (kd * e_rem).astype(BF16)
    wtot = jnp.exp(ctot)
    v_all = v_ref[0].astype(BF16)

    for h in range(heads):
        sl = slice(h * HEAD, (h + 1) * HEAD)
        s0 = s_ref[h]
        s0_b = s0.astype(BF16)
        v = v_all[:, sl]
        kkr = jnp.concatenate([kk_t[:, sl], r_t[:, sl]], axis=0)
        bk = jnp.concatenate([b_t[:, sl], k_t[:, sl]], axis=0)
        att = _dot_nt(kkr, bk)
        a_ab = jnp.where(strict, att[:CH, :CH], 0.0)
        a_ak = jnp.where(strict, att[:CH, CH:], 0.0)
        a_rb = jnp.where(incl, att[CH:, :CH], 0.0)
        a_rk = jnp.where(incl, att[CH:, CH:], 0.0)
        proj = _dot_nt(kkr, s0_b)
        inv = eye - jnp.where(halves[0], a_ab, 0.0)
        for lvl in range(1, len(halves)):
            inv_b = inv.astype(BF16)
            a_l = jnp.where(halves[lvl], a_ab, 0.0).astype(BF16)
            inv = inv - _dot(_dot(inv_b, a_l).astype(BF16), inv_b)
        rhs = proj[:CH] + _dot(a_ak.astype(BF16), v)
        u = -_dot(inv.astype(BF16), rhs.astype(BF16))
        u_b = u.astype(BF16)
        y = proj[CH:] + _dot(a_rb.astype(BF16), u_b) + _dot(a_rk.astype(BF16), v)
        y_ref[0, 0, :, sl] = y
        s_ref[h] = s0 * wtot[:, sl] + _dot_tn(u_b, b_w[:, sl]) + _dot_tn(v, k_w[:, sl])


def _rw_scan(r, lw, kd, v, kk, bb, n_ctx):
    bsz, t, width = r.shape
    heads = width // HEAD
    nc = t // CH
    ncc = n_ctx // CH
    tok = lambda d, b, i: (b, _scan_chunk(d, i, ncc, nc), 0)
    dtok = lambda d, b, i: (d, b, _scan_chunk(d, i, ncc, nc), 0)
    s1 = pl.BlockSpec((1, CH, width), tok)
    s2 = pl.BlockSpec((1, 1, CH, width), dtok)
    return pl.pallas_call(
        functools.partial(_rw_scan_kernel, heads=heads),
        grid=(2, bsz, nc),
        in_specs=[s1, s2, s2, s1, s1, s2],
        out_specs=s2,
        out_shape=jax.ShapeDtypeStruct((2, bsz, t, width), F32),
        scratch_shapes=[pltpu.VMEM((heads, HEAD, HEAD), F32)],
        compiler_params=pltpu.CompilerParams(
            dimension_semantics=("arbitrary", "arbitrary", "arbitrary"), vmem_limit_bytes=VMEM_LIMIT),
        name="rwkv_scan",
    )(r, lw, kd, v, kk, bb)


HG_LEVELS = 6


def _hg_decay_matrices():
    out = np.zeros((2, HG_LEVELS + 2, CH, CH), np.float32)
    pos = np.arange(CH)
    for d in range(2):
        for t in range(CH):
            if d == 0:
                out[d, 0, t, :t + 1] = 1.0
                out[d, 1, t, t + 1:] = 1.0
            else:
                out[d, 0, t, t:] = 1.0
                out[d, 1, t, :t] = 1.0
            for l in range(1, HG_LEVELS + 1):
                n = 1 << l
                mid = (t // n) * n + n // 2
                if d == 0:
                    sel = (pos >= mid) & (pos <= t) if t >= mid else (pos > t) & (pos < mid)
                else:
                    sel = (pos >= t) & (pos < mid) if t < mid else (pos >= mid) & (pos < t)
                out[d, 1 + l, t, sel] = 1.0
    return jnp.asarray(out.reshape(2, (HG_LEVELS + 2) * CH, CH), dtype=BF16)


def _hg_scan_kernel(q_ref, f_ref, i_ref, lb_ref, dm_ref, o_ref, s_ref, *, heads):
    d = pl.program_id(0)
    step = pl.program_id(2)

    @pl.when(step == 0)
    def _():
        s_ref[...] = jnp.zeros_like(s_ref)

    row = lax.broadcasted_iota(jnp.int32, (CH, CH), 0)
    col = lax.broadcasted_iota(jnp.int32, (CH, CH), 1)
    sgn = jnp.where(d == 0, 1, -1)

    lb = lb_ref[...]
    f = f_ref[0]
    q = q_ref[0]
    q = q * _sigmoid(q)
    log_sig = jnp.minimum(f, 0.0) - jnp.log(1.0 + jnp.exp(-jnp.abs(f)))
    hi_term = jnp.log(1.0 - lb) + log_sig
    lo_term = jnp.log(jnp.maximum(lb, 1e-38))
    mx = jnp.maximum(hi_term, lo_term)
    lf_mix = mx + jnp.log(1.0 + jnp.exp(-jnp.abs(hi_term - lo_term)))
    lf = jnp.where(lb > 0.0, lf_mix, hi_term)
    key = (1.0 - lb) * _sigmoid(-f)

    ex = jnp.exp(_dot_sel(dm_ref[0], lf))
    q_in = (q * ex[0:CH]).astype(BF16)
    k_out = (key * ex[CH:2 * CH]).astype(BF16)
    etot = jnp.exp(jnp.sum(lf, axis=0, keepdims=True))
    q_b = q.astype(BF16)
    k_b = key.astype(BF16)
    v_all = i_ref[0].astype(BF16)

    masks = _half_block_masks(row, col, sgn)
    diag = row == col

    for h in range(heads):
        sl = slice(h * HEAD, (h + 1) * HEAD)
        sc = jnp.where(diag, _dot_nt(q_b[:, sl], k_b[:, sl]), 0.0)
        for l in range(1, HG_LEVELS + 1):
            fac = ex[(1 + l) * CH:(2 + l) * CH, sl]
            ql = (q[:, sl] * fac).astype(BF16)
            kl = (key[:, sl] * fac).astype(BF16)
            sc = sc + jnp.where(masks[l - 1], _dot_nt(ql, kl), 0.0)
        v = v_all[:, sl]
        s0 = s_ref[h]
        o = _dot(sc.astype(BF16), v) + _dot_nt(q_in[:, sl], s0.astype(BF16))
        o_ref[0, 0, :, sl] = o
        s_ref[h] = s0 * etot[:, sl] + _dot_tn(v, k_out[:, sl])


def _hg_scan(p_hg, lb, n_ctx):
    bsz, t, win = p_hg.shape
    width = lb.shape[0]
    heads = width // HEAD
    nc = t // CH
    ncc = n_ctx // CH
    dm = _hg_decay_matrices()
    chunk = lambda d, i: _scan_chunk(d, i, ncc, nc)
    blk = (1, CH, width)
    return pl.pallas_call(
        functools.partial(_hg_scan_kernel, heads=heads),
        grid=(2, bsz, nc),
        in_specs=[
            pl.BlockSpec(blk, lambda d, b, i: (b, chunk(d, i), 0)),
            pl.BlockSpec(blk, lambda d, b, i: (b, chunk(d, i), 1 + d)),
            pl.BlockSpec(blk, lambda d, b, i: (b, chunk(d, i), 3)),
            pl.BlockSpec((1, width), lambda d, b, i: (0, 0)),
            pl.BlockSpec((1,) + dm.shape[1:], lambda d, b, i: (d, 0, 0)),
        ],
        out_specs=pl.BlockSpec((1, 1, CH, width), lambda d, b, i: (d, b, chunk(d, i), 0)),
        out_shape=jax.ShapeDtypeStruct((2, bsz, t, width), F32),
        scratch_shapes=[pltpu.VMEM((heads, HEAD, HEAD), F32)],
        compiler_params=pltpu.CompilerParams(
            dimension_semantics=("arbitrary", "arbitrary", "arbitrary"), vmem_limit_bytes=VMEM_LIMIT),
        name="hgrn2_scan",
    )(p_hg, p_hg, p_hg, lb.reshape(1, width), dm)


def _heads(z):
    return z.reshape(z.shape[:-1] + (z.shape[-1] // HEAD, HEAD))


def _rwkv7_mixer(p_rw, n_ctx, mu, w0, w2, a0, a2, g2, k_k, k_a, r_k, ln_w, ln_b):
    bsz, t, _ = p_rw.shape
    r, v, kk, g, lw, kd, bb = _rw_prep(p_rw, n_ctx, mu, w0, w2, a0, a2, g2, k_k, k_a)
    y2 = _rw_scan(r, lw, kd, v, kk, bb, n_ctx)
    y = _heads(y2[0] + y2[1])
    mean = jnp.mean(y, axis=-1, keepdims=True)
    var = jnp.mean(jnp.square(y - mean), axis=-1, keepdims=True)
    y = (y - mean) * lax.rsqrt(var + RW_GN_EPS) * _heads(ln_w) + _heads(ln_b)
    k_bonus = 0.5 * (kd[0] + kd[1])
    bonus = jnp.sum(_heads(r) * _heads(k_bonus) * _heads(r_k), axis=-1, keepdims=True) * _heads(v)
    return (y + bonus).reshape(bsz, t, -1) * g


def _hgrn2_mixer(p_hg, n_ctx, lb, norm_g):
    bsz, t, _ = p_hg.shape
    width = lb.shape[0]
    o2 = _hg_scan(p_hg, lb, n_ctx)
    o = _heads(o2[0] + o2[1])
    o = o * lax.rsqrt(jnp.mean(o * o, axis=-1, keepdims=True) + NORM_EPS) * _heads(norm_g)
    return o.reshape(bsz, t, width) * jax.nn.sigmoid(p_hg[..., 4 * width:])


def _s5_operators(lam_re, lam_im, log_dt, b_re, b_im, c_re, c_im):
    c = S5_CHUNK
    hp = lax.Precision.HIGHEST
    dt = jnp.exp(log_dt.astype(F32))[..., None]
    lam_re = lam_re.astype(F32)
    lam_im = lam_im.astype(F32)
    mag = jnp.exp(lam_re * dt)
    ang = lam_im * dt
    a_re, a_im = mag * jnp.cos(ang), mag * jnp.sin(ang)
    den = lam_re * lam_re + lam_im * lam_im
    f_re = ((a_re - 1.0) * lam_re + a_im * lam_im) / den
    f_im = (a_im * lam_re - (a_re - 1.0) * lam_im) / den
    bb_re = f_re[..., None] * b_re - f_im[..., None] * b_im
    bb_im = f_re[..., None] * b_im + f_im[..., None] * b_re
    j = jnp.arange(c + 1, dtype=F32)[None, None, :, None]
    pmag = jnp.exp((lam_re * dt)[:, :, None, :] * j)
    pang = ang[:, :, None, :] * j
    pw_re, pw_im = pmag * jnp.cos(pang), pmag * jnp.sin(pang)
    ca_re = c_re[:, :, None] * pw_re[:, :, :, None] - c_im[:, :, None] * pw_im[:, :, :, None]
    ca_im = c_re[:, :, None] * pw_im[:, :, :, None] + c_im[:, :, None] * pw_re[:, :, :, None]
    kern = (jnp.einsum('dgjcp,dgpk->dgjck', ca_re, bb_re, precision=hp)
            - jnp.einsum('dgjcp,dgpk->dgjck', ca_im, bb_im, precision=hp))
    tpos = np.arange(c)
    lag = tpos[None, :] - tpos[:, None]
    kf = jnp.where((lag >= 0)[None, :, :, None, None], kern[0][:, np.clip(lag, 0, c)], 0.0)
    kb = jnp.where((lag <= 0)[None, :, :, None, None], kern[1][:, np.clip(-lag, 0, c)], 0.0)
    g = kern.shape[1]
    tmat = jnp.transpose(kf + kb, (0, 1, 4, 2, 3)).reshape(g, c * S5_CH, c * S5_CH)

    def in_to_state(pr, pi, d):
        hr = pr[:, :, None, :] * jnp.swapaxes(bb_re[d], 1, 2)[:, None] - pi[:, :, None, :] * jnp.swapaxes(bb_im[d], 1, 2)[:, None]
        hi = pr[:, :, None, :] * jnp.swapaxes(bb_im[d], 1, 2)[:, None] + pi[:, :, None, :] * jnp.swapaxes(bb_re[d], 1, 2)[:, None]
        return hr.reshape(g, c * S5_CH, -1), hi.reshape(g, c * S5_CH, -1)

    hf_re, hf_im = in_to_state(pw_re[0, :, c - 1::-1][:, :c], pw_im[0, :, c - 1::-1][:, :c], 0)
    hb_re, hb_im = in_to_state(pw_re[1, :, :c], pw_im[1, :, :c], 1)
    hmat = jnp.concatenate([hf_re, hb_re, hf_im, hb_im], axis=-1)

    def state_to_out(cr, ci):
        return (jnp.transpose(cr, (0, 3, 1, 2)).reshape(g, -1, c * S5_CH),
                -jnp.transpose(ci, (0, 3, 1, 2)).reshape(g, -1, c * S5_CH))

    gf_re, gf_im = state_to_out(ca_re[0, :, 1:c + 1], ca_im[0, :, 1:c + 1])
    gb_re, gb_im = state_to_out(ca_re[1, :, c:0:-1], ca_im[1, :, c:0:-1])
    z = jnp.zeros_like(gf_re)
    gf = jnp.concatenate([gf_re, z, gf_im, z], axis=1)
    gb = jnp.concatenate([z, gb_re, z, gb_im], axis=1)
    ac_re = jnp.concatenate([pw_re[0, :, c], pw_re[1, :, c]], axis=-1)[:, None, :]
    ac_im = jnp.concatenate([pw_im[0, :, c], pw_im[1, :, c]], axis=-1)[:, None, :]
    return tmat.astype(BF16), hmat.astype(BF16), gf.astype(BF16), gb.astype(BF16), ac_re, ac_im


def _s5_local_kernel(u_ref, h_ref, loc_ref):
    loc_ref[0] = _dot(u_ref[0].astype(BF16), h_ref[0])


def _s5_state_kernel(loc_ref, ar_ref, ai_ref, sf_ref, sb_ref, *, nc, ncc, bsz):
    ar = ar_ref[0]
    ai = ai_ref[0]
    half = ar.shape[-1]
    is_fwd = lax.broadcasted_iota(jnp.int32, (1, half), 1) < half // 2

    def body(i, carry):
        nb = jnp.where(i < ncc, ncc - 1 - i, nc + ncc - 1 - i)
        new = []
        for b in range(bsz):
            re, im = carry[b]
            st = jnp.concatenate([re, im], axis=1)
            sf_ref[0, b, pl.ds(i, 1), :] = st
            sb_ref[0, b, pl.ds(nb, 1), :] = st
            lf = loc_ref[0, b, pl.ds(i, 1), :]
            lk = loc_ref[0, b, pl.ds(nb, 1), :]
            l_re = jnp.where(is_fwd, lf[:, :half], lk[:, :half])
            l_im = jnp.where(is_fwd, lf[:, half:], lk[:, half:])
            new.append((ar * re - ai * im + l_re, ar * im + ai * re + l_im))
        return tuple(new)

    zero = jnp.zeros((1, half), F32)
    lax.fori_loop(0, nc, body, tuple((zero, zero) for _ in range(bsz)))


def _s5_out_kernel(u_ref, t_ref, sf_ref, gf_ref, sb_ref, gb_ref, d_ref, y_ref):
    u = u_ref[0]
    y = (_dot(u.astype(BF16), t_ref[0]) + _dot(sf_ref[0].astype(BF16), gf_ref[0])
         + _dot(sb_ref[0].astype(BF16), gb_ref[0]) + d_ref[0] * u)
    y_ref[0] = 0.5 * y * (1.0 + jnp.tanh(math.sqrt(2.0 / math.pi) * (y + 0.044715 * (y * y * y))))


def _s5_scan(p_s5, n_ctx, lam_re, lam_im, log_dt, b_re, b_im, c_re, c_im, d_skip):
    bsz, t, width = p_s5.shape
    g = width // S5_CH
    c = S5_CHUNK
    nc, ncc = t // c, n_ctx // c
    blk = c * S5_CH
    rows = bsz * nc
    tmat, hmat, gf, gb, ac_re, ac_im = _s5_operators(lam_re, lam_im, log_dt, b_re, b_im, c_re, c_im)
    u = jnp.transpose(p_s5.astype(F32).reshape(bsz, nc, c, g, S5_CH), (3, 0, 1, 2, 4)).reshape(g, rows, blk)
    per_g = lambda gi: (gi, 0, 0)
    cparams = pltpu.CompilerParams(dimension_semantics=("arbitrary",), vmem_limit_bytes=VMEM_LIMIT)
    loc = pl.pallas_call(
        _s5_local_kernel, grid=(g,),
        in_specs=[pl.BlockSpec((1, rows, blk), per_g), pl.BlockSpec((1, blk, 256), per_g)],
        out_specs=pl.BlockSpec((1, rows, 256), per_g),
        out_shape=jax.ShapeDtypeStruct((g, rows, 256), F32),
        compiler_params=cparams, name="s5_local",
    )(u, hmat)
    per_g4 = lambda gi: (gi, 0, 0, 0)
    st_shape = jax.ShapeDtypeStruct((g, bsz, nc, 256), F32)
    sf, sb = pl.pallas_call(
        functools.partial(_s5_state_kernel, nc=nc, ncc=ncc, bsz=bsz), grid=(g,),
        in_specs=[pl.BlockSpec((1, bsz, nc, 256), per_g4), pl.BlockSpec((1, 1, 128), per_g),
                  pl.BlockSpec((1, 1, 128), per_g)],
        out_specs=[pl.BlockSpec((1, bsz, nc, 256), per_g4)] * 2,
        out_shape=[st_shape, st_shape],
        compiler_params=cparams, name="s5_state",
    )(loc.reshape(g, bsz, nc, 256), ac_re, ac_im)
    d_flat = jnp.tile(d_skip.astype(F32).reshape(g, 1, S5_CH), (1, c, 1)).reshape(g, 1, blk)
    y = pl.pallas_call(
        _s5_out_kernel, grid=(g,),
        in_specs=[pl.BlockSpec((1, rows, blk), per_g), pl.BlockSpec((1, blk, blk), per_g),
                  pl.BlockSpec((1, rows, 256), per_g), pl.BlockSpec((1, 256, blk), per_g),
                  pl.BlockSpec((1, rows, 256), per_g), pl.BlockSpec((1, 256, blk), per_g),
                  pl.BlockSpec((1, 1, blk), per_g)],
        out_specs=pl.BlockSpec((1, rows, blk), per_g),
        out_shape=jax.ShapeDtypeStruct((g, rows, blk), F32),
        compiler_params=cparams, name="s5_out",
    )(u, tmat, sf.reshape(g, rows, 256), gf, sb.reshape(g, rows, 256), gb, d_flat)
    return jnp.transpose(y.reshape(g, bsz, nc, c, S5_CH), (1, 2, 3, 0, 4)).reshape(bsz, t, width)


def _s5_mixer(p_s5, n_ctx, lam_re, lam_im, log_dt, b_re, b_im, c_re, c_im, d_skip, w_glu, b_glu):
    y = _s5_scan(p_s5, n_ctx, lam_re, lam_im, log_dt, b_re, b_im, c_re, c_im, d_skip)
    return y * jax.nn.sigmoid(y @ w_glu + b_glu)


def _rms_norm(x, g):
    xf = x.astype(F32)
    y = xf * lax.rsqrt(jnp.mean(xf * xf, axis=-1, keepdims=True) + NORM_EPS)
    return y.astype(x.dtype) * g


def _moe_ffn(h, w_group, b_group, w_expert, b_expert, w1, w3, w2):
    n, d = h.shape
    dtype = h.dtype
    n_groups = w_group.shape[1]
    n_experts = w_expert.shape[1]
    epg = n_experts // n_groups
    group_logits = (h @ w_group).astype(F32) + b_group.astype(F32)
    g_sel = jnp.argmax(group_logits, axis=-1)
    g_gate = jnp.take_along_axis(jax.nn.softmax(group_logits, axis=-1), g_sel[:, None], axis=1)
    exp_logits = ((h @ w_expert).astype(F32) + b_expert.astype(F32)).reshape(n, n_groups, epg)
    within = jnp.take_along_axis(exp_logits, g_sel[:, None, None], axis=1)[:, 0]
    top_val, top_idx = lax.top_k(within, TOP_K)
    gate = jax.nn.softmax(top_val, axis=-1) * g_gate
    e_flat = (g_sel[:, None] * epg + top_idx).reshape(-1).astype(jnp.int32)
    w_flat = gate.reshape(-1)
    n_assign = n * TOP_K
    tok_flat = jnp.arange(n_assign, dtype=jnp.int32) // TOP_K
    order = jnp.argsort(e_flat)
    e_s, tok_s, w_s = e_flat[order], tok_flat[order], w_flat[order]
    counts = jnp.zeros((n_experts,), jnp.int32).at[e_flat].add(1)
    starts = jnp.cumsum(counts) - counts
    padded = (counts + MOE_BLOCK - 1) // MOE_BLOCK * MOE_BLOCK
    pad_ends = jnp.cumsum(padded)
    pad_starts = pad_ends - padded
    dest = pad_starts[e_s] + jnp.arange(n_assign, dtype=jnp.int32) - starts[e_s]
    n_blocks = -(-n_assign // MOE_BLOCK) + n_experts
    n_rows = n_blocks * MOE_BLOCK
    row_tok = jnp.full((n_rows,), n, jnp.int32).at[dest].set(tok_s)
    row_w = jnp.zeros((n_rows,), F32).at[dest].set(w_s)
    block_exp = jnp.minimum(jnp.searchsorted(pad_ends, jnp.arange(n_blocks, dtype=jnp.int32) * MOE_BLOCK, side='right'), n_experts - 1)
    h_pad = jnp.concatenate([h, jnp.zeros((1, d), dtype)], axis=0)

    def expert_block(args):
        tok, e = args
        xb = h_pad[tok]
        return (jax.nn.silu(xb @ w1[e]) * (xb @ w3[e])) @ w2[e]

    y_rows = lax.map(expert_block, (row_tok.reshape(n_blocks, MOE_BLOCK), block_exp)).reshape(n_rows, d)
    y = jnp.zeros((n + 1, d), dtype).at[row_tok].add(y_rows * row_w[:, None].astype(dtype))
    return y[:n]


def kernel(x, c, ctx, c_ctx, ada_w, ada_b, norm_mix, norm_ffn, w_in, w_out, s5_lam_re, s5_lam_im, s5_log_dt, s5_b_re, s5_b_im, s5_c_re, s5_c_im, s5_d, s5_w_glu, s5_b_glu, rw_mu, rw_w0, rw_w2, rw_a0, rw_a2, rw_g2, rw_k_k, rw_k_a, rw_r_k, rw_ln_w, rw_ln_b, hg_lb_logits, hg_norm, moe_w_group, moe_b_group, moe_w_expert, moe_b_expert, moe_w1, moe_w3, moe_w2, final_norm):
    bsz, n_lat, d = x.shape
    n_ctx = ctx.shape[1]
    depth = w_in.shape[0]
    s5_width = s5_d.shape[1]
    rw_width = rw_k_k.shape[1]
    rw_in = rw_mu.shape[2]
    silu_c = jax.nn.silu(c)
    silu_cc = jax.nn.silu(c_ctx)
    lb_all = jnp.cumsum(jax.nn.softmax(hg_lb_logits.astype(F32), axis=0), axis=0)
    lb_all = lb_all - lb_all[0:1]
    for l in range(depth):
        last = l == depth - 1
        sh_m, sc_m, gt_m, sh_f, sc_f, gt_f = jnp.split((silu_c @ ada_w[l] + ada_b[l])[:, None, :], 6, axis=-1)
        csh_m, csc_m, cgt_m, csh_f, csc_f, cgt_f = jnp.split(silu_cc @ ada_w[l] + ada_b[l], 6, axis=-1)
        h = jnp.concatenate([_rms_norm(ctx, norm_mix[l]) * (1.0 + csc_m) + csh_m,
                             _rms_norm(x, norm_mix[l]) * (1.0 + sc_m) + sh_m], axis=1)
        p = h @ w_in[l]
        p_s5, p_rw, p_hg = jnp.split(p, [s5_width, s5_width + rw_in], axis=-1)
        y = jnp.concatenate([
            _s5_mixer(p_s5, n_ctx, s5_lam_re[l], s5_lam_im[l], s5_log_dt[l], s5_b_re[l], s5_b_im[l],
                      s5_c_re[l], s5_c_im[l], s5_d[l], s5_w_glu[l], s5_b_glu[l]),
            _rwkv7_mixer(p_rw, n_ctx, rw_mu[l], rw_w0[l], rw_w2[l], rw_a0[l], rw_a2[l], rw_g2[l],
                         rw_k_k[l], rw_k_a[l], rw_r_k[l], rw_ln_w[l], rw_ln_b[l]),
            _hgrn2_mixer(p_hg, n_ctx, lb_all[l], hg_norm[l]),
        ], axis=-1)
        m = y @ w_out[l]
        x = x + gt_m * m[:, n_ctx:]
        h_lat = _rms_norm(x, norm_ffn[l]) * (1.0 + sc_f) + sh_f
        moe_args = (moe_w_group[l], moe_b_group[l], moe_w_expert[l], moe_b_expert[l], moe_w1[l], moe_w3[l], moe_w2[l])
        if last:
            f = _moe_ffn(h_lat.reshape(bsz * n_lat, d), *moe_args).reshape(bsz, n_lat, d)
            x = x + gt_f * f
        else:
            ctx = ctx + cgt_m * m[:, :n_ctx]
            h_ctx = _rms_norm(ctx, norm_ffn[l]) * (1.0 + csc_f) + csh_f
            tok = jnp.concatenate([h_ctx, h_lat], axis=1)
            f = _moe_ffn(tok.reshape(bsz * (n_ctx + n_lat), d), *moe_args).reshape(bsz, n_ctx + n_lat, d)
            ctx = ctx + cgt_f * f[:, :n_ctx]
            x = x + gt_f * f[:, n_ctx:]
    return _rms_norm(x, final_norm)
```

```python
import functools
import math

import numpy as np
import jax
import jax.numpy as jnp
from jax import lax
from jax.experimental import pallas as pl
from jax.experimental.pallas import tpu as pltpu

F32 = jnp.float32
BF16 = jnp.bfloat16

GRID_W = 64
CH = GRID_W
HEAD = 64
S5_CH = 16
S5_STATE = 64
S5_CHUNK = 32
RW_LORA = 64
RW_GN_EPS = 64e-5
NORM_EPS = 1e-6
TOP_K = 2
MOE_BLOCK = 256
VMEM_LIMIT = 48 * 1024 * 1024


def _dot(a, b):
    return jnp.dot(a, b, preferred_element_type=F32)


def _dot_nt(a, b):
    return lax.dot_general(a, b, (((1,), (1,)), ((), ())), preferred_element_type=F32)


def _dot_tn(a, b):
    return lax.dot_general(a, b, (((0,), (0,)), ((), ())), preferred_element_type=F32)


def _split3(x):
    hi = x.astype(BF16)
    r1 = x - hi.astype(F32)
    mid = r1.astype(BF16)
    lo = (r1 - mid.astype(F32)).astype(BF16)
    return hi, mid, lo


def _dot_sel(m01, x):
    hi, mid, lo = _split3(x)
    return _dot(m01, hi) + _dot(m01, mid) + _dot(m01, lo)


def _dot_hp(a, b):
    a_hi = a.astype(BF16)
    a_lo = (a - a_hi.astype(F32)).astype(BF16)
    b_hi = b.astype(BF16)
    b_lo = (b - b_hi.astype(F32)).astype(BF16)
    return _dot(a_hi, b_hi) + _dot(a_hi, b_lo) + _dot(a_lo, b_hi)


def _sigmoid(x):
    return 1.0 / (1.0 + jnp.exp(-x))


def _scan_chunk(d, i, ncc, nc):
    back = jnp.where(i < ncc, ncc - 1 - i, nc + ncc - 1 - i)
    return jnp.where(d == 0, i, back)


def _half_block_masks(row, col, sgn):
    masks = []
    n = 2
    while n <= CH:
        same = (row // n) == (col // n)
        t_hi = ((row % n) >= n // 2).astype(jnp.int32)
        s_hi = ((col % n) >= n // 2).astype(jnp.int32)
        masks.append(jnp.logical_and(same, (t_hi - s_hi) * sgn == 1))
        n *= 2
    return masks


def _head_block_diag(width):
    idx = np.arange(width) // HEAD
    return jnp.asarray(idx[:, None] == idx[None, :], dtype=BF16)


def _rw_prep_kernel(cur_ref, prev_ref, next_ref, mu_ref, w0_ref, w2_ref, a0_ref, a2_ref, g2_ref,
                    kkw_ref, ka_ref, bd_ref,
                    r_out, v_out, kk_out, g_out, lw_out, kd_out, bb_out, *, ncc, nc, width):
    n = pl.program_id(1)
    cur = cur_ref[0]
    prv = prev_ref[0]
    nxt = next_ref[0]
    is_ctx = n < ncc
    row = lax.broadcasted_iota(jnp.int32, (CH, 1), 0)
    zero_row = jnp.zeros((1, cur.shape[1]), F32)
    first_fill = jnp.where(jnp.logical_and(is_ctx, n > 0), prv[CH - 1:CH, :], zero_row)
    last_fill = jnp.where(jnp.logical_and(is_ctx, n < ncc - 1), nxt[0:1, :], zero_row)
    nb_prev = jnp.where(row == 0, first_fill, pltpu.roll(cur, 1, 0))
    nb_next = jnp.where(row == CH - 1, last_fill, pltpu.roll(cur, CH - 1, 0))
    zeros = jnp.zeros_like(cur)
    up = jnp.where(is_ctx, cur, jnp.where(n > ncc, prv, zeros))
    down = jnp.where(is_ctx, cur, jnp.where(n < nc - 1, nxt, zeros))
    mu = mu_ref[...]
    x = (cur + mu[0:1] * (nb_prev - cur) + mu[1:2] * (nb_next - cur)
         + mu[2:3] * (up - cur) + mu[3:4] * (down - cur))

    w = width
    r = x[:, 0:w]
    k = x[:, w:2 * w]
    v = x[:, 2 * w:3 * w]
    wd = x[:, 3 * w:3 * w + 2 * RW_LORA]
    ad = x[:, 3 * w + 2 * RW_LORA:3 * w + 4 * RW_LORA]
    gd = x[:, 3 * w + 4 * RW_LORA:]

    r_out[0] = r
    v_out[0] = v
    g_out[0] = _dot_hp(_sigmoid(gd), g2_ref[...])
    kk = k * kkw_ref[...]
    kk2 = kk * kk
    kk2_hi = kk2.astype(BF16)
    kk2_lo = (kk2 - kk2_hi.astype(F32)).astype(BF16)
    ss = _dot(kk2_hi, bd_ref[...]) + _dot(kk2_lo, bd_ref[...])
    kk = kk / jnp.maximum(jnp.sqrt(ss), 1e-12)
    kk_out[0] = kk
    ka = ka_ref[...]
    for d in range(2):
        w_raw = w0_ref[d:d + 1, :] + _dot_hp(jnp.tanh(wd[:, d * RW_LORA:(d + 1) * RW_LORA]), w2_ref[d])
        nw = -w_raw
        softplus = jnp.maximum(nw, 0.0) + jnp.log(1.0 + jnp.exp(-jnp.abs(nw)))
        lw_out[d, 0] = -jnp.exp(-softplus - 0.5)
        a = _sigmoid(a0_ref[d:d + 1, :] + _dot_hp(ad[:, d * RW_LORA:(d + 1) * RW_LORA], a2_ref[d]))
        kd_out[d, 0] = k * (1.0 + (a - 1.0) * ka)
        bb_out[d, 0] = kk * a


def _rw_prep(p_rw, n_ctx, mu, w0, w2, a0, a2, g2, k_k, k_a):
    bsz, t, _ = p_rw.shape
    win = mu.shape[1]
    width = k_k.shape[0]
    nc = t // CH
    ncc = n_ctx // CH
    tok = lambda b, n: (b, n, 0)
    full2 = lambda b, n: (0, 0)
    full3 = lambda b, n: (0, 0, 0)
    blk_in = (1, CH, win)
    out1 = jax.ShapeDtypeStruct((bsz, t, width), F32)
    out2 = jax.ShapeDtypeStruct((2, bsz, t, width), F32)
    spec1 = pl.BlockSpec((1, CH, width), tok)
    spec2 = pl.BlockSpec((2, 1, CH, width), lambda b, n: (0, b, n, 0))
    return pl.pallas_call(
        functools.partial(_rw_prep_kernel, ncc=ncc, nc=nc, width=width),
        grid=(bsz, nc),
        in_specs=[
            pl.BlockSpec(blk_in, tok),
            pl.BlockSpec(blk_in, lambda b, n: (b, jnp.maximum(n - 1, 0), 0)),
            pl.BlockSpec(blk_in, lambda b, n: (b, jnp.minimum(n + 1, nc - 1), 0)),
            pl.BlockSpec(mu.shape, full2),
            pl.BlockSpec(w0.shape, full2),
            pl.BlockSpec(w2.shape, full3),
            pl.BlockSpec(a0.shape, full2),
            pl.BlockSpec(a2.shape, full3),
            pl.BlockSpec(g2.shape, full2),
            pl.BlockSpec((1, width), full2),
            pl.BlockSpec((1, width), full2),
            pl.BlockSpec((width, width), full2),
        ],
        out_specs=[spec1, spec1, spec1, spec1, spec2, spec2, spec2],
        out_shape=[out1, out1, out1, out1, out2, out2, out2],
        compiler_params=pltpu.CompilerParams(
            dimension_semantics=("arbitrary", "arbitrary"), vmem_limit_bytes=VMEM_LIMIT),
        name="rwkv_prep",
    )(p_rw, p_rw, p_rw, mu, w0, w2, a0, a2, g2, k_k.reshape(1, width), k_a.reshape(1, width),
      _head_block_diag(width))


def _rw_scan_kernel(r_ref, lw_ref, kd_ref, v_ref, kk_ref, bb_ref, y_ref, s_ref, *, heads):
    d = pl.program_id(0)
    i = pl.program_id(2)

    @pl.when(i == 0)
    def _():
        s_ref[...] = jnp.zeros_like(s_ref)

    row = lax.broadcasted_iota(jnp.int32, (CH, CH), 0)
    col = lax.broadcasted_iota(jnp.int32, (CH, CH), 1)
    sgn = jnp.where(d == 0, 1, -1)
    strict = (row - col) * sgn > 0
    diag = row == col
    incl = jnp.logical_or(strict, diag)
    tri = jnp.where(incl, 1.0, 0.0).astype(BF16)
    eye = jnp.where(diag, 1.0, 0.0)
    halves = _half_block_masks(row, col, sgn)

    lw = lw_ref[0, 0]
    cw = _dot_sel(tri, lw)
    ctot = jnp.sum(lw, axis=0, keepdims=True)
    e_cw = jnp.exp(cw)
    e_ncw = jnp.exp(-cw)
    r_t = (r_ref[0] * e_cw).astype(BF16)
    kk_t = (kk_ref[0] * jnp.exp(cw - lw)).astype(BF16)
    bb = bb_ref[0, 0]
    kd = kd_ref[0, 0]
    b_t = (bb * e_ncw).astype(BF16)
    k_t = (kd * e_ncw).astype(BF16)
    e_rem = jnp.exp(ctot - cw)
    b_w = (bb * e_rem).astype(BF16)
    k_w = (kd * e_rem).astype(BF16)
    wtot = jnp.exp(ctot)
    v_all = v_ref[0].astype(BF16)

    hs = range(heads)
    sls = [slice(h * HEAD, (h + 1) * HEAD) for h in hs]
    s0 = [s_ref[h] for h in hs]
    v = [v_all[:, sl] for sl in sls]
    kkr = [jnp.concatenate([kk_t[:, sl], r_t[:, sl]], axis=0) for sl in sls]
    bk = [jnp.concatenate([b_t[:, sl], k_t[:, sl]], axis=0) for sl in sls]
    att = [_dot_nt(kkr[h], bk[h]) for h in hs]
    proj = [_dot_nt(kkr[h], s0[h].astype(BF16)) for h in hs]
    a_ab = [jnp.where(strict, att[h][:CH, :CH], 0.0) for h in hs]
    a_ak = [jnp.where(strict, att[h][:CH, CH:], 0.0).astype(BF16) for h in hs]
    a_rb = [jnp.where(incl, att[h][CH:, :CH], 0.0).astype(BF16) for h in hs]
    a_rk = [jnp.where(incl, att[h][CH:, CH:], 0.0).astype(BF16) for h in hs]
    inv = [eye - jnp.where(halves[0], a_ab[h], 0.0) for h in hs]
    for lvl in range(1, len(halves)):
        inv_b = [inv[h].astype(BF16) for h in hs]
        a_l = [jnp.where(halves[lvl], a_ab[h], 0.0).astype(BF16) for h in hs]
        left = [_dot(inv_b[h], a_l[h]).astype(BF16) for h in hs]
        inv = [inv[h] - _dot(left[h], inv_b[h]) for h in hs]
    rhs = [(proj[h][:CH] + _dot(a_ak[h], v[h])).astype(BF16) for h in hs]
    u_b = [(-_dot(inv[h].astype(BF16), rhs[h])).astype(BF16) for h in hs]
    y = [proj[h][CH:] + _dot(a_rb[h], u_b[h]) + _dot(a_rk[h], v[h]) for h in hs]
    s_new = [s0[h] * wtot[:, sls[h]] + _dot_tn(u_b[h], b_w[:, sls[h]]) + _dot_tn(v[h], k_w[:, sls[h]]) for h in hs]
    for h in hs:
        y_ref[0, 0, :, sls[h]] = y[h]
        s_ref[h] = s_new[h]


def _rw_scan(r, lw, kd, v, kk, bb, n_ctx):
    bsz, t, width = r.shape
    heads = width // HEAD
    nc = t // CH
    ncc = n_ctx // CH
    tok = lambda d, b, i: (b, _scan_chunk(d, i, ncc, nc), 0)
    dtok = lambda d, b, i: (d, b, _scan_chunk(d, i, ncc, nc), 0)
    s1 = pl.BlockSpec((1, CH, width), tok)
    s2 = pl.BlockSpec((1, 1, CH, width), dtok)
    return pl.pallas_call(
        functools.partial(_rw_scan_kernel, heads=heads),
        grid=(2, bsz, nc),
        in_specs=[s1, s2, s2, s1, s1, s2],
        out_specs=s2,
        out_shape=jax.ShapeDtypeStruct((2, bsz, t, width), F32),
        scratch_shapes=[pltpu.VMEM((heads, HEAD, HEAD), F32)],
        compiler_params=pltpu.CompilerParams(
            dimension_semantics=("arbitrary", "arbitrary", "arbitrary"), vmem_limit_bytes=VMEM_LIMIT),
        name="rwkv_scan",
    )(r, lw, kd, v, kk, bb)


HG_LEVELS = 6


def _hg_decay_matrices():
    out = np.zeros((2, HG_LEVELS + 2, CH, CH), np.float32)
    pos = np.arange(CH)
    for d in range(2):
        for t in range(CH):
            if d == 0:
                out[d, 0, t, :t + 1] = 1.0
                out[d, 1, t, t + 1:] = 1.0
            else:
                out[d, 0, t, t:] = 1.0
                out[d, 1, t, :t] = 1.0
            for l in range(1, HG_LEVELS + 1):
                n = 1 << l
                mid = (t // n) * n + n // 2
                if d == 0:
                    sel = (pos >= mid) & (pos <= t) if t >= mid else (pos > t) & (pos < mid)
                else:
                    sel = (pos >= t) & (pos < mid) if t < mid else (pos >= mid) & (pos < t)
                out[d, 1 + l, t, sel] = 1.0
    return jnp.asarray(out.reshape(2, (HG_LEVELS + 2) * CH, CH), dtype=BF16)


def _hg_scan_kernel(q_ref, f_ref, i_ref, lb_ref, dm_ref, o_ref, s_ref, *, heads):
    d = pl.program_id(0)
    step = pl.program_id(2)

    @pl.when(step == 0)
    def _():
        s_ref[...] = jnp.zeros_like(s_ref)

    row = lax.broadcasted_iota(jnp.int32, (CH, CH), 0)
    col = lax.broadcasted_iota(jnp.int32, (CH, CH), 1)
    sgn = jnp.where(d == 0, 1, -1)

    lb = lb_ref[...]
    f = f_ref[0]
    q = q_ref[0]
    q = q * _sigmoid(q)
    log_sig = jnp.minimum(f, 0.0) - jnp.log(1.0 + jnp.exp(-jnp.abs(f)))
    hi_term = jnp.log(1.0 - lb) + log_sig
    lo_term = jnp.log(jnp.where(lb > 0.0, lb, 1.0))
    mx = jnp.maximum(hi_term, lo_term)
    lf_mix = mx + jnp.log(1.0 + jnp.exp(-jnp.abs(hi_term - lo_term)))
    lf = jnp.where(lb > 0.0, lf_mix, hi_term)
    key = (1.0 - lb) * _sigmoid(-f)

    ex = jnp.exp(_dot_sel(dm_ref[0], lf))
    q_in = (q * ex[0:CH]).astype(BF16)
    k_out = (key * ex[CH:2 * CH]).astype(BF16)
    etot = jnp.exp(jnp.sum(lf, axis=0, keepdims=True))
    q_b = q.astype(BF16)
    k_b = key.astype(BF16)
    v_all = i_ref[0].astype(BF16)

    masks = _half_block_masks(row, col, sgn)
    diag = row == col

    q_lvl = [q_b] + [(q * ex[(1 + l) * CH:(2 + l) * CH]).astype(BF16) for l in range(1, HG_LEVELS + 1)]
    k_lvl = [k_b] + [(key * ex[(1 + l) * CH:(2 + l) * CH]).astype(BF16) for l in range(1, HG_LEVELS + 1)]
    lvl_masks = [diag] + masks
    hs = range(heads)
    sls = [slice(h * HEAD, (h + 1) * HEAD) for h in hs]
    s0 = [s_ref[h] for h in hs]
    sc = [jnp.zeros((CH, CH), F32) for _ in hs]
    for l in range(HG_LEVELS + 1):
        sc = [sc[h] + jnp.where(lvl_masks[l], _dot_nt(q_lvl[l][:, sls[h]], k_lvl[l][:, sls[h]]), 0.0) for h in hs]
    o = [_dot(sc[h].astype(BF16), v_all[:, sls[h]]) + _dot_nt(q_in[:, sls[h]], s0[h].astype(BF16)) for h in hs]
    s_new = [s0[h] * etot[:, sls[h]] + _dot_tn(v_all[:, sls[h]], k_out[:, sls[h]]) for h in hs]
    for h in hs:
        o_ref[0, 0, :, sls[h]] = o[h]
        s_ref[h] = s_new[h]


def _hg_scan(p_hg, lb, n_ctx, col0=0):
    bsz, t, _ = p_hg.shape
    width = lb.shape[0]
    heads = width // HEAD
    nc = t // CH
    ncc = n_ctx // CH
    dm = _hg_decay_matrices()
    chunk = lambda d, i: _scan_chunk(d, i, ncc, nc)
    blk = (1, CH, width)
    return pl.pallas_call(
        functools.partial(_hg_scan_kernel, heads=heads),
        grid=(2, bsz, nc),
        in_specs=[
            pl.BlockSpec(blk, lambda d, b, i: (b, chunk(d, i), col0)),
            pl.BlockSpec(blk, lambda d, b, i: (b, chunk(d, i), col0 + 1 + d)),
            pl.BlockSpec(blk, lambda d, b, i: (b, chunk(d, i), col0 + 3)),
            pl.BlockSpec((1, width), lambda d, b, i: (0, 0)),
            pl.BlockSpec((1,) + dm.shape[1:], lambda d, b, i: (d, 0, 0)),
        ],
        out_specs=pl.BlockSpec((1, 1, CH, width), lambda d, b, i: (d, b, chunk(d, i), 0)),
        out_shape=jax.ShapeDtypeStruct((2, bsz, t, width), F32),
        scratch_shapes=[pltpu.VMEM((heads, HEAD, HEAD), F32)],
        compiler_params=pltpu.CompilerParams(
            dimension_semantics=("arbitrary", "arbitrary", "arbitrary"), vmem_limit_bytes=VMEM_LIMIT),
        name="hgrn2_scan",
    )(p_hg, p_hg, p_hg, lb.reshape(1, width), dm)


def _s5_operators(lam_re, lam_im, log_dt, b_re, b_im, c_re, c_im):
    c = S5_CHUNK
    hp = lax.Precision.HIGHEST
    dt = jnp.exp(log_dt.astype(F32))[..., None]
    lam_re = lam_re.astype(F32)
    lam_im = lam_im.astype(F32)
    mag = jnp.exp(lam_re * dt)
    ang = lam_im * dt
    a_re, a_im = mag * jnp.cos(ang), mag * jnp.sin(ang)
    den = lam_re * lam_re + lam_im * lam_im
    f_re = ((a_re - 1.0) * lam_re + a_im * lam_im) / den
    f_im = (a_im * lam_re - (a_re - 1.0) * lam_im) / den
    bb_re = f_re[..., None] * b_re - f_im[..., None] * b_im
    bb_im = f_re[..., None] * b_im + f_im[..., None] * b_re
    j = jnp.arange(c + 1, dtype=F32)[None, None, :, None]
    pmag = jnp.exp((lam_re * dt)[:, :, None, :] * j)
    pang = ang[:, :, None, :] * j
    pw_re, pw_im = pmag * jnp.cos(pang), pmag * jnp.sin(pang)
    ca_re = c_re[:, :, None] * pw_re[:, :, :, None] - c_im[:, :, None] * pw_im[:, :, :, None]
    ca_im = c_re[:, :, None] * pw_im[:, :, :, None] + c_im[:, :, None] * pw_re[:, :, :, None]
    kern = (jnp.einsum('dgjcp,dgpk->dgjck', ca_re, bb_re, precision=hp)
            - jnp.einsum('dgjcp,dgpk->dgjck', ca_im, bb_im, precision=hp))
    tpos = np.arange(c)
    lag = tpos[None, :] - tpos[:, None]
    kf = jnp.where((lag >= 0)[None, :, :, None, None], kern[0][:, np.clip(lag, 0, c)], 0.0)
    kb = jnp.where((lag <= 0)[None, :, :, None, None], kern[1][:, np.clip(-lag, 0, c)], 0.0)
    g = kern.shape[1]
    tmat = jnp.transpose(kf + kb, (0, 1, 4, 2, 3)).reshape(g, c * S5_CH, c * S5_CH)

    def in_to_state(pr, pi, d):
        hr = pr[:, :, None, :] * jnp.swapaxes(bb_re[d], 1, 2)[:, None] - pi[:, :, None, :] * jnp.swapaxes(bb_im[d], 1, 2)[:, None]
        hi = pr[:, :, None, :] * jnp.swapaxes(bb_im[d], 1, 2)[:, None] + pi[:, :, None, :] * jnp.swapaxes(bb_re[d], 1, 2)[:, None]
        return hr.reshape(g, c * S5_CH, -1), hi.reshape(g, c * S5_CH, -1)

    hf_re, hf_im = in_to_state(pw_re[0, :, c - 1::-1][:, :c], pw_im[0, :, c - 1::-1][:, :c], 0)
    hb_re, hb_im = in_to_state(pw_re[1, :, :c], pw_im[1, :, :c], 1)
    hmat = jnp.concatenate([hf_re, hb_re, hf_im, hb_im], axis=-1)

    def state_to_out(cr, ci):
        return (jnp.transpose(cr, (0, 3, 1, 2)).reshape(g, -1, c * S5_CH),
                -jnp.transpose(ci, (0, 3, 1, 2)).reshape(g, -1, c * S5_CH))

    gf_re, gf_im = state_to_out(ca_re[0, :, 1:c + 1], ca_im[0, :, 1:c + 1])
    gb_re, gb_im = state_to_out(ca_re[1, :, c:0:-1], ca_im[1, :, c:0:-1])
    z = jnp.zeros_like(gf_re)
    gf = jnp.concatenate([gf_re, z, gf_im, z], axis=1)
    gb = jnp.concatenate([z, gb_re, z, gb_im], axis=1)
    ac_re = jnp.concatenate([pw_re[0, :, c], pw_re[1, :, c]], axis=-1)[:, None, :]
    ac_im = jnp.concatenate([pw_im[0, :, c], pw_im[1, :, c]], axis=-1)[:, None, :]
    return tmat.astype(BF16), hmat.astype(BF16), gf.astype(BF16), gb.astype(BF16), ac_re, ac_im


def _s5_local_kernel(u_ref, h_ref, loc_ref):
    loc_ref[0] = _dot(u_ref[0].astype(BF16), h_ref[0])


def _s5_state_kernel(loc_ref, ar_ref, ai_ref, sf_ref, sb_ref, *, nc, ncc, bsz):
    ar = ar_ref[0]
    ai = ai_ref[0]
    half = ar.shape[-1]
    is_fwd = lax.broadcasted_iota(jnp.int32, (1, half), 1) < half // 2

    def body(i, carry):
        nb = jnp.where(i < ncc, ncc - 1 - i, nc + ncc - 1 - i)
        new = []
        for b in range(bsz):
            re, im = carry[b]
            st = jnp.concatenate([re, im], axis=1)
            sf_ref[0, b, pl.ds(i, 1), :] = st
            sb_ref[0, b, pl.ds(nb, 1), :] = st
            lf = loc_ref[0, b, pl.ds(i, 1), :]
            lk = loc_ref[0, b, pl.ds(nb, 1), :]
            l_re = jnp.where(is_fwd, lf[:, :half], lk[:, :half])
            l_im = jnp.where(is_fwd, lf[:, half:], lk[:, half:])
            new.append((ar * re - ai * im + l_re, ar * im + ai * re + l_im))
        return tuple(new)

    zero = jnp.zeros((1, half), F32)
    lax.fori_loop(0, nc, body, tuple((zero, zero) for _ in range(bsz)))


def _s5_out_kernel(u_ref, t_ref, sf_ref, gf_ref, sb_ref, gb_ref, d_ref, y_ref):
    u = u_ref[0]
    y = (_dot(u.astype(BF16), t_ref[0]) + _dot(sf_ref[0].astype(BF16), gf_ref[0])
         + _dot(sb_ref[0].astype(BF16), gb_ref[0]) + d_ref[0] * u)
    y_ref[0] = 0.5 * y * (1.0 + jnp.tanh(math.sqrt(2.0 / math.pi) * (y + 0.044715 * (y * y * y))))


def _s5_scan(p_s5, n_ctx, lam_re, lam_im, log_dt, b_re, b_im, c_re, c_im, d_skip):
    bsz, t, width = p_s5.shape
    g = width // S5_CH
    c = S5_CHUNK
    nc, ncc = t // c, n_ctx // c
    blk = c * S5_CH
    rows = bsz * nc
    tmat, hmat, gf, gb, ac_re, ac_im = _s5_operators(lam_re, lam_im, log_dt, b_re, b_im, c_re, c_im)
    u = jnp.transpose(p_s5.astype(F32).reshape(bsz, nc, c, g, S5_CH), (3, 0, 1, 2, 4)).reshape(g, rows, blk)
    per_g = lambda gi: (gi, 0, 0)
    cparams = pltpu.CompilerParams(dimension_semantics=("arbitrary",), vmem_limit_bytes=VMEM_LIMIT)
    loc = pl.pallas_call(
        _s5_local_kernel, grid=(g,),
        in_specs=[pl.BlockSpec((1, rows, blk), per_g), pl.BlockSpec((1, blk, 256), per_g)],
        out_specs=pl.BlockSpec((1, rows, 256), per_g),
        out_shape=jax.ShapeDtypeStruct((g, rows, 256), F32),
        compiler_params=cparams, name="s5_local",
    )(u, hmat)
    per_g4 = lambda gi: (gi, 0, 0, 0)
    st_shape = jax.ShapeDtypeStruct((g, bsz, nc, 256), F32)
    sf, sb = pl.pallas_call(
        functools.partial(_s5_state_kernel, nc=nc, ncc=ncc, bsz=bsz), grid=(g,),
        in_specs=[pl.BlockSpec((1, bsz, nc, 256), per_g4), pl.BlockSpec((1, 1, 128), per_g),
                  pl.BlockSpec((1, 1, 128), per_g)],
        out_specs=[pl.BlockSpec((1, bsz, nc, 256), per_g4)] * 2,
        out_shape=[st_shape, st_shape],
        compiler_params=cparams, name="s5_state",
    )(loc.reshape(g, bsz, nc, 256), ac_re, ac_im)
    d_flat = jnp.tile(d_skip.astype(F32).reshape(g, 1, S5_CH), (1, c, 1)).reshape(g, 1, blk)
    y = pl.pallas_call(
        _s5_out_kernel, grid=(g,),
        in_specs=[pl.BlockSpec((1, rows, blk), per_g), pl.BlockSpec((1, blk, blk), per_g),
                  pl.BlockSpec((1, rows, 256), per_g), pl.BlockSpec((1, 256, blk), per_g),
                  pl.BlockSpec((1, rows, 256), per_g), pl.BlockSpec((1, 256, blk), per_g),
                  pl.BlockSpec((1, 1, blk), per_g)],
        out_specs=pl.BlockSpec((1, rows, blk), per_g),
        out_shape=jax.ShapeDtypeStruct((g, rows, blk), F32),
        compiler_params=cparams, name="s5_out",
    )(u, tmat, sf.reshape(g, rows, 256), gf, sb.reshape(g, rows, 256), gb, d_flat)
    return jnp.transpose(y.reshape(g, bsz, nc, c, S5_CH), (1, 2, 3, 0, 4)).reshape(bsz, t, width)


IN_BM = 640
IN_BN = 768
OUT_BM = 128
ROUTER_PAD = 128


def _row_block(t, target):
    bm = target
    while t % bm:
        bm -= CH
    return bm


def _norm_mod(x, mod_ref, first_row, n_ctx):
    xn = x * lax.rsqrt(jnp.mean(x * x, axis=-1, keepdims=True) + NORM_EPS)
    is_ctx = (first_row + lax.broadcasted_iota(jnp.int32, (x.shape[0], 1), 0)) < n_ctx
    scale = jnp.where(is_ctx, mod_ref[0, 0, 0:1, :], mod_ref[0, 1, 0:1, :])
    shift = jnp.where(is_ctx, mod_ref[0, 0, 1:2, :], mod_ref[0, 1, 1:2, :])
    return xn * scale + shift


def _in_proj_kernel(x_ref, mod_ref, w_ref, p_ref, h_ref, *, n_ctx, bm):
    @pl.when(pl.program_id(2) == 0)
    def _():
        h_ref[...] = _norm_mod(x_ref[0], mod_ref, pl.program_id(1) * bm, n_ctx).astype(BF16)

    p_ref[0] = _dot(h_ref[...], w_ref[...])


def _in_proj(tok, mod, w, n_ctx):
    bsz, t, d = tok.shape
    n = w.shape[1]
    bm = _row_block(t, IN_BM)
    return pl.pallas_call(
        functools.partial(_in_proj_kernel, n_ctx=n_ctx, bm=bm),
        grid=(bsz, t // bm, n // IN_BN),
        in_specs=[pl.BlockSpec((1, bm, d), lambda b, i, j: (b, i, 0)),
                  pl.BlockSpec((1, 2, 2, d), lambda b, i, j: (b, 0, 0, 0)),
                  pl.BlockSpec((d, IN_BN), lambda b, i, j: (0, j))],
        out_specs=pl.BlockSpec((1, bm, IN_BN), lambda b, i, j: (b, i, j)),
        out_shape=jax.ShapeDtypeStruct((bsz, t, n), F32),
        scratch_shapes=[pltpu.VMEM((bm, d), BF16)],
        compiler_params=pltpu.CompilerParams(
            dimension_semantics=("arbitrary", "arbitrary", "arbitrary"), vmem_limit_bytes=VMEM_LIMIT),
        name="in_proj",
    )(tok, mod, w)


def _out_proj_kernel(s5_ref, wglu_ref, bglu_ref,
                     y2_ref, r_ref, kd_ref, v_ref, g_ref, rk_ref, lnw_ref, lnb_ref,
                     o2_ref, og_ref, hgn_ref, bd_ref,
                     x_ref, gate_ref, wout_ref, modf_ref, wr_ref, br_ref,
                     xo_ref, h_ref, lg_ref, *, n_ctx, bm):
    first = pl.program_id(1) * bm
    bd = bd_ref[...]
    inv_n = 1.0 / HEAD

    def head_sum(z):
        return _dot(z.astype(BF16), bd)

    ys = s5_ref[0]
    ys = ys * _sigmoid(_dot(ys.astype(BF16), wglu_ref[...]) + bglu_ref[...])
    y = y2_ref[0, 0] + y2_ref[1, 0]
    yc = y - head_sum(y) * inv_n
    var = head_sum(yc * yc) * inv_n
    k_bonus = 0.5 * (kd_ref[0, 0] + kd_ref[1, 0])
    bonus = head_sum(r_ref[0] * k_bonus * rk_ref[...]) * v_ref[0]
    yr = (yc * lax.rsqrt(var + RW_GN_EPS) * lnw_ref[...] + lnb_ref[...] + bonus) * g_ref[0]
    o = o2_ref[0, 0] + o2_ref[1, 0]
    yh = o * lax.rsqrt(head_sum(o * o) * inv_n + NORM_EPS) * hgn_ref[...] * _sigmoid(og_ref[0])

    m = _dot(jnp.concatenate([ys, yr, yh], axis=-1).astype(BF16), wout_ref[...])
    is_ctx = (first + lax.broadcasted_iota(jnp.int32, (bm, 1), 0)) < n_ctx
    x_new = x_ref[0] + jnp.where(is_ctx, gate_ref[0, 0:1, :], gate_ref[0, 1:2, :]) * m
    xo_ref[0] = x_new
    h = _norm_mod(x_new, modf_ref, first, n_ctx).astype(BF16)
    h_ref[0] = h
    lg_ref[0] = _dot(h, wr_ref[...]) + br_ref[...]


def _out_proj(s5y, w_glu, b_glu, y2, r, kd, v, g, r_k, ln_w, ln_b, o2, p, og_col, hg_norm,
              tok, gate, w_out, mod_f, w_router, b_router, n_ctx):
    bsz, t, d = tok.shape
    bm = OUT_BM
    s5w = s5y.shape[-1]
    w = r.shape[-1]
    row = lambda b, i: (b, i, 0)
    row2 = lambda b, i: (0, b, i, 0)
    c2 = lambda b, i: (0, 0)
    vec = lambda z: z.reshape(1, -1).astype(F32)
    spec_w = pl.BlockSpec((1, bm, w), row)
    spec_2w = pl.BlockSpec((2, 1, bm, w), row2)
    spec_vw = pl.BlockSpec((1, w), c2)
    return pl.pallas_call(
        functools.partial(_out_proj_kernel, n_ctx=n_ctx, bm=bm),
        grid=(bsz, t // bm),
        in_specs=[
            pl.BlockSpec((1, bm, s5w), row), pl.BlockSpec((s5w, s5w), c2), pl.BlockSpec((1, s5w), c2),
            spec_2w, spec_w, spec_2w, spec_w, spec_w, spec_vw, spec_vw, spec_vw,
            spec_2w, pl.BlockSpec((1, bm, w), lambda b, i: (b, i, og_col)), spec_vw, pl.BlockSpec((w, w), c2),
            pl.BlockSpec((1, bm, d), row), pl.BlockSpec((1, 2, d), lambda b, i: (b, 0, 0)),
            pl.BlockSpec(w_out.shape, c2), pl.BlockSpec((1, 2, 2, d), lambda b, i: (b, 0, 0, 0)),
            pl.BlockSpec((d, ROUTER_PAD), c2), pl.BlockSpec((1, ROUTER_PAD), c2),
        ],
        out_specs=[pl.BlockSpec((1, bm, d), row), pl.BlockSpec((1, bm, d), row),
                   pl.BlockSpec((1, bm, ROUTER_PAD), row)],
        out_shape=[jax.ShapeDtypeStruct((bsz, t, d), F32), jax.ShapeDtypeStruct((bsz, t, d), BF16),
                   jax.ShapeDtypeStruct((bsz, t, ROUTER_PAD), F32)],
        compiler_params=pltpu.CompilerParams(
            dimension_semantics=("arbitrary", "arbitrary"), vmem_limit_bytes=VMEM_LIMIT),
        name="out_proj",
    )(s5y, w_glu.astype(BF16), vec(b_glu), y2, r, kd, v, g, vec(r_k), vec(ln_w), vec(ln_b),
      o2, p, vec(hg_norm), _head_block_diag(w), tok, gate, w_out.astype(BF16), mod_f,
      w_router.astype(BF16), b_router)


def _final_norm_kernel(x_ref, g_ref, o_ref):
    x = x_ref[0]
    o_ref[0] = x * lax.rsqrt(jnp.mean(x * x, axis=-1, keepdims=True) + NORM_EPS) * g_ref[...]


def _final_norm(x, g):
    bsz, t, d = x.shape
    bm = _row_block(t, 512)
    return pl.pallas_call(
        _final_norm_kernel, grid=(bsz, t // bm),
        in_specs=[pl.BlockSpec((1, bm, d), lambda b, i: (b, i, 0)), pl.BlockSpec((1, d), lambda b, i: (0, 0))],
        out_specs=pl.BlockSpec((1, bm, d), lambda b, i: (b, i, 0)),
        out_shape=jax.ShapeDtypeStruct((bsz, t, d), F32),
        compiler_params=pltpu.CompilerParams(dimension_semantics=("arbitrary", "arbitrary")),
        name="final_norm",
    )(x, g.reshape(1, d).astype(F32))


def _moe_kernel(meta_ref, x_ref, w1_ref, w3_ref, w2_ref, y_ref, *, n_blocks):
    active = pl.program_id(0) < meta_ref[n_blocks]

    @pl.when(active)
    def _():
        x = x_ref[...]
        a = _dot(x, w1_ref[0])
        b = _dot(x, w3_ref[0])
        y_ref[...] = _dot((a * _sigmoid(a) * b).astype(BF16), w2_ref[0])

    @pl.when(jnp.logical_not(active))
    def _():
        y_ref[...] = jnp.zeros_like(y_ref)


def _moe_experts(xg, meta, w1, w3, w2):
    n_rows, d = xg.shape
    n_blocks = n_rows // MOE_BLOCK
    de = w1.shape[2]
    return pl.pallas_call(
        functools.partial(_moe_kernel, n_blocks=n_blocks),
        grid_spec=pltpu.PrefetchScalarGridSpec(
            num_scalar_prefetch=1, grid=(n_blocks,),
            in_specs=[pl.BlockSpec((MOE_BLOCK, d), lambda i, m: (i, 0)),
                      pl.BlockSpec((1, d, de), lambda i, m: (m[i], 0, 0)),
                      pl.BlockSpec((1, d, de), lambda i, m: (m[i], 0, 0)),
                      pl.BlockSpec((1, de, d), lambda i, m: (m[i], 0, 0))],
            out_specs=pl.BlockSpec((MOE_BLOCK, d), lambda i, m: (i, 0))),
        out_shape=jax.ShapeDtypeStruct((n_rows, d), F32),
        compiler_params=pltpu.CompilerParams(dimension_semantics=("arbitrary",), vmem_limit_bytes=VMEM_LIMIT),
        name="moe_experts",
    )(meta, xg, w1, w3, w2)


def _moe_ffn(h, logits, n_groups, n_experts, w1, w3, w2):
    n, d = h.shape
    epg = n_experts // n_groups
    group_logits = logits[:, :n_groups]
    g_sel = jnp.argmax(group_logits, axis=-1)
    g_gate = jnp.take_along_axis(jax.nn.softmax(group_logits, axis=-1), g_sel[:, None], axis=1)
    exp_logits = logits[:, n_groups:n_groups + n_experts].reshape(n, n_groups, epg)
    within = jnp.take_along_axis(exp_logits, g_sel[:, None, None], axis=1)[:, 0]
    top_val, top_idx = lax.top_k(within, TOP_K)
    gate = jax.nn.softmax(top_val, axis=-1) * g_gate
    e_flat = (g_sel[:, None] * epg + top_idx).reshape(-1).astype(jnp.int32)
    n_assign = n * TOP_K
    tok_flat = jnp.arange(n_assign, dtype=jnp.int32) // TOP_K
    order = jnp.argsort(e_flat)
    e_s, tok_s = e_flat[order], tok_flat[order]
    counts = jnp.zeros((n_experts,), jnp.int32).at[e_flat].add(1)
    starts = jnp.cumsum(counts) - counts
    padded = (counts + MOE_BLOCK - 1) // MOE_BLOCK * MOE_BLOCK
    pad_ends = jnp.cumsum(padded)
    pad_starts = pad_ends - padded
    dest = pad_starts[e_s] + jnp.arange(n_assign, dtype=jnp.int32) - starts[e_s]
    n_blocks = -(-n_assign // MOE_BLOCK) + n_experts
    n_rows = n_blocks * MOE_BLOCK
    row_tok = jnp.full((n_rows,), n, jnp.int32).at[dest].set(tok_s)
    block_exp = jnp.minimum(jnp.searchsorted(pad_ends, jnp.arange(n_blocks, dtype=jnp.int32) * MOE_BLOCK, side='right'), n_experts - 1)
    meta = jnp.concatenate([block_exp.astype(jnp.int32), (pad_ends[-1:] // MOE_BLOCK).astype(jnp.int32)])
    h_pad = jnp.concatenate([h, jnp.zeros((1, d), h.dtype)], axis=0)
    y_rows = _moe_experts(h_pad[row_tok], meta, w1, w3, w2)
    pos = jnp.zeros((n_assign,), jnp.int32).at[order].set(dest).reshape(n, TOP_K)
    return gate[:, 0:1] * y_rows[pos[:, 0]] + gate[:, 1:2] * y_rows[pos[:, 1]]


def kernel(x, c, ctx, c_ctx, ada_w, ada_b, norm_mix, norm_ffn, w_in, w_out, s5_lam_re, s5_lam_im, s5_log_dt, s5_b_re, s5_b_im, s5_c_re, s5_c_im, s5_d, s5_w_glu, s5_b_glu, rw_mu, rw_w0, rw_w2, rw_a0, rw_a2, rw_g2, rw_k_k, rw_k_a, rw_r_k, rw_ln_w, rw_ln_b, hg_lb_logits, hg_norm, moe_w_group, moe_b_group, moe_w_expert, moe_b_expert, moe_w1, moe_w3, moe_w2, final_norm):
    bsz, n_lat, d = x.shape
    n_ctx = ctx.shape[1]
    depth = w_in.shape[0]
    s5_width = s5_d.shape[1]
    rw_width = rw_k_k.shape[1]
    rw_in = rw_mu.shape[2]
    hg_width = hg_norm.shape[1]
    n_groups = moe_w_group.shape[2]
    n_experts = moe_w_expert.shape[2]
    t = n_ctx + n_lat
    hg_col0 = -(-rw_in // hg_width)
    s5_off = (hg_col0 + 5) * hg_width
    n_proj = -(-(s5_off + s5_width) // IN_BN) * IN_BN
    pad1 = hg_col0 * hg_width - rw_in
    pad2 = n_proj - s5_off - s5_width

    silu_c = jax.nn.silu(c)
    silu_cc = jax.nn.silu(c_ctx)
    lb_all = jnp.cumsum(jax.nn.softmax(hg_lb_logits.astype(F32), axis=0), axis=0)
    lb_all = lb_all - lb_all[0:1]
    tok = jnp.concatenate([ctx, x], axis=1)
    for l in range(depth):
        last = l == depth - 1
        mod_lat = jnp.split(silu_c @ ada_w[l] + ada_b[l], 6, axis=-1)
        mod_ctx = [jnp.broadcast_to(z, (bsz, d)) for z in jnp.split(silu_cc @ ada_w[l] + ada_b[l], 6, axis=-1)]
        sh_m, sc_m, gt_m, sh_f, sc_f, gt_f = [jnp.stack([zc, zl], axis=1) for zc, zl in zip(mod_ctx, mod_lat)]
        mod_m = jnp.stack([norm_mix[l] * (1.0 + sc_m), sh_m], axis=2)
        mod_f = jnp.stack([norm_ffn[l] * (1.0 + sc_f), sh_f], axis=2)

        w_cols = [w_in[l][:, s5_width:s5_width + rw_in], jnp.zeros((d, pad1), F32),
                  w_in[l][:, s5_width + rw_in:], w_in[l][:, :s5_width], jnp.zeros((d, pad2), F32)]
        p = _in_proj(tok, mod_m, jnp.concatenate(w_cols, axis=1).astype(BF16), n_ctx)

        r, v, kk, g, lw, kd, bb = _rw_prep(p, n_ctx, rw_mu[l], rw_w0[l], rw_w2[l], rw_a0[l], rw_a2[l],
                                           rw_g2[l], rw_k_k[l], rw_k_a[l])
        y2 = _rw_scan(r, lw, kd, v, kk, bb, n_ctx)
        o2 = _hg_scan(p, lb_all[l], n_ctx, col0=hg_col0)
        s5y = _s5_scan(p[:, :, s5_off:s5_off + s5_width], n_ctx, s5_lam_re[l], s5_lam_im[l], s5_log_dt[l],
                       s5_b_re[l], s5_b_im[l], s5_c_re[l], s5_c_im[l], s5_d[l])

        w_router = jnp.concatenate([moe_w_group[l], moe_w_expert[l],
                                    jnp.zeros((d, ROUTER_PAD - n_groups - n_experts), F32)], axis=1)
        b_router = jnp.concatenate([moe_b_group[l], moe_b_expert[l],
                                    jnp.zeros((ROUTER_PAD - n_groups - n_experts,), F32)]).reshape(1, ROUTER_PAD)
        tok, h_ffn, logits = _out_proj(
            s5y, s5_w_glu[l], s5_b_glu[l], y2, r, kd, v, g, rw_r_k[l], rw_ln_w[l], rw_ln_b[l],
            o2, p, hg_col0 + 4, hg_norm[l], tok, gt_m, w_out[l], mod_f, w_router, b_router, n_ctx)

        experts = (moe_w1[l].astype(BF16), moe_w3[l].astype(BF16), moe_w2[l].astype(BF16))
        if last:
            f = _moe_ffn(h_ffn[:, n_ctx:].reshape(bsz * n_lat, d), logits[:, n_ctx:].reshape(bsz * n_lat, ROUTER_PAD),
                         n_groups, n_experts, *experts).reshape(bsz, n_lat, d)
            x = tok[:, n_ctx:] + gt_f[:, 1:2] * f
        else:
            f = _moe_ffn(h_ffn.reshape(bsz * t, d), logits.reshape(bsz * t, ROUTER_PAD),
                         n_groups, n_experts, *experts).reshape(bsz, t, d)
            tok = jnp.concatenate([tok[:, :n_ctx] + gt_f[:, 0:1] * f[:, :n_ctx],
                                   tok[:, n_ctx:] + gt_f[:, 1:2] * f[:, n_ctx:]], axis=1)
    return _final_norm(x, final_norm)
```

```python
import functools
import math

import numpy as np
import jax
import jax.numpy as jnp
from jax import lax
from jax.experimental import pallas as pl
from jax.experimental.pallas import tpu as pltpu

F32 = jnp.float32
BF16 = jnp.bfloat16

GRID_W = 64
CH = GRID_W
HEAD = 64
S5_CH = 16
S5_STATE = 64
S5_CHUNK = 32
RW_LORA = 64
RW_GN_EPS = 64e-5
NORM_EPS = 1e-6
TOP_K = 2
MOE_BLOCK = 256
VMEM_LIMIT = 48 * 1024 * 1024


def _dot(a, b):
    return jnp.dot(a, b, preferred_element_type=F32)


def _dot_nt(a, b):
    return lax.dot_general(a, b, (((1,), (1,)), ((), ())), preferred_element_type=F32)


def _dot_tn(a, b):
    return lax.dot_general(a, b, (((0,), (0,)), ((), ())), preferred_element_type=F32)


def _split3(x):
    hi = x.astype(BF16)
    r1 = x - hi.astype(F32)
    mid = r1.astype(BF16)
    lo = (r1 - mid.astype(F32)).astype(BF16)
    return hi, mid, lo


def _dot_sel(m01, x):
    hi, mid, lo = _split3(x)
    return _dot(m01, hi) + _dot(m01, mid) + _dot(m01, lo)


def _dot_hp(a, b):
    a_hi = a.astype(BF16)
    a_lo = (a - a_hi.astype(F32)).astype(BF16)
    b_hi = b.astype(BF16)
    b_lo = (b - b_hi.astype(F32)).astype(BF16)
    return _dot(a_hi, b_hi) + _dot(a_hi, b_lo) + _dot(a_lo, b_hi)


def _sigmoid(x):
    return 1.0 / (1.0 + jnp.exp(-x))


def _scan_chunk(d, i, ncc, nc):
    back = jnp.where(i < ncc, ncc - 1 - i, nc + ncc - 1 - i)
    return jnp.where(d == 0, i, back)


def _half_block_masks(row, col, sgn):
    masks = []
    n = 2
    while n <= CH:
        same = (row // n) == (col // n)
        t_hi = ((row % n) >= n // 2).astype(jnp.int32)
        s_hi = ((col % n) >= n // 2).astype(jnp.int32)
        masks.append(jnp.logical_and(same, (t_hi - s_hi) * sgn == 1))
        n *= 2
    return masks


def _head_block_diag(width):
    idx = np.arange(width) // HEAD
    return jnp.asarray(idx[:, None] == idx[None, :], dtype=BF16)


def _rw_prep_kernel(cur_ref, prev_ref, next_ref, mu_ref, w0_ref, w2_ref, a0_ref, a2_ref, g2_ref,
                    kkw_ref, ka_ref, bd_ref,
                    r_out, v_out, kk_out, g_out, lw_out, kd_out, bb_out, *, ncc, nc, width):
    n = pl.program_id(1)
    cur = cur_ref[0]
    prv = prev_ref[0]
    nxt = next_ref[0]
    is_ctx = n < ncc
    row = lax.broadcasted_iota(jnp.int32, (CH, 1), 0)
    zero_row = jnp.zeros((1, cur.shape[1]), F32)
    first_fill = jnp.where(jnp.logical_and(is_ctx, n > 0), prv[CH - 1:CH, :], zero_row)
    last_fill = jnp.where(jnp.logical_and(is_ctx, n < ncc - 1), nxt[0:1, :], zero_row)
    nb_prev = jnp.where(row == 0, first_fill, pltpu.roll(cur, 1, 0))
    nb_next = jnp.where(row == CH - 1, last_fill, pltpu.roll(cur, CH - 1, 0))
    zeros = jnp.zeros_like(cur)
    up = jnp.where(is_ctx, cur, jnp.where(n > ncc, prv, zeros))
    down = jnp.where(is_ctx, cur, jnp.where(n < nc - 1, nxt, zeros))
    mu = mu_ref[...]
    x = (cur + mu[0:1] * (nb_prev - cur) + mu[1:2] * (nb_next - cur)
         + mu[2:3] * (up - cur) + mu[3:4] * (down - cur))

    w = width
    r = x[:, 0:w]
    k = x[:, w:2 * w]
    v = x[:, 2 * w:3 * w]
    wd = x[:, 3 * w:3 * w + 2 * RW_LORA]
    ad = x[:, 3 * w + 2 * RW_LORA:3 * w + 4 * RW_LORA]
    gd = x[:, 3 * w + 4 * RW_LORA:]

    r_out[0] = r.astype(r_out.dtype)
    v_out[0] = v.astype(v_out.dtype)
    g_out[0] = _dot_hp(_sigmoid(gd), g2_ref[...]).astype(g_out.dtype)
    kk = k * kkw_ref[...]
    kk2 = kk * kk
    kk2_hi = kk2.astype(BF16)
    kk2_lo = (kk2 - kk2_hi.astype(F32)).astype(BF16)
    ss = _dot(kk2_hi, bd_ref[...]) + _dot(kk2_lo, bd_ref[...])
    kk = kk / jnp.maximum(jnp.sqrt(ss), 1e-12)
    kk_out[0] = kk.astype(kk_out.dtype)
    ka = ka_ref[...]
    for d in range(2):
        w_raw = w0_ref[d:d + 1, :] + _dot_hp(jnp.tanh(wd[:, d * RW_LORA:(d + 1) * RW_LORA]), w2_ref[d])
        nw = -w_raw
        softplus = jnp.maximum(nw, 0.0) + jnp.log(1.0 + jnp.exp(-jnp.abs(nw)))
        lw_out[d, 0] = -jnp.exp(-softplus - 0.5)
        a = _sigmoid(a0_ref[d:d + 1, :] + _dot_hp(ad[:, d * RW_LORA:(d + 1) * RW_LORA], a2_ref[d]))
        kd_out[d, 0] = (k * (1.0 + (a - 1.0) * ka)).astype(kd_out.dtype)
        bb_out[d, 0] = (kk * a).astype(bb_out.dtype)


def _rw_prep(p_rw, n_ctx, mu, w0, w2, a0, a2, g2, k_k, k_a):
    bsz, t, _ = p_rw.shape
    win = mu.shape[1]
    width = k_k.shape[0]
    nc = t // CH
    ncc = n_ctx // CH
    tok = lambda b, n: (b, n, 0)
    full2 = lambda b, n: (0, 0)
    full3 = lambda b, n: (0, 0, 0)
    blk_in = (1, CH, win)
    out1 = jax.ShapeDtypeStruct((bsz, t, width), BF16)
    out2 = jax.ShapeDtypeStruct((2, bsz, t, width), BF16)
    out_lw = jax.ShapeDtypeStruct((2, bsz, t, width), F32)
    spec1 = pl.BlockSpec((1, CH, width), tok)
    spec2 = pl.BlockSpec((2, 1, CH, width), lambda b, n: (0, b, n, 0))
    return pl.pallas_call(
        functools.partial(_rw_prep_kernel, ncc=ncc, nc=nc, width=width),
        grid=(bsz, nc),
        in_specs=[
            pl.BlockSpec(blk_in, tok),
            pl.BlockSpec(blk_in, lambda b, n: (b, jnp.maximum(n - 1, 0), 0)),
            pl.BlockSpec(blk_in, lambda b, n: (b, jnp.minimum(n + 1, nc - 1), 0)),
            pl.BlockSpec(mu.shape, full2),
            pl.BlockSpec(w0.shape, full2),
            pl.BlockSpec(w2.shape, full3),
            pl.BlockSpec(a0.shape, full2),
            pl.BlockSpec(a2.shape, full3),
            pl.BlockSpec(g2.shape, full2),
            pl.BlockSpec((1, width), full2),
            pl.BlockSpec((1, width), full2),
            pl.BlockSpec((width, width), full2),
        ],
        out_specs=[spec1, spec1, spec1, spec1, spec2, spec2, spec2],
        out_shape=[out1, out1, out1, out1, out_lw, out2, out2],
        compiler_params=pltpu.CompilerParams(
            dimension_semantics=("arbitrary", "arbitrary"), vmem_limit_bytes=VMEM_LIMIT),
        name="rwkv_prep",
    )(p_rw, p_rw, p_rw, mu, w0, w2, a0, a2, g2, k_k.reshape(1, width), k_a.reshape(1, width),
      _head_block_diag(width))


def _rw_scan_kernel(r_ref, lw_ref, kd_ref, v_ref, kk_ref, bb_ref, y_ref, s_ref, *, heads, bsz):
    d = pl.program_id(0)
    i = pl.program_id(1)

    @pl.when(i == 0)
    def _():
        s_ref[...] = jnp.zeros_like(s_ref)

    row = lax.broadcasted_iota(jnp.int32, (CH, CH), 0)
    col = lax.broadcasted_iota(jnp.int32, (CH, CH), 1)
    sgn = jnp.where(d == 0, 1, -1)
    strict = (row - col) * sgn > 0
    diag = row == col
    incl = jnp.logical_or(strict, diag)
    tri = jnp.where(incl, 1.0, 0.0).astype(BF16)
    eye = jnp.where(diag, 1.0, 0.0)
    halves = _half_block_masks(row, col, sgn)

    hs = range(bsz * heads)
    sls = [slice(h * HEAD, (h + 1) * HEAD) for h in range(heads)] * bsz
    s0 = [s_ref[h] for h in hs]
    v, kkr, bk, b_w, k_w, wtot = [], [], [], [], [], []
    for b in range(bsz):
        lw = lw_ref[0, b]
        cw = _dot_sel(tri, lw)
        ctot = jnp.sum(lw, axis=0, keepdims=True)
        e_ncw = jnp.exp(-cw)
        e_rem = jnp.exp(ctot - cw)
        r_t = (r_ref[b].astype(F32) * jnp.exp(cw)).astype(BF16)
        kk_t = (kk_ref[b].astype(F32) * jnp.exp(cw - lw)).astype(BF16)
        bb = bb_ref[0, b].astype(F32)
        kd = kd_ref[0, b].astype(F32)
        b_t = (bb * e_ncw).astype(BF16)
        k_t = (kd * e_ncw).astype(BF16)
        bw_all = (bb * e_rem).astype(BF16)
        kw_all = (kd * e_rem).astype(BF16)
        wt_all = jnp.exp(ctot)
        v_all = v_ref[b]
        for sl in sls[:heads]:
            v.append(v_all[:, sl])
            kkr.append(jnp.concatenate([kk_t[:, sl], r_t[:, sl]], axis=0))
            bk.append(jnp.concatenate([b_t[:, sl], k_t[:, sl]], axis=0))
            b_w.append(bw_all[:, sl])
            k_w.append(kw_all[:, sl])
            wtot.append(wt_all[:, sl])
    att = [_dot_nt(kkr[h], bk[h]) for h in hs]
    proj = [_dot_nt(kkr[h], s0[h].astype(BF16)) for h in hs]
    a_ab = [jnp.where(strict, att[h][:CH, :CH], 0.0) for h in hs]
    a_ak = [jnp.where(strict, att[h][:CH, CH:], 0.0).astype(BF16) for h in hs]
    a_rb = [jnp.where(incl, att[h][CH:, :CH], 0.0).astype(BF16) for h in hs]
    a_rk = [jnp.where(incl, att[h][CH:, CH:], 0.0).astype(BF16) for h in hs]
    inv = [eye - jnp.where(halves[0], a_ab[h], 0.0) for h in hs]
    for lvl in range(1, len(halves)):
        inv_b = [inv[h].astype(BF16) for h in hs]
        a_l = [jnp.where(halves[lvl], a_ab[h], 0.0).astype(BF16) for h in hs]
        left = [_dot(inv_b[h], a_l[h]).astype(BF16) for h in hs]
        inv = [inv[h] - _dot(left[h], inv_b[h]) for h in hs]
    rhs = [(proj[h][:CH] + _dot(a_ak[h], v[h])).astype(BF16) for h in hs]
    u_b = [(-_dot(inv[h].astype(BF16), rhs[h])).astype(BF16) for h in hs]
    y = [proj[h][CH:] + _dot(a_rb[h], u_b[h]) + _dot(a_rk[h], v[h]) for h in hs]
    s_new = [s0[h] * wtot[h] + _dot_tn(u_b[h], b_w[h]) + _dot_tn(v[h], k_w[h]) for h in hs]
    for h in hs:
        y_ref[0, h // heads, :, sls[h]] = y[h].astype(y_ref.dtype)
        s_ref[h] = s_new[h]


def _rw_scan(r, lw, kd, v, kk, bb, n_ctx):
    bsz, t, width = r.shape
    heads = width // HEAD
    nc = t // CH
    ncc = n_ctx // CH
    s1 = pl.BlockSpec((bsz, CH, width), lambda d, i: (0, _scan_chunk(d, i, ncc, nc), 0))
    s2 = pl.BlockSpec((1, bsz, CH, width), lambda d, i: (d, 0, _scan_chunk(d, i, ncc, nc), 0))
    return pl.pallas_call(
        functools.partial(_rw_scan_kernel, heads=heads, bsz=bsz),
        grid=(2, nc),
        in_specs=[s1, s2, s2, s1, s1, s2],
        out_specs=s2,
        out_shape=jax.ShapeDtypeStruct((2, bsz, t, width), BF16),
        scratch_shapes=[pltpu.VMEM((bsz * heads, HEAD, HEAD), F32)],
        compiler_params=pltpu.CompilerParams(
            dimension_semantics=("arbitrary", "arbitrary"), vmem_limit_bytes=VMEM_LIMIT),
        name="rwkv_scan",
    )(r, lw, kd, v, kk, bb)


HG_LEVELS = 6


def _hg_decay_matrices():
    out = np.zeros((2, HG_LEVELS + 2, CH, CH), np.float32)
    pos = np.arange(CH)
    for d in range(2):
        for t in range(CH):
            if d == 0:
                out[d, 0, t, :t + 1] = 1.0
                out[d, 1, t, t + 1:] = 1.0
            else:
                out[d, 0, t, t:] = 1.0
                out[d, 1, t, :t] = 1.0
            for l in range(1, HG_LEVELS + 1):
                n = 1 << l
                mid = (t // n) * n + n // 2
                if d == 0:
                    sel = (pos >= mid) & (pos <= t) if t >= mid else (pos > t) & (pos < mid)
                else:
                    sel = (pos >= t) & (pos < mid) if t < mid else (pos >= mid) & (pos < t)
                out[d, 1 + l, t, sel] = 1.0
    return jnp.asarray(out.reshape(2, (HG_LEVELS + 2) * CH, CH), dtype=BF16)


def _hg_scan_kernel(q_ref, f_ref, i_ref, lb_ref, dm_ref, o_ref, s_ref, *, heads, bsz):
    d = pl.program_id(0)
    step = pl.program_id(1)

    @pl.when(step == 0)
    def _():
        s_ref[...] = jnp.zeros_like(s_ref)

    row = lax.broadcasted_iota(jnp.int32, (CH, CH), 0)
    col = lax.broadcasted_iota(jnp.int32, (CH, CH), 1)
    sgn = jnp.where(d == 0, 1, -1)

    lb = lb_ref[...]
    masks = _half_block_masks(row, col, sgn)
    lvl_masks = [row == col] + masks
    hs = range(bsz * heads)
    sls = [slice(h * HEAD, (h + 1) * HEAD) for h in range(heads)] * bsz
    q_lvl, k_lvl, q_in, k_out, etot, v_all = [], [], [], [], [], []
    for b in range(bsz):
        f = f_ref[b]
        q = q_ref[b]
        q = q * _sigmoid(q)
        log_sig = jnp.minimum(f, 0.0) - jnp.log(1.0 + jnp.exp(-jnp.abs(f)))
        hi_term = jnp.log(1.0 - lb) + log_sig
        lo_term = jnp.log(jnp.where(lb > 0.0, lb, 1.0))
        mx = jnp.maximum(hi_term, lo_term)
        lf_mix = mx + jnp.log(1.0 + jnp.exp(-jnp.abs(hi_term - lo_term)))
        lf = jnp.where(lb > 0.0, lf_mix, hi_term)
        key = (1.0 - lb) * _sigmoid(-f)
        ex = jnp.exp(_dot_sel(dm_ref[0], lf))
        q_in.append((q * ex[0:CH]).astype(BF16))
        k_out.append((key * ex[CH:2 * CH]).astype(BF16))
        etot.append(jnp.exp(jnp.sum(lf, axis=0, keepdims=True)))
        v_all.append(i_ref[b].astype(BF16))
        q_lvl.append([q.astype(BF16)] + [(q * ex[(1 + l) * CH:(2 + l) * CH]).astype(BF16)
                                         for l in range(1, HG_LEVELS + 1)])
        k_lvl.append([key.astype(BF16)] + [(key * ex[(1 + l) * CH:(2 + l) * CH]).astype(BF16)
                                           for l in range(1, HG_LEVELS + 1)])

    bi = [h // heads for h in hs]
    s0 = [s_ref[h] for h in hs]
    sc = [jnp.zeros((CH, CH), F32) for _ in hs]
    for l in range(HG_LEVELS + 1):
        sc = [sc[h] + jnp.where(lvl_masks[l], _dot_nt(q_lvl[bi[h]][l][:, sls[h]], k_lvl[bi[h]][l][:, sls[h]]), 0.0)
              for h in hs]
    o = [_dot(sc[h].astype(BF16), v_all[bi[h]][:, sls[h]]) + _dot_nt(q_in[bi[h]][:, sls[h]], s0[h].astype(BF16))
         for h in hs]
    s_new = [s0[h] * etot[bi[h]][:, sls[h]] + _dot_tn(v_all[bi[h]][:, sls[h]], k_out[bi[h]][:, sls[h]]) for h in hs]
    for h in hs:
        o_ref[0, bi[h], :, sls[h]] = o[h].astype(o_ref.dtype)
        s_ref[h] = s_new[h]


def _hg_scan(p_hg, lb, n_ctx, col0=0):
    bsz, t, _ = p_hg.shape
    width = lb.shape[0]
    heads = width // HEAD
    nc = t // CH
    ncc = n_ctx // CH
    dm = _hg_decay_matrices()
    chunk = lambda d, i: _scan_chunk(d, i, ncc, nc)
    blk = (bsz, CH, width)
    return pl.pallas_call(
        functools.partial(_hg_scan_kernel, heads=heads, bsz=bsz),
        grid=(2, nc),
        in_specs=[
            pl.BlockSpec(blk, lambda d, i: (0, chunk(d, i), col0)),
            pl.BlockSpec(blk, lambda d, i: (0, chunk(d, i), col0 + 1 + d)),
            pl.BlockSpec(blk, lambda d, i: (0, chunk(d, i), col0 + 3)),
            pl.BlockSpec((1, width), lambda d, i: (0, 0)),
            pl.BlockSpec((1,) + dm.shape[1:], lambda d, i: (d, 0, 0)),
        ],
        out_specs=pl.BlockSpec((1, bsz, CH, width), lambda d, i: (d, 0, chunk(d, i), 0)),
        out_shape=jax.ShapeDtypeStruct((2, bsz, t, width), BF16),
        scratch_shapes=[pltpu.VMEM((bsz * heads, HEAD, HEAD), F32)],
        compiler_params=pltpu.CompilerParams(
            dimension_semantics=("arbitrary", "arbitrary"), vmem_limit_bytes=VMEM_LIMIT),
        name="hgrn2_scan",
    )(p_hg, p_hg, p_hg, lb.reshape(1, width), dm)


def _s5_operators(lam_re, lam_im, log_dt, b_re, b_im, c_re, c_im):
    c = S5_CHUNK
    hp = lax.Precision.HIGHEST
    dt = jnp.exp(log_dt.astype(F32))[..., None]
    lam_re = lam_re.astype(F32)
    lam_im = lam_im.astype(F32)
    mag = jnp.exp(lam_re * dt)
    ang = lam_im * dt
    a_re, a_im = mag * jnp.cos(ang), mag * jnp.sin(ang)
    den = lam_re * lam_re + lam_im * lam_im
    f_re = ((a_re - 1.0) * lam_re + a_im * lam_im) / den
    f_im = (a_im * lam_re - (a_re - 1.0) * lam_im) / den
    bb_re = f_re[..., None] * b_re - f_im[..., None] * b_im
    bb_im = f_re[..., None] * b_im + f_im[..., None] * b_re
    j = jnp.arange(c + 1, dtype=F32)[None, None, :, None]
    pmag = jnp.exp((lam_re * dt)[:, :, None, :] * j)
    pang = ang[:, :, None, :] * j
    pw_re, pw_im = pmag * jnp.cos(pang), pmag * jnp.sin(pang)
    ca_re = c_re[:, :, None] * pw_re[:, :, :, None] - c_im[:, :, None] * pw_im[:, :, :, None]
    ca_im = c_re[:, :, None] * pw_im[:, :, :, None] + c_im[:, :, None] * pw_re[:, :, :, None]
    kern = (jnp.einsum('dgjcp,dgpk->dgjck', ca_re, bb_re, precision=hp)
            - jnp.einsum('dgjcp,dgpk->dgjck', ca_im, bb_im, precision=hp))
    tpos = np.arange(c)
    lag = tpos[None, :] - tpos[:, None]
    kf = jnp.where((lag >= 0)[None, :, :, None, None], kern[0][:, np.clip(lag, 0, c)], 0.0)
    kb = jnp.where((lag <= 0)[None, :, :, None, None], kern[1][:, np.clip(-lag, 0, c)], 0.0)
    g = kern.shape[1]
    tmat = jnp.transpose(kf + kb, (0, 1, 4, 2, 3)).reshape(g, c * S5_CH, c * S5_CH)

    def in_to_state(pr, pi, d):
        hr = pr[:, :, None, :] * jnp.swapaxes(bb_re[d], 1, 2)[:, None] - pi[:, :, None, :] * jnp.swapaxes(bb_im[d], 1, 2)[:, None]
        hi = pr[:, :, None, :] * jnp.swapaxes(bb_im[d], 1, 2)[:, None] + pi[:, :, None, :] * jnp.swapaxes(bb_re[d], 1, 2)[:, None]
        return hr.reshape(g, c * S5_CH, -1), hi.reshape(g, c * S5_CH, -1)

    hf_re, hf_im = in_to_state(pw_re[0, :, c - 1::-1][:, :c], pw_im[0, :, c - 1::-1][:, :c], 0)
    hb_re, hb_im = in_to_state(pw_re[1, :, :c], pw_im[1, :, :c], 1)
    hmat = jnp.concatenate([hf_re, hb_re, hf_im, hb_im], axis=-1)

    def state_to_out(cr, ci):
        return (jnp.transpose(cr, (0, 3, 1, 2)).reshape(g, -1, c * S5_CH),
                -jnp.transpose(ci, (0, 3, 1, 2)).reshape(g, -1, c * S5_CH))

    gf_re, gf_im = state_to_out(ca_re[0, :, 1:c + 1], ca_im[0, :, 1:c + 1])
    gb_re, gb_im = state_to_out(ca_re[1, :, c:0:-1], ca_im[1, :, c:0:-1])
    z = jnp.zeros_like(gf_re)
    gf = jnp.concatenate([gf_re, z, gf_im, z], axis=1)
    gb = jnp.concatenate([z, gb_re, z, gb_im], axis=1)
    ac_re = jnp.concatenate([pw_re[0, :, c], pw_re[1, :, c]], axis=-1)[:, None, :]
    ac_im = jnp.concatenate([pw_im[0, :, c], pw_im[1, :, c]], axis=-1)[:, None, :]
    return tmat.astype(BF16), hmat.astype(BF16), gf.astype(BF16), gb.astype(BF16), ac_re, ac_im


def _s5_local_kernel(u_ref, h_ref, loc_ref):
    loc_ref[0] = _dot(u_ref[0].astype(BF16), h_ref[0])


def _s5_state_kernel(loc_ref, ar_ref, ai_ref, sf_ref, sb_ref, *, nc, ncc, bsz):
    ar = ar_ref[0]
    ai = ai_ref[0]
    half = ar.shape[-1]
    is_fwd = lax.broadcasted_iota(jnp.int32, (1, half), 1) < half // 2

    def body(i, carry):
        nb = jnp.where(i < ncc, ncc - 1 - i, nc + ncc - 1 - i)
        new = []
        for b in range(bsz):
            re, im = carry[b]
            st = jnp.concatenate([re, im], axis=1)
            sf_ref[0, b, pl.ds(i, 1), :] = st
            sb_ref[0, b, pl.ds(nb, 1), :] = st
            lf = loc_ref[0, b, pl.ds(i, 1), :]
            lk = loc_ref[0, b, pl.ds(nb, 1), :]
            l_re = jnp.where(is_fwd, lf[:, :half], lk[:, :half])
            l_im = jnp.where(is_fwd, lf[:, half:], lk[:, half:])
            new.append((ar * re - ai * im + l_re, ar * im + ai * re + l_im))
        return tuple(new)

    zero = jnp.zeros((1, half), F32)
    lax.fori_loop(0, nc, body, tuple((zero, zero) for _ in range(bsz)))


def _s5_out_kernel(u_ref, t_ref, sf_ref, gf_ref, sb_ref, gb_ref, d_ref, y_ref):
    u = u_ref[0]
    y = (_dot(u.astype(BF16), t_ref[0]) + _dot(sf_ref[0].astype(BF16), gf_ref[0])
         + _dot(sb_ref[0].astype(BF16), gb_ref[0]) + d_ref[0] * u.astype(F32))
    gelu = 0.5 * y * (1.0 + jnp.tanh(math.sqrt(2.0 / math.pi) * (y + 0.044715 * (y * y * y))))
    y_ref[0] = gelu.astype(y_ref.dtype)


def _s5_scan(p_s5, n_ctx, lam_re, lam_im, log_dt, b_re, b_im, c_re, c_im, d_skip):
    bsz, t, width = p_s5.shape
    g = width // S5_CH
    c = S5_CHUNK
    nc, ncc = t // c, n_ctx // c
    blk = c * S5_CH
    rows = bsz * nc
    tmat, hmat, gf, gb, ac_re, ac_im = _s5_operators(lam_re, lam_im, log_dt, b_re, b_im, c_re, c_im)
    u = jnp.transpose(p_s5.astype(BF16).reshape(bsz, nc, c, g, S5_CH), (3, 0, 1, 2, 4)).reshape(g, rows, blk)
    per_g = lambda gi: (gi, 0, 0)
    cparams = pltpu.CompilerParams(dimension_semantics=("arbitrary",), vmem_limit_bytes=VMEM_LIMIT)
    loc = pl.pallas_call(
        _s5_local_kernel, grid=(g,),
        in_specs=[pl.BlockSpec((1, rows, blk), per_g), pl.BlockSpec((1, blk, 256), per_g)],
        out_specs=pl.BlockSpec((1, rows, 256), per_g),
        out_shape=jax.ShapeDtypeStruct((g, rows, 256), F32),
        compiler_params=cparams, name="s5_local",
    )(u, hmat)
    per_g4 = lambda gi: (gi, 0, 0, 0)
    st_shape = jax.ShapeDtypeStruct((g, bsz, nc, 256), F32)
    sf, sb = pl.pallas_call(
        functools.partial(_s5_state_kernel, nc=nc, ncc=ncc, bsz=bsz), grid=(g,),
        in_specs=[pl.BlockSpec((1, bsz, nc, 256), per_g4), pl.BlockSpec((1, 1, 128), per_g),
                  pl.BlockSpec((1, 1, 128), per_g)],
        out_specs=[pl.BlockSpec((1, bsz, nc, 256), per_g4)] * 2,
        out_shape=[st_shape, st_shape],
        compiler_params=cparams, name="s5_state",
    )(loc.reshape(g, bsz, nc, 256), ac_re, ac_im)
    d_flat = jnp.tile(d_skip.astype(F32).reshape(g, 1, S5_CH), (1, c, 1)).reshape(g, 1, blk)
    y = pl.pallas_call(
        _s5_out_kernel, grid=(g,),
        in_specs=[pl.BlockSpec((1, rows, blk), per_g), pl.BlockSpec((1, blk, blk), per_g),
                  pl.BlockSpec((1, rows, 256), per_g), pl.BlockSpec((1, 256, blk), per_g),
                  pl.BlockSpec((1, rows, 256), per_g), pl.BlockSpec((1, 256, blk), per_g),
                  pl.BlockSpec((1, 1, blk), per_g)],
        out_specs=pl.BlockSpec((1, rows, blk), per_g),
        out_shape=jax.ShapeDtypeStruct((g, rows, blk), BF16),
        compiler_params=cparams, name="s5_out",
    )(u, tmat, sf.reshape(g, rows, 256), gf, sb.reshape(g, rows, 256), gb, d_flat)
    return jnp.transpose(y.reshape(g, bsz, nc, c, S5_CH), (1, 2, 3, 0, 4)).reshape(bsz, t, width)


IN_BM = 640
IN_BN = 768
OUT_BM = 256
ROUTER_PAD = 128


def _row_block(t, target):
    bm = target
    while t % bm:
        bm -= CH
    return bm


def _norm_mod(x, mod_ref, first_row, n_ctx):
    xn = x * lax.rsqrt(jnp.mean(x * x, axis=-1, keepdims=True) + NORM_EPS)
    is_ctx = (first_row + lax.broadcasted_iota(jnp.int32, (x.shape[0], 1), 0)) < n_ctx
    scale = jnp.where(is_ctx, mod_ref[0, 0, 0:1, :], mod_ref[0, 1, 0:1, :])
    shift = jnp.where(is_ctx, mod_ref[0, 0, 1:2, :], mod_ref[0, 1, 1:2, :])
    return xn * scale + shift


def _in_proj_kernel(x_ref, mod_ref, w_ref, p_ref, h_ref, *, n_ctx, bm):
    @pl.when(pl.program_id(2) == 0)
    def _():
        h_ref[...] = _norm_mod(x_ref[0], mod_ref, pl.program_id(1) * bm, n_ctx).astype(BF16)

    p_ref[0] = _dot(h_ref[...], w_ref[...])


def _in_proj(tok, mod, w, n_ctx):
    bsz, t, d = tok.shape
    n = w.shape[1]
    bm = _row_block(t, IN_BM)
    return pl.pallas_call(
        functools.partial(_in_proj_kernel, n_ctx=n_ctx, bm=bm),
        grid=(bsz, t // bm, n // IN_BN),
        in_specs=[pl.BlockSpec((1, bm, d), lambda b, i, j: (b, i, 0)),
                  pl.BlockSpec((1, 2, 2, d), lambda b, i, j: (b, 0, 0, 0)),
                  pl.BlockSpec((d, IN_BN), lambda b, i, j: (0, j))],
        out_specs=pl.BlockSpec((1, bm, IN_BN), lambda b, i, j: (b, i, j)),
        out_shape=jax.ShapeDtypeStruct((bsz, t, n), F32),
        scratch_shapes=[pltpu.VMEM((bm, d), BF16)],
        compiler_params=pltpu.CompilerParams(
            dimension_semantics=("arbitrary", "arbitrary", "arbitrary"), vmem_limit_bytes=VMEM_LIMIT),
        name="in_proj",
    )(tok, mod, w)


def _out_proj_kernel(s5_ref, wglu_ref, bglu_ref,
                     y2_ref, r_ref, kd_ref, v_ref, g_ref, rk_ref, lnw_ref, lnb_ref,
                     o2_ref, og_ref, hgn_ref, bd_ref,
                     x_ref, gate_ref, wout_ref, modf_ref, wr_ref, br_ref,
                     xo_ref, h_ref, lg_ref, *, n_ctx, bm):
    first = pl.program_id(1) * bm
    bd = bd_ref[...]
    inv_n = 1.0 / HEAD

    def head_sum(z):
        return _dot(z.astype(BF16), bd)

    ys = s5_ref[0].astype(F32)
    ys = ys * _sigmoid(_dot(ys.astype(BF16), wglu_ref[...]) + bglu_ref[...])
    y = y2_ref[0, 0].astype(F32) + y2_ref[1, 0].astype(F32)
    yc = y - head_sum(y) * inv_n
    var = head_sum(yc * yc) * inv_n
    k_bonus = 0.5 * (kd_ref[0, 0].astype(F32) + kd_ref[1, 0].astype(F32))
    bonus = head_sum(r_ref[0].astype(F32) * k_bonus * rk_ref[...]) * v_ref[0].astype(F32)
    yr = (yc * lax.rsqrt(var + RW_GN_EPS) * lnw_ref[...] + lnb_ref[...] + bonus) * g_ref[0].astype(F32)
    o = o2_ref[0, 0].astype(F32) + o2_ref[1, 0].astype(F32)
    yh = o * lax.rsqrt(head_sum(o * o) * inv_n + NORM_EPS) * hgn_ref[...] * _sigmoid(og_ref[0])

    m = _dot(jnp.concatenate([ys, yr, yh], axis=-1).astype(BF16), wout_ref[...])
    is_ctx = (first + lax.broadcasted_iota(jnp.int32, (bm, 1), 0)) < n_ctx
    x_new = x_ref[0] + jnp.where(is_ctx, gate_ref[0, 0:1, :], gate_ref[0, 1:2, :]) * m
    xo_ref[0] = x_new
    h = _norm_mod(x_new, modf_ref, first, n_ctx).astype(BF16)
    h_ref[0] = h
    lg_ref[0] = _dot(h, wr_ref[...]) + br_ref[...]


def _out_proj(s5y, w_glu, b_glu, y2, r, kd, v, g, r_k, ln_w, ln_b, o2, p, og_col, hg_norm,
              tok, gate, w_out, mod_f, w_router, b_router, n_ctx):
    bsz, t, d = tok.shape
    bm = _row_block(t, OUT_BM)
    s5w = s5y.shape[-1]
    w = r.shape[-1]
    row = lambda b, i: (b, i, 0)
    row2 = lambda b, i: (0, b, i, 0)
    c2 = lambda b, i: (0, 0)
    vec = lambda z: z.reshape(1, -1).astype(F32)
    spec_w = pl.BlockSpec((1, bm, w), row)
    spec_2w = pl.BlockSpec((2, 1, bm, w), row2)
    spec_vw = pl.BlockSpec((1, w), c2)
    return pl.pallas_call(
        functools.partial(_out_proj_kernel, n_ctx=n_ctx, bm=bm),
        grid=(bsz, t // bm),
        in_specs=[
            pl.BlockSpec((1, bm, s5w), row), pl.BlockSpec((s5w, s5w), c2), pl.BlockSpec((1, s5w), c2),
            spec_2w, spec_w, spec_2w, spec_w, spec_w, spec_vw, spec_vw, spec_vw,
            spec_2w, pl.BlockSpec((1, bm, w), lambda b, i: (b, i, og_col)), spec_vw, pl.BlockSpec((w, w), c2),
            pl.BlockSpec((1, bm, d), row), pl.BlockSpec((1, 2, d), lambda b, i: (b, 0, 0)),
            pl.BlockSpec(w_out.shape, c2), pl.BlockSpec((1, 2, 2, d), lambda b, i: (b, 0, 0, 0)),
            pl.BlockSpec((d, ROUTER_PAD), c2), pl.BlockSpec((1, ROUTER_PAD), c2),
        ],
        out_specs=[pl.BlockSpec((1, bm, d), row), pl.BlockSpec((1, bm, d), row),
                   pl.BlockSpec((1, bm, ROUTER_PAD), row)],
        out_shape=[jax.ShapeDtypeStruct((bsz, t, d), F32), jax.ShapeDtypeStruct((bsz, t, d), BF16),
                   jax.ShapeDtypeStruct((bsz, t, ROUTER_PAD), F32)],
        compiler_params=pltpu.CompilerParams(
            dimension_semantics=("arbitrary", "arbitrary"), vmem_limit_bytes=VMEM_LIMIT),
        name="out_proj",
    )(s5y, w_glu.astype(BF16), vec(b_glu), y2, r, kd, v, g, vec(r_k), vec(ln_w), vec(ln_b),
      o2, p, vec(hg_norm), _head_block_diag(w), tok, gate, w_out.astype(BF16), mod_f,
      w_router.astype(BF16), b_router)


def _final_norm_kernel(x_ref, g_ref, o_ref):
    x = x_ref[0]
    o_ref[0] = x * lax.rsqrt(jnp.mean(x * x, axis=-1, keepdims=True) + NORM_EPS) * g_ref[...]


def _final_norm(x, g):
    bsz, t, d = x.shape
    bm = _row_block(t, 512)
    return pl.pallas_call(
        _final_norm_kernel, grid=(bsz, t // bm),
        in_specs=[pl.BlockSpec((1, bm, d), lambda b, i: (b, i, 0)), pl.BlockSpec((1, d), lambda b, i: (0, 0))],
        out_specs=pl.BlockSpec((1, bm, d), lambda b, i: (b, i, 0)),
        out_shape=jax.ShapeDtypeStruct((bsz, t, d), F32),
        compiler_params=pltpu.CompilerParams(dimension_semantics=("arbitrary", "arbitrary")),
        name="final_norm",
    )(x, g.reshape(1, d).astype(F32))


def _moe_kernel(meta_ref, x_ref, w1_ref, w3_ref, w2_ref, y_ref, *, n_blocks):
    active = pl.program_id(0) < meta_ref[n_blocks]

    @pl.when(active)
    def _():
        x = x_ref[...]
        a = _dot(x, w1_ref[0])
        b = _dot(x, w3_ref[0])
        y_ref[...] = _dot((a * _sigmoid(a) * b).astype(BF16), w2_ref[0])

    @pl.when(jnp.logical_not(active))
    def _():
        y_ref[...] = jnp.zeros_like(y_ref)


def _cast_kernel(x_ref, o_ref):
    o_ref[0] = x_ref[0, 0].astype(o_ref.dtype)


def _layer_to_bf16(w, layer):
    _, e, r, c = w.shape
    return pl.pallas_call(
        _cast_kernel, grid=(e, 2),
        in_specs=[pl.BlockSpec((1, 1, r // 2, c), lambda i, j: (layer, i, j, 0))],
        out_specs=pl.BlockSpec((1, r // 2, c), lambda i, j: (i, j, 0)),
        out_shape=jax.ShapeDtypeStruct((e, r, c), BF16),
        compiler_params=pltpu.CompilerParams(dimension_semantics=("arbitrary", "arbitrary"), vmem_limit_bytes=VMEM_LIMIT),
        name="cast_bf16",
    )(w)


def _moe_experts(xg, meta, w1, w3, w2):
    n_rows, d = xg.shape
    n_blocks = n_rows // MOE_BLOCK
    de = w1.shape[2]
    return pl.pallas_call(
        functools.partial(_moe_kernel, n_blocks=n_blocks),
        grid_spec=pltpu.PrefetchScalarGridSpec(
            num_scalar_prefetch=1, grid=(n_blocks,),
            in_specs=[pl.BlockSpec((MOE_BLOCK, d), lambda i, m: (i, 0)),
                      pl.BlockSpec((1, d, de), lambda i, m: (m[i], 0, 0)),
                      pl.BlockSpec((1, d, de), lambda i, m: (m[i], 0, 0)),
                      pl.BlockSpec((1, de, d), lambda i, m: (m[i], 0, 0))],
            out_specs=pl.BlockSpec((MOE_BLOCK, d), lambda i, m: (i, 0))),
        out_shape=jax.ShapeDtypeStruct((n_rows, d), F32),
        compiler_params=pltpu.CompilerParams(dimension_semantics=("arbitrary",), vmem_limit_bytes=VMEM_LIMIT),
        name="moe_experts",
    )(meta, xg, w1, w3, w2)


def _moe_ffn(h, logits, n_groups, n_experts, w1, w3, w2):
    n, d = h.shape
    epg = n_experts // n_groups
    group_logits = logits[:, :n_groups]
    g_sel = jnp.argmax(group_logits, axis=-1)
    g_gate = jnp.take_along_axis(jax.nn.softmax(group_logits, axis=-1), g_sel[:, None], axis=1)
    exp_logits = logits[:, n_groups:n_groups + n_experts].reshape(n, n_groups, epg)
    within = jnp.take_along_axis(exp_logits, g_sel[:, None, None], axis=1)[:, 0]
    top_val, top_idx = lax.top_k(within, TOP_K)
    gate = jax.nn.softmax(top_val, axis=-1) * g_gate
    e_flat = (g_sel[:, None] * epg + top_idx).reshape(-1).astype(jnp.int32)
    n_assign = n * TOP_K
    order = jnp.argsort(e_flat)
    rank = jnp.argsort(order).astype(jnp.int32)
    counts = jnp.sum(e_flat[:, None] == jnp.arange(n_experts, dtype=jnp.int32)[None, :], axis=0, dtype=jnp.int32)
    starts = jnp.cumsum(counts) - counts
    padded = (counts + MOE_BLOCK - 1) // MOE_BLOCK * MOE_BLOCK
    pad_ends = jnp.cumsum(padded)
    pad_starts = pad_ends - padded
    n_blocks = -(-n_assign // MOE_BLOCK) + n_experts
    n_rows = n_blocks * MOE_BLOCK
    block_exp = jnp.minimum(jnp.searchsorted(pad_ends, jnp.arange(n_blocks, dtype=jnp.int32) * MOE_BLOCK, side='right'), n_experts - 1).astype(jnp.int32)
    rows = jnp.arange(n_rows, dtype=jnp.int32)
    row_exp = jnp.repeat(block_exp, MOE_BLOCK)
    offset = rows - pad_starts[row_exp]
    src = jnp.clip(starts[row_exp] + offset, 0, n_assign - 1)
    row_tok = jnp.where(offset < counts[row_exp], order[src].astype(jnp.int32) // TOP_K, n)
    meta = jnp.concatenate([block_exp, (pad_ends[-1:] // MOE_BLOCK).astype(jnp.int32)])
    h_pad = jnp.concatenate([h, jnp.zeros((1, d), h.dtype)], axis=0)
    y_rows = _moe_experts(h_pad[row_tok], meta, w1, w3, w2)
    pos = (pad_starts[e_flat] + rank - starts[e_flat]).reshape(n, TOP_K)
    return gate[:, 0:1] * y_rows[pos[:, 0]] + gate[:, 1:2] * y_rows[pos[:, 1]]


def kernel(x, c, ctx, c_ctx, ada_w, ada_b, norm_mix, norm_ffn, w_in, w_out, s5_lam_re, s5_lam_im, s5_log_dt, s5_b_re, s5_b_im, s5_c_re, s5_c_im, s5_d, s5_w_glu, s5_b_glu, rw_mu, rw_w0, rw_w2, rw_a0, rw_a2, rw_g2, rw_k_k, rw_k_a, rw_r_k, rw_ln_w, rw_ln_b, hg_lb_logits, hg_norm, moe_w_group, moe_b_group, moe_w_expert, moe_b_expert, moe_w1, moe_w3, moe_w2, final_norm):
    bsz, n_lat, d = x.shape
    n_ctx = ctx.shape[1]
    depth = w_in.shape[0]
    s5_width = s5_d.shape[1]
    rw_width = rw_k_k.shape[1]
    rw_in = rw_mu.shape[2]
    hg_width = hg_norm.shape[1]
    n_groups = moe_w_group.shape[2]
    n_experts = moe_w_expert.shape[2]
    t = n_ctx + n_lat
    hg_col0 = -(-rw_in // hg_width)
    s5_off = (hg_col0 + 5) * hg_width
    n_proj = -(-(s5_off + s5_width) // IN_BN) * IN_BN
    pad1 = hg_col0 * hg_width - rw_in
    pad2 = n_proj - s5_off - s5_width

    silu_c = jax.nn.silu(c)
    silu_cc = jax.nn.silu(c_ctx)
    lb_all = jnp.cumsum(jax.nn.softmax(hg_lb_logits.astype(F32), axis=0), axis=0)
    lb_all = lb_all - lb_all[0:1]
    tok = jnp.concatenate([ctx, x], axis=1)
    for l in range(depth):
        last = l == depth - 1
        mod_lat = jnp.split(silu_c @ ada_w[l] + ada_b[l], 6, axis=-1)
        mod_ctx = [jnp.broadcast_to(z, (bsz, d)) for z in jnp.split(silu_cc @ ada_w[l] + ada_b[l], 6, axis=-1)]
        sh_m, sc_m, gt_m, sh_f, sc_f, gt_f = [jnp.stack([zc, zl], axis=1) for zc, zl in zip(mod_ctx, mod_lat)]
        mod_m = jnp.stack([norm_mix[l] * (1.0 + sc_m), sh_m], axis=2)
        mod_f = jnp.stack([norm_ffn[l] * (1.0 + sc_f), sh_f], axis=2)

        w_cols = [w_in[l][:, s5_width:s5_width + rw_in], jnp.zeros((d, pad1), F32),
                  w_in[l][:, s5_width + rw_in:], w_in[l][:, :s5_width], jnp.zeros((d, pad2), F32)]
        p = _in_proj(tok, mod_m, jnp.concatenate(w_cols, axis=1).astype(BF16), n_ctx)

        r, v, kk, g, lw, kd, bb = _rw_prep(p, n_ctx, rw_mu[l], rw_w0[l], rw_w2[l], rw_a0[l], rw_a2[l],
                                           rw_g2[l], rw_k_k[l], rw_k_a[l])
        y2 = _rw_scan(r, lw, kd, v, kk, bb, n_ctx)
        o2 = _hg_scan(p, lb_all[l], n_ctx, col0=hg_col0)
        s5y = _s5_scan(p[:, :, s5_off:s5_off + s5_width], n_ctx, s5_lam_re[l], s5_lam_im[l], s5_log_dt[l],
                       s5_b_re[l], s5_b_im[l], s5_c_re[l], s5_c_im[l], s5_d[l])

        w_router = jnp.concatenate([moe_w_group[l], moe_w_expert[l],
                                    jnp.zeros((d, ROUTER_PAD - n_groups - n_experts), F32)], axis=1)
        b_router = jnp.concatenate([moe_b_group[l], moe_b_expert[l],
                                    jnp.zeros((ROUTER_PAD - n_groups - n_experts,), F32)]).reshape(1, ROUTER_PAD)
        tok, h_ffn, logits = _out_proj(
            s5y, s5_w_glu[l], s5_b_glu[l], y2, r, kd, v, g, rw_r_k[l], rw_ln_w[l], rw_ln_b[l],
            o2, p, hg_col0 + 4, hg_norm[l], tok, gt_m, w_out[l], mod_f, w_router, b_router, n_ctx)

        experts = (_layer_to_bf16(moe_w1, l), _layer_to_bf16(moe_w3, l), _layer_to_bf16(moe_w2, l))
        if last:
            f = _moe_ffn(h_ffn[:, n_ctx:].reshape(bsz * n_lat, d), logits[:, n_ctx:].reshape(bsz * n_lat, ROUTER_PAD),
                         n_groups, n_experts, *experts).reshape(bsz, n_lat, d)
            x = tok[:, n_ctx:] + gt_f[:, 1:2] * f
        else:
            f = _moe_ffn(h_ffn.reshape(bsz * t, d), logits.reshape(bsz * t, ROUTER_PAD),
                         n_groups, n_experts, *experts).reshape(bsz, t, d)
            tok = jnp.concatenate([tok[:, :n_ctx] + gt_f[:, 0:1] * f[:, :n_ctx],
                                   tok[:, n_ctx:] + gt_f[:, 1:2] * f[:, n_ctx:]], axis=1)
    return _final_norm(x, final_norm)
```

```python
import functools
import math

import numpy as np
import jax
import jax.numpy as jnp
from jax import lax
from jax.experimental import pallas as pl
from jax.experimental.pallas import tpu as pltpu

F32 = jnp.float32
BF16 = jnp.bfloat16

GRID_W = 64
CH = GRID_W
HEAD = 64
S5_CH = 16
S5_STATE = 64
S5_CHUNK = 32
RW_LORA = 64
RW_GN_EPS = 64e-5
NORM_EPS = 1e-6
TOP_K = 2
MOE_BLOCK = 256
VMEM_LIMIT = 48 * 1024 * 1024


def _dot(a, b):
    return jnp.dot(a, b, preferred_element_type=F32)


def _dot_nt(a, b):
    return lax.dot_general(a, b, (((1,), (1,)), ((), ())), preferred_element_type=F32)


def _dot_tn(a, b):
    return lax.dot_general(a, b, (((0,), (0,)), ((), ())), preferred_element_type=F32)


def _split3(x):
    hi = x.astype(BF16)
    r1 = x - hi.astype(F32)
    mid = r1.astype(BF16)
    lo = (r1 - mid.astype(F32)).astype(BF16)
    return hi, mid, lo


def _dot_sel(m01, x):
    hi, mid, lo = _split3(x)
    return _dot(m01, hi) + _dot(m01, mid) + _dot(m01, lo)


def _dot_hp(a, b):
    a_hi = a.astype(BF16)
    a_lo = (a - a_hi.astype(F32)).astype(BF16)
    b_hi = b.astype(BF16)
    b_lo = (b - b_hi.astype(F32)).astype(BF16)
    return _dot(a_hi, b_hi) + _dot(a_hi, b_lo) + _dot(a_lo, b_hi)


def _sigmoid(x):
    return 1.0 / (1.0 + jnp.exp(-x))


def _scan_chunk(d, i, ncc, nc):
    back = jnp.where(i < ncc, ncc - 1 - i, nc + ncc - 1 - i)
    return jnp.where(d == 0, i, back)


def _half_block_masks(row, col, sgn):
    masks = []
    n = 2
    while n <= CH:
        same = (row // n) == (col // n)
        t_hi = ((row % n) >= n // 2).astype(jnp.int32)
        s_hi = ((col % n) >= n // 2).astype(jnp.int32)
        masks.append(jnp.logical_and(same, (t_hi - s_hi) * sgn == 1))
        n *= 2
    return masks


def _head_block_diag(width):
    idx = np.arange(width) // HEAD
    return jnp.asarray(idx[:, None] == idx[None, :], dtype=BF16)


def _rw_prep_kernel(cur_ref, prev_ref, next_ref, mu_ref, w0_ref, w2_ref, a0_ref, a2_ref, g2_ref,
                    kkw_ref, ka_ref, bd_ref,
                    r_out, v_out, kk_out, g_out, lw_out, kd_out, bb_out, *, ncc, nc, width):
    n = pl.program_id(1)
    cur = cur_ref[0]
    prv = prev_ref[0]
    nxt = next_ref[0]
    is_ctx = n < ncc
    row = lax.broadcasted_iota(jnp.int32, (CH, 1), 0)
    zero_row = jnp.zeros((1, cur.shape[1]), F32)
    first_fill = jnp.where(jnp.logical_and(is_ctx, n > 0), prv[CH - 1:CH, :], zero_row)
    last_fill = jnp.where(jnp.logical_and(is_ctx, n < ncc - 1), nxt[0:1, :], zero_row)
    nb_prev = jnp.where(row == 0, first_fill, pltpu.roll(cur, 1, 0))
    nb_next = jnp.where(row == CH - 1, last_fill, pltpu.roll(cur, CH - 1, 0))
    mu = mu_ref[...]
    zero_mu = jnp.zeros_like(mu[0:1])
    w_up = jnp.where(jnp.logical_and(jnp.logical_not(is_ctx), n > ncc), mu[2:3], zero_mu)
    w_down = jnp.where(jnp.logical_and(jnp.logical_not(is_ctx), n < nc - 1), mu[3:4], zero_mu)
    w_cur = 1.0 - mu[0:1] - mu[1:2] - jnp.where(is_ctx, zero_mu, mu[2:3] + mu[3:4])
    x = w_cur * cur + mu[0:1] * nb_prev + mu[1:2] * nb_next + w_up * prv + w_down * nxt

    w = width
    r = x[:, 0:w]
    k = x[:, w:2 * w]
    v = x[:, 2 * w:3 * w]
    wd = x[:, 3 * w:3 * w + 2 * RW_LORA]
    ad = x[:, 3 * w + 2 * RW_LORA:3 * w + 4 * RW_LORA]
    gd = x[:, 3 * w + 4 * RW_LORA:]

    r_out[0] = r.astype(r_out.dtype)
    v_out[0] = v.astype(v_out.dtype)
    g_out[0] = _dot_hp(_sigmoid(gd), g2_ref[...]).astype(g_out.dtype)
    kk = k * kkw_ref[...]
    kk2 = kk * kk
    kk2_hi = kk2.astype(BF16)
    kk2_lo = (kk2 - kk2_hi.astype(F32)).astype(BF16)
    ss = _dot(kk2_hi, bd_ref[...]) + _dot(kk2_lo, bd_ref[...])
    kk = kk / jnp.maximum(jnp.sqrt(ss), 1e-12)
    kk_out[0] = kk.astype(kk_out.dtype)
    ka = ka_ref[...]
    for d in range(2):
        w_raw = w0_ref[d:d + 1, :] + _dot_hp(jnp.tanh(wd[:, d * RW_LORA:(d + 1) * RW_LORA]), w2_ref[d])
        nw = -w_raw
        softplus = jnp.maximum(nw, 0.0) + jnp.log(1.0 + jnp.exp(-jnp.abs(nw)))
        lw_out[d, 0] = -jnp.exp(-softplus - 0.5)
        a = _sigmoid(a0_ref[d:d + 1, :] + _dot_hp(ad[:, d * RW_LORA:(d + 1) * RW_LORA], a2_ref[d]))
        kd_out[d, 0] = (k * (1.0 + (a - 1.0) * ka)).astype(kd_out.dtype)
        bb_out[d, 0] = (kk * a).astype(bb_out.dtype)


def _rw_prep(p_rw, n_ctx, mu, w0, w2, a0, a2, g2, k_k, k_a):
    bsz, t, _ = p_rw.shape
    win = mu.shape[1]
    width = k_k.shape[0]
    nc = t // CH
    ncc = n_ctx // CH
    tok = lambda b, n: (b, n, 0)
    full2 = lambda b, n: (0, 0)
    full3 = lambda b, n: (0, 0, 0)
    blk_in = (1, CH, win)
    out1 = jax.ShapeDtypeStruct((bsz, t, width), BF16)
    out2 = jax.ShapeDtypeStruct((2, bsz, t, width), BF16)
    out_lw = jax.ShapeDtypeStruct((2, bsz, t, width), F32)
    spec1 = pl.BlockSpec((1, CH, width), tok)
    spec2 = pl.BlockSpec((2, 1, CH, width), lambda b, n: (0, b, n, 0))
    return pl.pallas_call(
        functools.partial(_rw_prep_kernel, ncc=ncc, nc=nc, width=width),
        grid=(bsz, nc),
        in_specs=[
            pl.BlockSpec(blk_in, tok),
            pl.BlockSpec(blk_in, lambda b, n: (b, jnp.maximum(n - 1, 0), 0)),
            pl.BlockSpec(blk_in, lambda b, n: (b, jnp.minimum(n + 1, nc - 1), 0)),
            pl.BlockSpec(mu.shape, full2),
            pl.BlockSpec(w0.shape, full2),
            pl.BlockSpec(w2.shape, full3),
            pl.BlockSpec(a0.shape, full2),
            pl.BlockSpec(a2.shape, full3),
            pl.BlockSpec(g2.shape, full2),
            pl.BlockSpec((1, width), full2),
            pl.BlockSpec((1, width), full2),
            pl.BlockSpec((width, width), full2),
        ],
        out_specs=[spec1, spec1, spec1, spec1, spec2, spec2, spec2],
        out_shape=[out1, out1, out1, out1, out_lw, out2, out2],
        compiler_params=pltpu.CompilerParams(
            dimension_semantics=("arbitrary", "arbitrary"), vmem_limit_bytes=VMEM_LIMIT),
        name="rwkv_prep",
    )(p_rw, p_rw, p_rw, mu, w0, w2, a0, a2, g2, k_k.reshape(1, width), k_a.reshape(1, width),
      _head_block_diag(width))


def _rw_scan_kernel(r_ref, lw_ref, kd_ref, v_ref, kk_ref, bb_ref, y_ref, s_ref, *, heads, bsz):
    d = pl.program_id(0)
    i = pl.program_id(1)

    @pl.when(i == 0)
    def _():
        s_ref[...] = jnp.zeros_like(s_ref)

    row = lax.broadcasted_iota(jnp.int32, (CH, CH), 0)
    col = lax.broadcasted_iota(jnp.int32, (CH, CH), 1)
    sgn = jnp.where(d == 0, 1, -1)
    strict = (row - col) * sgn > 0
    diag = row == col
    incl = jnp.logical_or(strict, diag)
    tri = jnp.where(incl, 1.0, 0.0).astype(BF16)
    eye = jnp.where(diag, 1.0, 0.0)
    halves = _half_block_masks(row, col, sgn)

    hs = range(bsz * heads)
    sls = [slice(h * HEAD, (h + 1) * HEAD) for h in range(heads)] * bsz
    s0 = [s_ref[h] for h in hs]
    v, kkr, bk, b_w, k_w, wtot = [], [], [], [], [], []
    for b in range(bsz):
        lw = lw_ref[0, b]
        cw = _dot_sel(tri, lw)
        ctot = jnp.sum(lw, axis=0, keepdims=True)
        e_ncw = jnp.exp(-cw)
        e_rem = jnp.exp(ctot - cw)
        r_t = (r_ref[b].astype(F32) * jnp.exp(cw)).astype(BF16)
        kk_t = (kk_ref[b].astype(F32) * jnp.exp(cw - lw)).astype(BF16)
        bb = bb_ref[0, b].astype(F32)
        kd = kd_ref[0, b].astype(F32)
        b_t = (bb * e_ncw).astype(BF16)
        k_t = (kd * e_ncw).astype(BF16)
        bw_all = (bb * e_rem).astype(BF16)
        kw_all = (kd * e_rem).astype(BF16)
        wt_all = jnp.exp(ctot)
        v_all = v_ref[b]
        for sl in sls[:heads]:
            v.append(v_all[:, sl])
            kkr.append(jnp.concatenate([kk_t[:, sl], r_t[:, sl]], axis=0))
            bk.append(jnp.concatenate([b_t[:, sl], k_t[:, sl]], axis=0))
            b_w.append(bw_all[:, sl])
            k_w.append(kw_all[:, sl])
            wtot.append(wt_all[:, sl])
    att = [_dot_nt(kkr[h], bk[h]) for h in hs]
    proj = [_dot_nt(kkr[h], s0[h].astype(BF16)) for h in hs]
    a_ab = [jnp.where(strict, att[h][:CH, :CH], 0.0) for h in hs]
    a_ak = [jnp.where(strict, att[h][:CH, CH:], 0.0).astype(BF16) for h in hs]
    a_rb = [jnp.where(incl, att[h][CH:, :CH], 0.0).astype(BF16) for h in hs]
    a_rk = [jnp.where(incl, att[h][CH:, CH:], 0.0).astype(BF16) for h in hs]
    inv = [eye - jnp.where(halves[0], a_ab[h], 0.0) for h in hs]
    for lvl in range(1, len(halves)):
        inv_b = [inv[h].astype(BF16) for h in hs]
        a_l = [jnp.where(halves[lvl], a_ab[h], 0.0).astype(BF16) for h in hs]
        left = [_dot(inv_b[h], a_l[h]).astype(BF16) for h in hs]
        inv = [inv[h] - _dot(left[h], inv_b[h]) for h in hs]
    rhs = [(proj[h][:CH] + _dot(a_ak[h], v[h])).astype(BF16) for h in hs]
    u_b = [(-_dot(inv[h].astype(BF16), rhs[h])).astype(BF16) for h in hs]
    y = [proj[h][CH:] + _dot(a_rb[h], u_b[h]) + _dot(a_rk[h], v[h]) for h in hs]
    s_new = [s0[h] * wtot[h] + _dot_tn(u_b[h], b_w[h]) + _dot_tn(v[h], k_w[h]) for h in hs]
    for h in hs:
        y_ref[0, h // heads, :, sls[h]] = y[h].astype(y_ref.dtype)
        s_ref[h] = s_new[h]


def _rw_scan(r, lw, kd, v, kk, bb, n_ctx):
    bsz, t, width = r.shape
    heads = width // HEAD
    nc = t // CH
    ncc = n_ctx // CH
    s1 = pl.BlockSpec((bsz, CH, width), lambda d, i: (0, _scan_chunk(d, i, ncc, nc), 0))
    s2 = pl.BlockSpec((1, bsz, CH, width), lambda d, i: (d, 0, _scan_chunk(d, i, ncc, nc), 0))
    return pl.pallas_call(
        functools.partial(_rw_scan_kernel, heads=heads, bsz=bsz),
        grid=(2, nc),
        in_specs=[s1, s2, s2, s1, s1, s2],
        out_specs=s2,
        out_shape=jax.ShapeDtypeStruct((2, bsz, t, width), BF16),
        scratch_shapes=[pltpu.VMEM((bsz * heads, HEAD, HEAD), F32)],
        compiler_params=pltpu.CompilerParams(
            dimension_semantics=("arbitrary", "arbitrary"), vmem_limit_bytes=VMEM_LIMIT),
        name="rwkv_scan",
    )(r, lw, kd, v, kk, bb)


HG_LEVELS = 6


def _hg_decay_matrices():
    out = np.zeros((2, HG_LEVELS + 2, CH, CH), np.float32)
    pos = np.arange(CH)
    for d in range(2):
        for t in range(CH):
            if d == 0:
                out[d, 0, t, :t + 1] = 1.0
                out[d, 1, t, t + 1:] = 1.0
            else:
                out[d, 0, t, t:] = 1.0
                out[d, 1, t, :t] = 1.0
            for l in range(1, HG_LEVELS + 1):
                n = 1 << l
                mid = (t // n) * n + n // 2
                if d == 0:
                    sel = (pos >= mid) & (pos <= t) if t >= mid else (pos > t) & (pos < mid)
                else:
                    sel = (pos >= t) & (pos < mid) if t < mid else (pos >= mid) & (pos < t)
                out[d, 1 + l, t, sel] = 1.0
    return jnp.asarray(out.reshape(2, (HG_LEVELS + 2) * CH, CH), dtype=BF16)


def _hg_scan_kernel(q_ref, f_ref, i_ref, lb_ref, dm_ref, o_ref, s_ref, *, heads, bsz):
    d = pl.program_id(0)
    step = pl.program_id(1)

    @pl.when(step == 0)
    def _():
        s_ref[...] = jnp.zeros_like(s_ref)

    row = lax.broadcasted_iota(jnp.int32, (CH, CH), 0)
    col = lax.broadcasted_iota(jnp.int32, (CH, CH), 1)
    sgn = jnp.where(d == 0, 1, -1)

    lb = lb_ref[...]
    masks = _half_block_masks(row, col, sgn)
    lvl_masks = [row == col] + masks
    hs = range(bsz * heads)
    sls = [slice(h * HEAD, (h + 1) * HEAD) for h in range(heads)] * bsz
    q_lvl, k_lvl, q_in, k_out, etot, v_all = [], [], [], [], [], []
    for b in range(bsz):
        f = f_ref[b]
        q = q_ref[b]
        q = q * _sigmoid(q)
        log_sig = jnp.minimum(f, 0.0) - jnp.log(1.0 + jnp.exp(-jnp.abs(f)))
        hi_term = jnp.log(1.0 - lb) + log_sig
        lo_term = jnp.log(jnp.where(lb > 0.0, lb, 1.0))
        mx = jnp.maximum(hi_term, lo_term)
        lf_mix = mx + jnp.log(1.0 + jnp.exp(-jnp.abs(hi_term - lo_term)))
        lf = jnp.where(lb > 0.0, lf_mix, hi_term)
        key = (1.0 - lb) * _sigmoid(-f)
        ex = jnp.exp(_dot_sel(dm_ref[0], lf))
        q_in.append((q * ex[0:CH]).astype(BF16))
        k_out.append((key * ex[CH:2 * CH]).astype(BF16))
        etot.append(jnp.exp(jnp.sum(lf, axis=0, keepdims=True)))
        v_all.append(i_ref[b].astype(BF16))
        q_lvl.append([q.astype(BF16)] + [(q * ex[(1 + l) * CH:(2 + l) * CH]).astype(BF16)
                                         for l in range(1, HG_LEVELS + 1)])
        k_lvl.append([key.astype(BF16)] + [(key * ex[(1 + l) * CH:(2 + l) * CH]).astype(BF16)
                                           for l in range(1, HG_LEVELS + 1)])

    bi = [h // heads for h in hs]
    s0 = [s_ref[h] for h in hs]
    sc = [jnp.zeros((CH, CH), F32) for _ in hs]
    for l in range(HG_LEVELS + 1):
        sc = [sc[h] + jnp.where(lvl_masks[l], _dot_nt(q_lvl[bi[h]][l][:, sls[h]], k_lvl[bi[h]][l][:, sls[h]]), 0.0)
              for h in hs]
    o = [_dot(sc[h].astype(BF16), v_all[bi[h]][:, sls[h]]) + _dot_nt(q_in[bi[h]][:, sls[h]], s0[h].astype(BF16))
         for h in hs]
    s_new = [s0[h] * etot[bi[h]][:, sls[h]] + _dot_tn(v_all[bi[h]][:, sls[h]], k_out[bi[h]][:, sls[h]]) for h in hs]
    for h in hs:
        o_ref[0, bi[h], :, sls[h]] = o[h].astype(o_ref.dtype)
        s_ref[h] = s_new[h]


def _hg_scan(p_hg, lb, n_ctx, col0=0):
    bsz, t, _ = p_hg.shape
    width = lb.shape[0]
    heads = width // HEAD
    nc = t // CH
    ncc = n_ctx // CH
    dm = _hg_decay_matrices()
    chunk = lambda d, i: _scan_chunk(d, i, ncc, nc)
    blk = (bsz, CH, width)
    return pl.pallas_call(
        functools.partial(_hg_scan_kernel, heads=heads, bsz=bsz),
        grid=(2, nc),
        in_specs=[
            pl.BlockSpec(blk, lambda d, i: (0, chunk(d, i), col0)),
            pl.BlockSpec(blk, lambda d, i: (0, chunk(d, i), col0 + 1 + d)),
            pl.BlockSpec(blk, lambda d, i: (0, chunk(d, i), col0 + 3)),
            pl.BlockSpec((1, width), lambda d, i: (0, 0)),
            pl.BlockSpec((1,) + dm.shape[1:], lambda d, i: (d, 0, 0)),
        ],
        out_specs=pl.BlockSpec((1, bsz, CH, width), lambda d, i: (d, 0, chunk(d, i), 0)),
        out_shape=jax.ShapeDtypeStruct((2, bsz, t, width), BF16),
        scratch_shapes=[pltpu.VMEM((bsz * heads, HEAD, HEAD), F32)],
        compiler_params=pltpu.CompilerParams(
            dimension_semantics=("arbitrary", "arbitrary"), vmem_limit_bytes=VMEM_LIMIT),
        name="hgrn2_scan",
    )(p_hg, p_hg, p_hg, lb.reshape(1, width), dm)


def _s5_operators(lam_re, lam_im, log_dt, b_re, b_im, c_re, c_im):
    c = S5_CHUNK
    hp = lax.Precision.HIGHEST
    dt = jnp.exp(log_dt.astype(F32))[..., None]
    lam_re = lam_re.astype(F32)
    lam_im = lam_im.astype(F32)
    mag = jnp.exp(lam_re * dt)
    ang = lam_im * dt
    a_re, a_im = mag * jnp.cos(ang), mag * jnp.sin(ang)
    den = lam_re * lam_re + lam_im * lam_im
    f_re = ((a_re - 1.0) * lam_re + a_im * lam_im) / den
    f_im = (a_im * lam_re - (a_re - 1.0) * lam_im) / den
    bb_re = f_re[..., None] * b_re - f_im[..., None] * b_im
    bb_im = f_re[..., None] * b_im + f_im[..., None] * b_re
    j = jnp.arange(c + 1, dtype=F32)[None, None, :, None]
    pmag = jnp.exp((lam_re * dt)[:, :, None, :] * j)
    pang = ang[:, :, None, :] * j
    pw_re, pw_im = pmag * jnp.cos(pang), pmag * jnp.sin(pang)
    ca_re = c_re[:, :, None] * pw_re[:, :, :, None] - c_im[:, :, None] * pw_im[:, :, :, None]
    ca_im = c_re[:, :, None] * pw_im[:, :, :, None] + c_im[:, :, None] * pw_re[:, :, :, None]
    kern = (jnp.einsum('dgjcp,dgpk->dgjck', ca_re, bb_re, precision=hp)
            - jnp.einsum('dgjcp,dgpk->dgjck', ca_im, bb_im, precision=hp))
    tpos = np.arange(c)
    lag = tpos[None, :] - tpos[:, None]
    kf = jnp.where((lag >= 0)[None, :, :, None, None], kern[0][:, np.clip(lag, 0, c)], 0.0)
    kb = jnp.where((lag <= 0)[None, :, :, None, None], kern[1][:, np.clip(-lag, 0, c)], 0.0)
    g = kern.shape[1]
    tmat = jnp.transpose(kf + kb, (0, 1, 4, 2, 3)).reshape(g, c * S5_CH, c * S5_CH)

    def in_to_state(pr, pi, d):
        hr = pr[:, :, None, :] * jnp.swapaxes(bb_re[d], 1, 2)[:, None] - pi[:, :, None, :] * jnp.swapaxes(bb_im[d], 1, 2)[:, None]
        hi = pr[:, :, None, :] * jnp.swapaxes(bb_im[d], 1, 2)[:, None] + pi[:, :, None, :] * jnp.swapaxes(bb_re[d], 1, 2)[:, None]
        return hr.reshape(g, c * S5_CH, -1), hi.reshape(g, c * S5_CH, -1)

    hf_re, hf_im = in_to_state(pw_re[0, :, c - 1::-1][:, :c], pw_im[0, :, c - 1::-1][:, :c], 0)
    hb_re, hb_im = in_to_state(pw_re[1, :, :c], pw_im[1, :, :c], 1)
    hmat = jnp.concatenate([hf_re, hb_re, hf_im, hb_im], axis=-1)

    def state_to_out(cr, ci):
        return (jnp.transpose(cr, (0, 3, 1, 2)).reshape(g, -1, c * S5_CH),
                -jnp.transpose(ci, (0, 3, 1, 2)).reshape(g, -1, c * S5_CH))

    gf_re, gf_im = state_to_out(ca_re[0, :, 1:c + 1], ca_im[0, :, 1:c + 1])
    gb_re, gb_im = state_to_out(ca_re[1, :, c:0:-1], ca_im[1, :, c:0:-1])
    z = jnp.zeros_like(gf_re)
    gf = jnp.concatenate([gf_re, z, gf_im, z], axis=1)
    gb = jnp.concatenate([z, gb_re, z, gb_im], axis=1)
    ac_re = jnp.concatenate([pw_re[0, :, c], pw_re[1, :, c]], axis=-1)[:, None, :]
    ac_im = jnp.concatenate([pw_im[0, :, c], pw_im[1, :, c]], axis=-1)[:, None, :]
    return tmat.astype(BF16), hmat.astype(BF16), gf.astype(BF16), gb.astype(BF16), ac_re, ac_im


def _s5_local_kernel(u_ref, h_ref, loc_ref):
    loc_ref[0] = _dot(u_ref[0].astype(BF16), h_ref[0])


def _s5_state_kernel(loc_ref, ar_ref, ai_ref, sf_ref, sb_ref, *, nc, ncc, bsz):
    ar = ar_ref[0]
    ai = ai_ref[0]
    half = ar.shape[-1]
    is_fwd = lax.broadcasted_iota(jnp.int32, (1, half), 1) < half // 2

    def body(i, carry):
        nb = jnp.where(i < ncc, ncc - 1 - i, nc + ncc - 1 - i)
        new = []
        for b in range(bsz):
            re, im = carry[b]
            st = jnp.concatenate([re, im], axis=1)
            sf_ref[0, b, pl.ds(i, 1), :] = st
            sb_ref[0, b, pl.ds(nb, 1), :] = st
            lf = loc_ref[0, b, pl.ds(i, 1), :]
            lk = loc_ref[0, b, pl.ds(nb, 1), :]
            l_re = jnp.where(is_fwd, lf[:, :half], lk[:, :half])
            l_im = jnp.where(is_fwd, lf[:, half:], lk[:, half:])
            new.append((ar * re - ai * im + l_re, ar * im + ai * re + l_im))
        return tuple(new)

    zero = jnp.zeros((1, half), F32)
    lax.fori_loop(0, nc, body, tuple((zero, zero) for _ in range(bsz)))


def _s5_out_kernel(u_ref, t_ref, sf_ref, gf_ref, sb_ref, gb_ref, d_ref, y_ref):
    u = u_ref[0]
    y = (_dot(u.astype(BF16), t_ref[0]) + _dot(sf_ref[0].astype(BF16), gf_ref[0])
         + _dot(sb_ref[0].astype(BF16), gb_ref[0]) + d_ref[0] * u.astype(F32))
    gelu = 0.5 * y * (1.0 + jnp.tanh(math.sqrt(2.0 / math.pi) * (y + 0.044715 * (y * y * y))))
    y_ref[0] = gelu.astype(y_ref.dtype)


def _s5_scan(p_s5, n_ctx, lam_re, lam_im, log_dt, b_re, b_im, c_re, c_im, d_skip):
    bsz, t, width = p_s5.shape
    g = width // S5_CH
    c = S5_CHUNK
    nc, ncc = t // c, n_ctx // c
    blk = c * S5_CH
    rows = bsz * nc
    tmat, hmat, gf, gb, ac_re, ac_im = _s5_operators(lam_re, lam_im, log_dt, b_re, b_im, c_re, c_im)
    u = jnp.transpose(p_s5.astype(F32).reshape(bsz, nc, c, g, S5_CH), (3, 0, 1, 2, 4)).reshape(g, rows, blk)
    per_g = lambda gi: (gi, 0, 0)
    cparams = pltpu.CompilerParams(dimension_semantics=("arbitrary",), vmem_limit_bytes=VMEM_LIMIT)
    loc = pl.pallas_call(
        _s5_local_kernel, grid=(g,),
        in_specs=[pl.BlockSpec((1, rows, blk), per_g), pl.BlockSpec((1, blk, 256), per_g)],
        out_specs=pl.BlockSpec((1, rows, 256), per_g),
        out_shape=jax.ShapeDtypeStruct((g, rows, 256), F32),
        compiler_params=cparams, name="s5_local",
    )(u, hmat)
    per_g4 = lambda gi: (gi, 0, 0, 0)
    st_shape = jax.ShapeDtypeStruct((g, bsz, nc, 256), F32)
    sf, sb = pl.pallas_call(
        functools.partial(_s5_state_kernel, nc=nc, ncc=ncc, bsz=bsz), grid=(g,),
        in_specs=[pl.BlockSpec((1, bsz, nc, 256), per_g4), pl.BlockSpec((1, 1, 128), per_g),
                  pl.BlockSpec((1, 1, 128), per_g)],
        out_specs=[pl.BlockSpec((1, bsz, nc, 256), per_g4)] * 2,
        out_shape=[st_shape, st_shape],
        compiler_params=cparams, name="s5_state",
    )(loc.reshape(g, bsz, nc, 256), ac_re, ac_im)
    d_flat = jnp.tile(d_skip.astype(F32).reshape(g, 1, S5_CH), (1, c, 1)).reshape(g, 1, blk)
    y = pl.pallas_call(
        _s5_out_kernel, grid=(g,),
        in_specs=[pl.BlockSpec((1, rows, blk), per_g), pl.BlockSpec((1, blk, blk), per_g),
                  pl.BlockSpec((1, rows, 256), per_g), pl.BlockSpec((1, 256, blk), per_g),
                  pl.BlockSpec((1, rows, 256), per_g), pl.BlockSpec((1, 256, blk), per_g),
                  pl.BlockSpec((1, 1, blk), per_g)],
        out_specs=pl.BlockSpec((1, rows, blk), per_g),
        out_shape=jax.ShapeDtypeStruct((g, rows, blk), F32),
        compiler_params=cparams, name="s5_out",
    )(u, tmat, sf.reshape(g, rows, 256), gf, sb.reshape(g, rows, 256), gb, d_flat)
    return jnp.transpose(y.reshape(g, bsz, nc, c, S5_CH), (1, 2, 3, 0, 4)).reshape(bsz, t, width)


IN_BM = 1280
IN_BN = 768
OUT_BM = 256
ROUTER_PAD = 128


def _row_block(t, target):
    bm = target
    while t % bm:
        bm -= CH
    return bm


def _norm_mod(x, mod_ref, first_row, n_ctx):
    xn = x * lax.rsqrt(jnp.mean(x * x, axis=-1, keepdims=True) + NORM_EPS)
    is_ctx = (first_row + lax.broadcasted_iota(jnp.int32, (x.shape[0], 1), 0)) < n_ctx
    scale = jnp.where(is_ctx, mod_ref[0, 0, 0:1, :], mod_ref[0, 1, 0:1, :])
    shift = jnp.where(is_ctx, mod_ref[0, 0, 1:2, :], mod_ref[0, 1, 1:2, :])
    return xn * scale + shift


def _in_proj_kernel(x_ref, mod_ref, w_ref, p_ref, h_ref, *, n_ctx, bm):
    @pl.when(pl.program_id(2) == 0)
    def _():
        sub = _row_block(bm, 256)

        def body(k, carry):
            rows = pl.ds(pl.multiple_of(k * sub, sub), sub)
            h = _norm_mod(x_ref[0, rows, :], mod_ref, pl.program_id(1) * bm + k * sub, n_ctx)
            h_ref[rows, :] = h.astype(BF16)
            return carry

        lax.fori_loop(0, bm // sub, body, 0)

    p_ref[0] = _dot(h_ref[...], w_ref[...])


def _in_proj(tok, mod, w, n_ctx):
    bsz, t, d = tok.shape
    n = w.shape[1]
    bm = _row_block(t, IN_BM)
    return pl.pallas_call(
        functools.partial(_in_proj_kernel, n_ctx=n_ctx, bm=bm),
        grid=(bsz, t // bm, n // IN_BN),
        in_specs=[pl.BlockSpec((1, bm, d), lambda b, i, j: (b, i, 0), pipeline_mode=pl.Buffered(1)),
                  pl.BlockSpec((1, 2, 2, d), lambda b, i, j: (b, 0, 0, 0)),
                  pl.BlockSpec((d, IN_BN), lambda b, i, j: (0, j))],
        out_specs=pl.BlockSpec((1, bm, IN_BN), lambda b, i, j: (b, i, j)),
        out_shape=jax.ShapeDtypeStruct((bsz, t, n), F32),
        scratch_shapes=[pltpu.VMEM((bm, d), BF16)],
        compiler_params=pltpu.CompilerParams(
            dimension_semantics=("arbitrary", "arbitrary", "arbitrary"), vmem_limit_bytes=VMEM_LIMIT),
        name="in_proj",
    )(tok, mod, w)


def _out_proj_kernel(s5_ref, wglu_ref, bglu_ref,
                     y2_ref, r_ref, kd_ref, v_ref, g_ref, rk_ref, lnw_ref, lnb_ref,
                     o2_ref, og_ref, hgn_ref, bd_ref,
                     x_ref, gate_ref, wout_ref, modf_ref, wr_ref, br_ref,
                     xo_ref, h_ref, lg_ref, *, n_ctx, bm):
    first = pl.program_id(1) * bm
    bd = bd_ref[...]
    inv_n = 1.0 / HEAD

    def head_sum(z):
        return _dot(z.astype(BF16), bd)

    ys = s5_ref[0].astype(F32)
    ys = ys * _sigmoid(_dot(ys.astype(BF16), wglu_ref[...]) + bglu_ref[...])
    y = y2_ref[0, 0].astype(F32) + y2_ref[1, 0].astype(F32)
    yc = y - head_sum(y) * inv_n
    var = head_sum(yc * yc) * inv_n
    k_bonus = 0.5 * (kd_ref[0, 0].astype(F32) + kd_ref[1, 0].astype(F32))
    bonus = head_sum(r_ref[0].astype(F32) * k_bonus * rk_ref[...]) * v_ref[0].astype(F32)
    yr = (yc * lax.rsqrt(var + RW_GN_EPS) * lnw_ref[...] + lnb_ref[...] + bonus) * g_ref[0].astype(F32)
    o = o2_ref[0, 0].astype(F32) + o2_ref[1, 0].astype(F32)
    yh = o * lax.rsqrt(head_sum(o * o) * inv_n + NORM_EPS) * hgn_ref[...] * _sigmoid(og_ref[0])

    m = _dot(jnp.concatenate([ys, yr, yh], axis=-1).astype(BF16), wout_ref[...])
    is_ctx = (first + lax.broadcasted_iota(jnp.int32, (bm, 1), 0)) < n_ctx
    x_new = x_ref[0] + jnp.where(is_ctx, gate_ref[0, 0:1, :], gate_ref[0, 1:2, :]) * m
    xo_ref[0] = x_new
    h = _norm_mod(x_new, modf_ref, first, n_ctx).astype(BF16)
    h_ref[0] = h
    lg_ref[0] = _dot(h, wr_ref[...]) + br_ref[...]


def _out_proj(s5y, w_glu, b_glu, y2, r, kd, v, g, r_k, ln_w, ln_b, o2, p, og_col, hg_norm,
              tok, gate, w_out, mod_f, w_router, b_router, n_ctx):
    bsz, t, d = tok.shape
    bm = _row_block(t, OUT_BM)
    s5w = s5y.shape[-1]
    w = r.shape[-1]
    row = lambda b, i: (b, i, 0)
    row2 = lambda b, i: (0, b, i, 0)
    c2 = lambda b, i: (0, 0)
    vec = lambda z: z.reshape(1, -1).astype(F32)
    spec_w = pl.BlockSpec((1, bm, w), row)
    spec_2w = pl.BlockSpec((2, 1, bm, w), row2)
    spec_vw = pl.BlockSpec((1, w), c2)
    return pl.pallas_call(
        functools.partial(_out_proj_kernel, n_ctx=n_ctx, bm=bm),
        grid=(bsz, t // bm),
        in_specs=[
            pl.BlockSpec((1, bm, s5w), row), pl.BlockSpec((s5w, s5w), c2), pl.BlockSpec((1, s5w), c2),
            spec_2w, spec_w, spec_2w, spec_w, spec_w, spec_vw, spec_vw, spec_vw,
            spec_2w, pl.BlockSpec((1, bm, w), lambda b, i: (b, i, og_col)), spec_vw, pl.BlockSpec((w, w), c2),
            pl.BlockSpec((1, bm, d), row), pl.BlockSpec((1, 2, d), lambda b, i: (b, 0, 0)),
            pl.BlockSpec(w_out.shape, c2), pl.BlockSpec((1, 2, 2, d), lambda b, i: (b, 0, 0, 0)),
            pl.BlockSpec((d, ROUTER_PAD), c2), pl.BlockSpec((1, ROUTER_PAD), c2),
        ],
        out_specs=[pl.BlockSpec((1, bm, d), row), pl.BlockSpec((1, bm, d), row),
                   pl.BlockSpec((1, bm, ROUTER_PAD), row)],
        out_shape=[jax.ShapeDtypeStruct((bsz, t, d), F32), jax.ShapeDtypeStruct((bsz, t, d), BF16),
                   jax.ShapeDtypeStruct((bsz, t, ROUTER_PAD), F32)],
        compiler_params=pltpu.CompilerParams(
            dimension_semantics=("arbitrary", "arbitrary"), vmem_limit_bytes=VMEM_LIMIT),
        name="out_proj",
    )(s5y, w_glu.astype(BF16), vec(b_glu), y2, r, kd, v, g, vec(r_k), vec(ln_w), vec(ln_b),
      o2, p, vec(hg_norm), _head_block_diag(w), tok, gate, w_out.astype(BF16), mod_f,
      w_router.astype(BF16), b_router)


def _final_norm_kernel(x_ref, g_ref, o_ref):
    x = x_ref[0]
    o_ref[0] = x * lax.rsqrt(jnp.mean(x * x, axis=-1, keepdims=True) + NORM_EPS) * g_ref[...]


def _final_norm(x, g):
    bsz, t, d = x.shape
    bm = _row_block(t, 512)
    return pl.pallas_call(
        _final_norm_kernel, grid=(bsz, t // bm),
        in_specs=[pl.BlockSpec((1, bm, d), lambda b, i: (b, i, 0)), pl.BlockSpec((1, d), lambda b, i: (0, 0))],
        out_specs=pl.BlockSpec((1, bm, d), lambda b, i: (b, i, 0)),
        out_shape=jax.ShapeDtypeStruct((bsz, t, d), F32),
        compiler_params=pltpu.CompilerParams(dimension_semantics=("arbitrary", "arbitrary")),
        name="final_norm",
    )(x, g.reshape(1, d).astype(F32))


def _moe_kernel(meta_ref, x_ref, w1_ref, w3_ref, w2_ref, y_ref, *, n_blocks):
    active = pl.program_id(0) < meta_ref[n_blocks]

    @pl.when(active)
    def _():
        x = x_ref[...]
        a = _dot(x, w1_ref[0])
        b = _dot(x, w3_ref[0])
        y_ref[...] = _dot((a * _sigmoid(a) * b).astype(BF16), w2_ref[0])

    @pl.when(jnp.logical_not(active))
    def _():
        y_ref[...] = jnp.zeros_like(y_ref)


def _cast_kernel(x_ref, o_ref):
    o_ref[0] = x_ref[0, 0].astype(o_ref.dtype)


def _layer_to_bf16(w, layer):
    _, e, r, c = w.shape
    return pl.pallas_call(
        _cast_kernel, grid=(e, 2),
        in_specs=[pl.BlockSpec((1, 1, r // 2, c), lambda i, j: (layer, i, j, 0))],
        out_specs=pl.BlockSpec((1, r // 2, c), lambda i, j: (i, j, 0)),
        out_shape=jax.ShapeDtypeStruct((e, r, c), BF16),
        compiler_params=pltpu.CompilerParams(dimension_semantics=("arbitrary", "arbitrary"), vmem_limit_bytes=VMEM_LIMIT),
        name="cast_bf16",
    )(w)


def _moe_experts(xg, meta, w1, w3, w2):
    n_rows, d = xg.shape
    n_blocks = n_rows // MOE_BLOCK
    de = w1.shape[2]
    return pl.pallas_call(
        functools.partial(_moe_kernel, n_blocks=n_blocks),
        grid_spec=pltpu.PrefetchScalarGridSpec(
            num_scalar_prefetch=1, grid=(n_blocks,),
            in_specs=[pl.BlockSpec((MOE_BLOCK, d), lambda i, m: (i, 0)),
                      pl.BlockSpec((1, d, de), lambda i, m: (m[i], 0, 0)),
                      pl.BlockSpec((1, d, de), lambda i, m: (m[i], 0, 0)),
                      pl.BlockSpec((1, de, d), lambda i, m: (m[i], 0, 0))],
            out_specs=pl.BlockSpec((MOE_BLOCK, d), lambda i, m: (i, 0))),
        out_shape=jax.ShapeDtypeStruct((n_rows, d), F32),
        compiler_params=pltpu.CompilerParams(dimension_semantics=("arbitrary",), vmem_limit_bytes=VMEM_LIMIT),
        name="moe_experts",
    )(meta, xg, w1, w3, w2)


def _moe_ffn(h, logits, n_groups, n_experts, w1, w3, w2):
    n, d = h.shape
    epg = n_experts // n_groups
    group_logits = logits[:, :n_groups]
    g_sel = jnp.argmax(group_logits, axis=-1)
    g_gate = jnp.take_along_axis(jax.nn.softmax(group_logits, axis=-1), g_sel[:, None], axis=1)
    exp_logits = logits[:, n_groups:n_groups + n_experts].reshape(n, n_groups, epg)
    within = jnp.take_along_axis(exp_logits, g_sel[:, None, None], axis=1)[:, 0]
    top_val, top_idx = lax.top_k(within, TOP_K)
    gate = jax.nn.softmax(top_val, axis=-1) * g_gate
    e_flat = (g_sel[:, None] * epg + top_idx).reshape(-1).astype(jnp.int32)
    n_assign = n * TOP_K
    order = jnp.argsort(e_flat)
    rank = jnp.argsort(order).astype(jnp.int32)
    counts = jnp.sum(e_flat[:, None] == jnp.arange(n_experts, dtype=jnp.int32)[None, :], axis=0, dtype=jnp.int32)
    starts = jnp.cumsum(counts) - counts
    padded = (counts + MOE_BLOCK - 1) // MOE_BLOCK * MOE_BLOCK
    pad_ends = jnp.cumsum(padded)
    pad_starts = pad_ends - padded
    n_blocks = -(-n_assign // MOE_BLOCK) + n_experts
    n_rows = n_blocks * MOE_BLOCK
    block_exp = jnp.minimum(jnp.searchsorted(pad_ends, jnp.arange(n_blocks, dtype=jnp.int32) * MOE_BLOCK, side='right'), n_experts - 1).astype(jnp.int32)
    rows = jnp.arange(n_rows, dtype=jnp.int32)
    row_exp = jnp.repeat(block_exp, MOE_BLOCK)
    offset = rows - pad_starts[row_exp]
    src = jnp.clip(starts[row_exp] + offset, 0, n_assign - 1)
    row_tok = jnp.where(offset < counts[row_exp], order[src].astype(jnp.int32) // TOP_K, n)
    meta = jnp.concatenate([block_exp, (pad_ends[-1:] // MOE_BLOCK).astype(jnp.int32)])
    h_pad = jnp.concatenate([h, jnp.zeros((1, d), h.dtype)], axis=0)
    y_rows = _moe_experts(h_pad[row_tok], meta, w1, w3, w2)
    pos = (pad_starts[e_flat] + rank - starts[e_flat]).reshape(n, TOP_K)
    return gate[:, 0:1] * y_rows[pos[:, 0]] + gate[:, 1:2] * y_rows[pos[:, 1]]


def kernel(x, c, ctx, c_ctx, ada_w, ada_b, norm_mix, norm_ffn, w_in, w_out, s5_lam_re, s5_lam_im, s5_log_dt, s5_b_re, s5_b_im, s5_c_re, s5_c_im, s5_d, s5_w_glu, s5_b_glu, rw_mu, rw_w0, rw_w2, rw_a0, rw_a2, rw_g2, rw_k_k, rw_k_a, rw_r_k, rw_ln_w, rw_ln_b, hg_lb_logits, hg_norm, moe_w_group, moe_b_group, moe_w_expert, moe_b_expert, moe_w1, moe_w3, moe_w2, final_norm):
    bsz, n_lat, d = x.shape
    n_ctx = ctx.shape[1]
    depth = w_in.shape[0]
    s5_width = s5_d.shape[1]
    rw_width = rw_k_k.shape[1]
    rw_in = rw_mu.shape[2]
    hg_width = hg_norm.shape[1]
    n_groups = moe_w_group.shape[2]
    n_experts = moe_w_expert.shape[2]
    t = n_ctx + n_lat
    hg_col0 = -(-rw_in // hg_width)
    s5_off = (hg_col0 + 5) * hg_width
    n_proj = -(-(s5_off + s5_width) // IN_BN) * IN_BN
    pad1 = hg_col0 * hg_width - rw_in
    pad2 = n_proj - s5_off - s5_width

    silu_c = jax.nn.silu(c)
    silu_cc = jax.nn.silu(c_ctx)
    lb_all = jnp.cumsum(jax.nn.softmax(hg_lb_logits.astype(F32), axis=0), axis=0)
    lb_all = lb_all - lb_all[0:1]
    tok = jnp.concatenate([ctx, x], axis=1)
    for l in range(depth):
        last = l == depth - 1
        mod_lat = jnp.split(silu_c @ ada_w[l] + ada_b[l], 6, axis=-1)
        mod_ctx = [jnp.broadcast_to(z, (bsz, d)) for z in jnp.split(silu_cc @ ada_w[l] + ada_b[l], 6, axis=-1)]
        sh_m, sc_m, gt_m, sh_f, sc_f, gt_f = [jnp.stack([zc, zl], axis=1) for zc, zl in zip(mod_ctx, mod_lat)]
        mod_m = jnp.stack([norm_mix[l] * (1.0 + sc_m), sh_m], axis=2)
        mod_f = jnp.stack([norm_ffn[l] * (1.0 + sc_f), sh_f], axis=2)

        w_cols = [w_in[l][:, s5_width:s5_width + rw_in], jnp.zeros((d, pad1), F32),
                  w_in[l][:, s5_width + rw_in:], w_in[l][:, :s5_width], jnp.zeros((d, pad2), F32)]
        p = _in_proj(tok, mod_m, jnp.concatenate(w_cols, axis=1).astype(BF16), n_ctx)

        r, v, kk, g, lw, kd, bb = _rw_prep(p, n_ctx, rw_mu[l], rw_w0[l], rw_w2[l], rw_a0[l], rw_a2[l],
                                           rw_g2[l], rw_k_k[l], rw_k_a[l])
        y2 = _rw_scan(r, lw, kd, v, kk, bb, n_ctx)
        o2 = _hg_scan(p, lb_all[l], n_ctx, col0=hg_col0)
        s5y = _s5_scan(p[:, :, s5_off:s5_off + s5_width], n_ctx, s5_lam_re[l], s5_lam_im[l], s5_log_dt[l],
                       s5_b_re[l], s5_b_im[l], s5_c_re[l], s5_c_im[l], s5_d[l])

        w_router = jnp.concatenate([moe_w_group[l], moe_w_expert[l],
                                    jnp.zeros((d, ROUTER_PAD - n_groups - n_experts), F32)], axis=1)
        b_router = jnp.concatenate([moe_b_group[l], moe_b_expert[l],
                                    jnp.zeros((ROUTER_PAD - n_groups - n_experts,), F32)]).reshape(1, ROUTER_PAD)
        tok, h_ffn, logits = _out_proj(
            s5y, s5_w_glu[l], s5_b_glu[l], y2, r, kd, v, g, rw_r_k[l], rw_ln_w[l], rw_ln_b[l],
            o2, p, hg_col0 + 4, hg_norm[l], tok, gt_m, w_out[l], mod_f, w_router, b_router, n_ctx)

        experts = (_layer_to_bf16(moe_w1, l), _layer_to_bf16(moe_w3, l), _layer_to_bf16(moe_w2, l))
        if last:
            f = _moe_ffn(h_ffn[:, n_ctx:].reshape(bsz * n_lat, d), logits[:, n_ctx:].reshape(bsz * n_lat, ROUTER_PAD),
                         n_groups, n_experts, *experts).reshape(bsz, n_lat, d)
            x = tok[:, n_ctx:] + gt_f[:, 1:2] * f
        else:
            f = _moe_ffn(h_ffn.reshape(bsz * t, d), logits.reshape(bsz * t, ROUTER_PAD),
                         n_groups, n_experts, *experts).reshape(bsz, t, d)
            tok = jnp.concatenate([tok[:, :n_ctx] + gt_f[:, 0:1] * f[:, :n_ctx],
                                   tok[:, n_ctx:] + gt_f[:, 1:2] * f[:, n_ctx:]], axis=1)
    return _final_norm(x, final_norm)
```

```python
import functools
import math

import numpy as np
import jax
import jax.numpy as jnp
from jax import lax
from jax.experimental import pallas as pl
from jax.experimental.pallas import tpu as pltpu

F32 = jnp.float32
BF16 = jnp.bfloat16

GRID_W = 64
CH = GRID_W
HEAD = 64
S5_CH = 16
S5_STATE = 64
S5_CHUNK = 32
RW_LORA = 64
RW_GN_EPS = 64e-5
NORM_EPS = 1e-6
TOP_K = 2
MOE_BLOCK = 256
VMEM_LIMIT = 48 * 1024 * 1024


def _dot(a, b):
    return jnp.dot(a, b, preferred_element_type=F32)


def _dot_nt(a, b):
    return lax.dot_general(a, b, (((1,), (1,)), ((), ())), preferred_element_type=F32)


def _dot_tn(a, b):
    return lax.dot_general(a, b, (((0,), (0,)), ((), ())), preferred_element_type=F32)


def _split3(x):
    hi = x.astype(BF16)
    r1 = x - hi.astype(F32)
    mid = r1.astype(BF16)
    lo = (r1 - mid.astype(F32)).astype(BF16)
    return hi, mid, lo


def _dot_sel(m01, x):
    hi, mid, lo = _split3(x)
    return _dot(m01, hi) + _dot(m01, mid) + _dot(m01, lo)


def _dot_hp(a, b):
    a_hi = a.astype(BF16)
    a_lo = (a - a_hi.astype(F32)).astype(BF16)
    b_hi = b.astype(BF16)
    b_lo = (b - b_hi.astype(F32)).astype(BF16)
    return _dot(a_hi, b_hi) + _dot(a_hi, b_lo) + _dot(a_lo, b_hi)


def _sigmoid(x):
    return 1.0 / (1.0 + jnp.exp(-x))


def _scan_chunk(d, i, ncc, nc):
    back = jnp.where(i < ncc, ncc - 1 - i, nc + ncc - 1 - i)
    return jnp.where(d == 0, i, back)


def _half_block_masks(row, col, sgn):
    masks = []
    n = 2
    while n <= CH:
        same = (row // n) == (col // n)
        t_hi = ((row % n) >= n // 2).astype(jnp.int32)
        s_hi = ((col % n) >= n // 2).astype(jnp.int32)
        masks.append(jnp.logical_and(same, (t_hi - s_hi) * sgn == 1))
        n *= 2
    return masks


def _head_block_diag(width):
    idx = np.arange(width) // HEAD
    return jnp.asarray(idx[:, None] == idx[None, :], dtype=BF16)


def _rw_prep_kernel(cur_ref, prev_ref, next_ref, mu_ref, w0_ref, w2_ref, a0_ref, a2_ref, g2_ref,
                    kkw_ref, ka_ref, bd_ref,
                    r_out, v_out, kk_out, g_out, lw_out, kd_out, bb_out, *, ncc, nc, width):
    n = pl.program_id(1)
    cur = cur_ref[0].astype(F32)
    prv = prev_ref[0].astype(F32)
    nxt = next_ref[0].astype(F32)
    is_ctx = n < ncc
    row = lax.broadcasted_iota(jnp.int32, (CH, 1), 0)
    zero_row = jnp.zeros((1, cur.shape[1]), F32)
    first_fill = jnp.where(jnp.logical_and(is_ctx, n > 0), prv[CH - 1:CH, :], zero_row)
    last_fill = jnp.where(jnp.logical_and(is_ctx, n < ncc - 1), nxt[0:1, :], zero_row)
    nb_prev = jnp.where(row == 0, first_fill, pltpu.roll(cur, 1, 0))
    nb_next = jnp.where(row == CH - 1, last_fill, pltpu.roll(cur, CH - 1, 0))
    mu = mu_ref[...]
    zero_mu = jnp.zeros_like(mu[0:1])
    w_up = jnp.where(jnp.logical_and(jnp.logical_not(is_ctx), n > ncc), mu[2:3], zero_mu)
    w_down = jnp.where(jnp.logical_and(jnp.logical_not(is_ctx), n < nc - 1), mu[3:4], zero_mu)
    w_cur = 1.0 - mu[0:1] - mu[1:2] - jnp.where(is_ctx, zero_mu, mu[2:3] + mu[3:4])
    x = w_cur * cur + mu[0:1] * nb_prev + mu[1:2] * nb_next + w_up * prv + w_down * nxt

    w = width
    r = x[:, 0:w]
    k = x[:, w:2 * w]
    v = x[:, 2 * w:3 * w]
    wd = x[:, 3 * w:3 * w + 2 * RW_LORA]
    ad = x[:, 3 * w + 2 * RW_LORA:3 * w + 4 * RW_LORA]
    gd = x[:, 3 * w + 4 * RW_LORA:]

    r_out[0] = r.astype(r_out.dtype)
    v_out[0] = v.astype(v_out.dtype)
    g_out[0] = _dot_hp(_sigmoid(gd), g2_ref[...]).astype(g_out.dtype)
    kk = k * kkw_ref[...]
    kk2 = kk * kk
    kk2_hi = kk2.astype(BF16)
    kk2_lo = (kk2 - kk2_hi.astype(F32)).astype(BF16)
    ss = _dot(kk2_hi, bd_ref[...]) + _dot(kk2_lo, bd_ref[...])
    kk = kk / jnp.maximum(jnp.sqrt(ss), 1e-12)
    kk_out[0] = kk.astype(kk_out.dtype)
    ka = ka_ref[...]
    for d in range(2):
        w_raw = w0_ref[d:d + 1, :] + _dot_hp(jnp.tanh(wd[:, d * RW_LORA:(d + 1) * RW_LORA]), w2_ref[d])
        nw = -w_raw
        softplus = jnp.maximum(nw, 0.0) + jnp.log(1.0 + jnp.exp(-jnp.abs(nw)))
        lw_out[d, 0] = -jnp.exp(-softplus - 0.5)
        a = _sigmoid(a0_ref[d:d + 1, :] + _dot_hp(ad[:, d * RW_LORA:(d + 1) * RW_LORA], a2_ref[d]))
        kd_out[d, 0] = (k * (1.0 + (a - 1.0) * ka)).astype(kd_out.dtype)
        bb_out[d, 0] = (kk * a).astype(bb_out.dtype)


def _rw_prep(p_rw, n_ctx, mu, w0, w2, a0, a2, g2, k_k, k_a):
    bsz, t, _ = p_rw.shape
    win = mu.shape[1]
    width = k_k.shape[0]
    nc = t // CH
    ncc = n_ctx // CH
    tok = lambda b, n: (b, n, 0)
    full2 = lambda b, n: (0, 0)
    full3 = lambda b, n: (0, 0, 0)
    blk_in = (1, CH, win)
    out1 = jax.ShapeDtypeStruct((bsz, t, width), BF16)
    out2 = jax.ShapeDtypeStruct((2, bsz, t, width), BF16)
    out_lw = jax.ShapeDtypeStruct((2, bsz, t, width), F32)
    spec1 = pl.BlockSpec((1, CH, width), tok)
    spec2 = pl.BlockSpec((2, 1, CH, width), lambda b, n: (0, b, n, 0))
    return pl.pallas_call(
        functools.partial(_rw_prep_kernel, ncc=ncc, nc=nc, width=width),
        grid=(bsz, nc),
        in_specs=[
            pl.BlockSpec(blk_in, tok),
            pl.BlockSpec(blk_in, lambda b, n: (b, jnp.maximum(n - 1, 0), 0)),
            pl.BlockSpec(blk_in, lambda b, n: (b, jnp.minimum(n + 1, nc - 1), 0)),
            pl.BlockSpec(mu.shape, full2),
            pl.BlockSpec(w0.shape, full2),
            pl.BlockSpec(w2.shape, full3),
            pl.BlockSpec(a0.shape, full2),
            pl.BlockSpec(a2.shape, full3),
            pl.BlockSpec(g2.shape, full2),
            pl.BlockSpec((1, width), full2),
            pl.BlockSpec((1, width), full2),
            pl.BlockSpec((width, width), full2),
        ],
        out_specs=[spec1, spec1, spec1, spec1, spec2, spec2, spec2],
        out_shape=[out1, out1, out1, out1, out_lw, out2, out2],
        compiler_params=pltpu.CompilerParams(
            dimension_semantics=("arbitrary", "arbitrary"), vmem_limit_bytes=VMEM_LIMIT),
        name="rwkv_prep",
    )(p_rw, p_rw, p_rw, mu, w0, w2, a0, a2, g2, k_k.reshape(1, width), k_a.reshape(1, width),
      _head_block_diag(width))


def _rw_scan_kernel(r_ref, lw_ref, kd_ref, v_ref, kk_ref, bb_ref, y_ref, s_ref, *, heads, bsz):
    d = pl.program_id(0)
    i = pl.program_id(1)

    @pl.when(i == 0)
    def _():
        s_ref[...] = jnp.zeros_like(s_ref)

    row = lax.broadcasted_iota(jnp.int32, (CH, CH), 0)
    col = lax.broadcasted_iota(jnp.int32, (CH, CH), 1)
    sgn = jnp.where(d == 0, 1, -1)
    strict = (row - col) * sgn > 0
    diag = row == col
    incl = jnp.logical_or(strict, diag)
    tri = jnp.where(incl, 1.0, 0.0).astype(BF16)
    eye = jnp.where(diag, 1.0, 0.0)
    halves = _half_block_masks(row, col, sgn)

    hs = range(bsz * heads)
    sls = [slice(h * HEAD, (h + 1) * HEAD) for h in range(heads)] * bsz
    s0 = [s_ref[h] for h in hs]
    v, kkr, bk, b_w, k_w, wtot = [], [], [], [], [], []
    for b in range(bsz):
        lw = lw_ref[0, b]
        cw = _dot_sel(tri, lw)
        ctot = jnp.sum(lw, axis=0, keepdims=True)
        e_ncw = jnp.exp(-cw)
        e_rem = jnp.exp(ctot - cw)
        r_t = (r_ref[b].astype(F32) * jnp.exp(cw)).astype(BF16)
        kk_t = (kk_ref[b].astype(F32) * jnp.exp(cw - lw)).astype(BF16)
        bb = bb_ref[0, b].astype(F32)
        kd = kd_ref[0, b].astype(F32)
        b_t = (bb * e_ncw).astype(BF16)
        k_t = (kd * e_ncw).astype(BF16)
        bw_all = (bb * e_rem).astype(BF16)
        kw_all = (kd * e_rem).astype(BF16)
        wt_all = jnp.exp(ctot)
        v_all = v_ref[b]
        for sl in sls[:heads]:
            v.append(v_all[:, sl])
            kkr.append(jnp.concatenate([kk_t[:, sl], r_t[:, sl]], axis=0))
            bk.append(jnp.concatenate([b_t[:, sl], k_t[:, sl]], axis=0))
            b_w.append(bw_all[:, sl])
            k_w.append(kw_all[:, sl])
            wtot.append(wt_all[:, sl])
    att = [_dot_nt(kkr[h], bk[h]) for h in hs]
    proj = [_dot_nt(kkr[h], s0[h].astype(BF16)) for h in hs]
    a_ab = [jnp.where(strict, att[h][:CH, :CH], 0.0) for h in hs]
    a_ak = [jnp.where(strict, att[h][:CH, CH:], 0.0).astype(BF16) for h in hs]
    a_rb = [jnp.where(incl, att[h][CH:, :CH], 0.0).astype(BF16) for h in hs]
    a_rk = [jnp.where(incl, att[h][CH:, CH:], 0.0).astype(BF16) for h in hs]
    inv = [eye - jnp.where(halves[0], a_ab[h], 0.0) for h in hs]
    for lvl in range(1, len(halves)):
        inv_b = [inv[h].astype(BF16) for h in hs]
        a_l = [jnp.where(halves[lvl], a_ab[h], 0.0).astype(BF16) for h in hs]
        left = [_dot(inv_b[h], a_l[h]).astype(BF16) for h in hs]
        inv = [inv[h] - _dot(left[h], inv_b[h]) for h in hs]
    rhs = [(proj[h][:CH] + _dot(a_ak[h], v[h])).astype(BF16) for h in hs]
    u_b = [(-_dot(inv[h].astype(BF16), rhs[h])).astype(BF16) for h in hs]
    y = [proj[h][CH:] + _dot(a_rb[h], u_b[h]) + _dot(a_rk[h], v[h]) for h in hs]
    s_new = [s0[h] * wtot[h] + _dot_tn(u_b[h], b_w[h]) + _dot_tn(v[h], k_w[h]) for h in hs]
    for h in hs:
        y_ref[0, h // heads, :, sls[h]] = y[h].astype(y_ref.dtype)
        s_ref[h] = s_new[h]


def _rw_scan(r, lw, kd, v, kk, bb, n_ctx):
    bsz, t, width = r.shape
    heads = width // HEAD
    nc = t // CH
    ncc = n_ctx // CH
    s1 = pl.BlockSpec((bsz, CH, width), lambda d, i: (0, _scan_chunk(d, i, ncc, nc), 0))
    s2 = pl.BlockSpec((1, bsz, CH, width), lambda d, i: (d, 0, _scan_chunk(d, i, ncc, nc), 0))
    return pl.pallas_call(
        functools.partial(_rw_scan_kernel, heads=heads, bsz=bsz),
        grid=(2, nc),
        in_specs=[s1, s2, s2, s1, s1, s2],
        out_specs=s2,
        out_shape=jax.ShapeDtypeStruct((2, bsz, t, width), BF16),
        scratch_shapes=[pltpu.VMEM((bsz * heads, HEAD, HEAD), F32)],
        compiler_params=pltpu.CompilerParams(
            dimension_semantics=("arbitrary", "arbitrary"), vmem_limit_bytes=VMEM_LIMIT),
        name="rwkv_scan",
    )(r, lw, kd, v, kk, bb)


HG_LEVELS = 6


def _hg_decay_matrices():
    out = np.zeros((2, HG_LEVELS + 2, CH, CH), np.float32)
    pos = np.arange(CH)
    for d in range(2):
        for t in range(CH):
            if d == 0:
                out[d, 0, t, :t + 1] = 1.0
                out[d, 1, t, t + 1:] = 1.0
            else:
                out[d, 0, t, t:] = 1.0
                out[d, 1, t, :t] = 1.0
            for l in range(1, HG_LEVELS + 1):
                n = 1 << l
                mid = (t // n) * n + n // 2
                if d == 0:
                    sel = (pos >= mid) & (pos <= t) if t >= mid else (pos > t) & (pos < mid)
                else:
                    sel = (pos >= t) & (pos < mid) if t < mid else (pos >= mid) & (pos < t)
                out[d, 1 + l, t, sel] = 1.0
    return jnp.asarray(out.reshape(2, (HG_LEVELS + 2) * CH, CH), dtype=BF16)


def _hg_scan_kernel(q_ref, f_ref, i_ref, lb_ref, dm_ref, o_ref, s_ref, *, heads, bsz):
    d = pl.program_id(0)
    step = pl.program_id(1)

    @pl.when(step == 0)
    def _():
        s_ref[...] = jnp.zeros_like(s_ref)

    row = lax.broadcasted_iota(jnp.int32, (CH, CH), 0)
    col = lax.broadcasted_iota(jnp.int32, (CH, CH), 1)
    sgn = jnp.where(d == 0, 1, -1)

    lb = lb_ref[...]
    masks = _half_block_masks(row, col, sgn)
    lvl_masks = [row == col] + masks
    hs = range(bsz * heads)
    sls = [slice(h * HEAD, (h + 1) * HEAD) for h in range(heads)] * bsz
    q_lvl, k_lvl, q_in, k_out, etot, v_all = [], [], [], [], [], []
    for b in range(bsz):
        f = f_ref[b].astype(F32)
        q = q_ref[b].astype(F32)
        q = q * _sigmoid(q)
        log_sig = jnp.minimum(f, 0.0) - jnp.log(1.0 + jnp.exp(-jnp.abs(f)))
        hi_term = jnp.log(1.0 - lb) + log_sig
        lo_term = jnp.log(jnp.where(lb > 0.0, lb, 1.0))
        mx = jnp.maximum(hi_term, lo_term)
        lf_mix = mx + jnp.log(1.0 + jnp.exp(-jnp.abs(hi_term - lo_term)))
        lf = jnp.where(lb > 0.0, lf_mix, hi_term)
        key = (1.0 - lb) * _sigmoid(-f)
        ex = jnp.exp(_dot_sel(dm_ref[0], lf))
        q_in.append((q * ex[0:CH]).astype(BF16))
        k_out.append((key * ex[CH:2 * CH]).astype(BF16))
        etot.append(jnp.exp(jnp.sum(lf, axis=0, keepdims=True)))
        v_all.append(i_ref[b].astype(BF16))
        q_lvl.append([q.astype(BF16)] + [(q * ex[(1 + l) * CH:(2 + l) * CH]).astype(BF16)
                                         for l in range(1, HG_LEVELS + 1)])
        k_lvl.append([key.astype(BF16)] + [(key * ex[(1 + l) * CH:(2 + l) * CH]).astype(BF16)
                                           for l in range(1, HG_LEVELS + 1)])

    bi = [h // heads for h in hs]
    s0 = [s_ref[h] for h in hs]
    sc = [jnp.zeros((CH, CH), F32) for _ in hs]
    for l in range(HG_LEVELS + 1):
        sc = [sc[h] + jnp.where(lvl_masks[l], _dot_nt(q_lvl[bi[h]][l][:, sls[h]], k_lvl[bi[h]][l][:, sls[h]]), 0.0)
              for h in hs]
    o = [_dot(sc[h].astype(BF16), v_all[bi[h]][:, sls[h]]) + _dot_nt(q_in[bi[h]][:, sls[h]], s0[h].astype(BF16))
         for h in hs]
    s_new = [s0[h] * etot[bi[h]][:, sls[h]] + _dot_tn(v_all[bi[h]][:, sls[h]], k_out[bi[h]][:, sls[h]]) for h in hs]
    for h in hs:
        o_ref[0, bi[h], :, sls[h]] = o[h].astype(o_ref.dtype)
        s_ref[h] = s_new[h]


def _hg_scan(p_hg, lb, n_ctx, col0=0):
    bsz, t, _ = p_hg.shape
    width = lb.shape[0]
    heads = width // HEAD
    nc = t // CH
    ncc = n_ctx // CH
    dm = _hg_decay_matrices()
    chunk = lambda d, i: _scan_chunk(d, i, ncc, nc)
    blk = (bsz, CH, width)
    return pl.pallas_call(
        functools.partial(_hg_scan_kernel, heads=heads, bsz=bsz),
        grid=(2, nc),
        in_specs=[
            pl.BlockSpec(blk, lambda d, i: (0, chunk(d, i), col0)),
            pl.BlockSpec(blk, lambda d, i: (0, chunk(d, i), col0 + 1 + d)),
            pl.BlockSpec(blk, lambda d, i: (0, chunk(d, i), col0 + 3)),
            pl.BlockSpec((1, width), lambda d, i: (0, 0)),
            pl.BlockSpec((1,) + dm.shape[1:], lambda d, i: (d, 0, 0)),
        ],
        out_specs=pl.BlockSpec((1, bsz, CH, width), lambda d, i: (d, 0, chunk(d, i), 0)),
        out_shape=jax.ShapeDtypeStruct((2, bsz, t, width), BF16),
        scratch_shapes=[pltpu.VMEM((bsz * heads, HEAD, HEAD), F32)],
        compiler_params=pltpu.CompilerParams(
            dimension_semantics=("arbitrary", "arbitrary"), vmem_limit_bytes=VMEM_LIMIT),
        name="hgrn2_scan",
    )(p_hg, p_hg, p_hg, lb.reshape(1, width), dm)


def _s5_operators(lam_re, lam_im, log_dt, b_re, b_im, c_re, c_im):
    c = S5_CHUNK
    hp = lax.Precision.HIGHEST
    dt = jnp.exp(log_dt.astype(F32))[..., None]
    lam_re = lam_re.astype(F32)
    lam_im = lam_im.astype(F32)
    mag = jnp.exp(lam_re * dt)
    ang = lam_im * dt
    a_re, a_im = mag * jnp.cos(ang), mag * jnp.sin(ang)
    den = lam_re * lam_re + lam_im * lam_im
    f_re = ((a_re - 1.0) * lam_re + a_im * lam_im) / den
    f_im = (a_im * lam_re - (a_re - 1.0) * lam_im) / den
    bb_re = f_re[..., None] * b_re - f_im[..., None] * b_im
    bb_im = f_re[..., None] * b_im + f_im[..., None] * b_re
    j = jnp.arange(c + 1, dtype=F32)[None, None, :, None]
    pmag = jnp.exp((lam_re * dt)[:, :, None, :] * j)
    pang = ang[:, :, None, :] * j
    pw_re, pw_im = pmag * jnp.cos(pang), pmag * jnp.sin(pang)
    ca_re = c_re[:, :, None] * pw_re[:, :, :, None] - c_im[:, :, None] * pw_im[:, :, :, None]
    ca_im = c_re[:, :, None] * pw_im[:, :, :, None] + c_im[:, :, None] * pw_re[:, :, :, None]
    kern = (jnp.einsum('dgjcp,dgpk->dgjck', ca_re, bb_re, precision=hp)
            - jnp.einsum('dgjcp,dgpk->dgjck', ca_im, bb_im, precision=hp))
    tpos = np.arange(c)
    lag = tpos[None, :] - tpos[:, None]
    kf = jnp.where((lag >= 0)[None, :, :, None, None], kern[0][:, np.clip(lag, 0, c)], 0.0)
    kb = jnp.where((lag <= 0)[None, :, :, None, None], kern[1][:, np.clip(-lag, 0, c)], 0.0)
    g = kern.shape[1]
    tmat = jnp.transpose(kf + kb, (0, 1, 4, 2, 3)).reshape(g, c * S5_CH, c * S5_CH)

    def in_to_state(pr, pi, d):
        hr = pr[:, :, None, :] * jnp.swapaxes(bb_re[d], 1, 2)[:, None] - pi[:, :, None, :] * jnp.swapaxes(bb_im[d], 1, 2)[:, None]
        hi = pr[:, :, None, :] * jnp.swapaxes(bb_im[d], 1, 2)[:, None] + pi[:, :, None, :] * jnp.swapaxes(bb_re[d], 1, 2)[:, None]
        return hr.reshape(g, c * S5_CH, -1), hi.reshape(g, c * S5_CH, -1)

    hf_re, hf_im = in_to_state(pw_re[0, :, c - 1::-1][:, :c], pw_im[0, :, c - 1::-1][:, :c], 0)
    hb_re, hb_im = in_to_state(pw_re[1, :, :c], pw_im[1, :, :c], 1)
    hmat = jnp.concatenate([hf_re, hb_re, hf_im, hb_im], axis=-1)

    def state_to_out(cr, ci):
        return (jnp.transpose(cr, (0, 3, 1, 2)).reshape(g, -1, c * S5_CH),
                -jnp.transpose(ci, (0, 3, 1, 2)).reshape(g, -1, c * S5_CH))

    gf_re, gf_im = state_to_out(ca_re[0, :, 1:c + 1], ca_im[0, :, 1:c + 1])
    gb_re, gb_im = state_to_out(ca_re[1, :, c:0:-1], ca_im[1, :, c:0:-1])
    z = jnp.zeros_like(gf_re)
    gf = jnp.concatenate([gf_re, z, gf_im, z], axis=1)
    gb = jnp.concatenate([z, gb_re, z, gb_im], axis=1)
    ac_re = jnp.concatenate([pw_re[0, :, c], pw_re[1, :, c]], axis=-1)[:, None, :]
    ac_im = jnp.concatenate([pw_im[0, :, c], pw_im[1, :, c]], axis=-1)[:, None, :]
    return tmat.astype(BF16), hmat.astype(BF16), gf.astype(BF16), gb.astype(BF16), ac_re, ac_im


def _s5_local_kernel(u_ref, h_ref, loc_ref):
    loc_ref[0] = _dot(u_ref[0].astype(BF16), h_ref[0])


def _s5_state_kernel(loc_ref, ar_ref, ai_ref, sf_ref, sb_ref, *, nc, ncc, bsz):
    ar = ar_ref[0]
    ai = ai_ref[0]
    half = ar.shape[-1]
    is_fwd = lax.broadcasted_iota(jnp.int32, (1, half), 1) < half // 2

    def body(i, carry):
        nb = jnp.where(i < ncc, ncc - 1 - i, nc + ncc - 1 - i)
        new = []
        for b in range(bsz):
            re, im = carry[b]
            st = jnp.concatenate([re, im], axis=1)
            sf_ref[0, b, pl.ds(i, 1), :] = st
            sb_ref[0, b, pl.ds(nb, 1), :] = st
            lf = loc_ref[0, b, pl.ds(i, 1), :]
            lk = loc_ref[0, b, pl.ds(nb, 1), :]
            l_re = jnp.where(is_fwd, lf[:, :half], lk[:, :half])
            l_im = jnp.where(is_fwd, lf[:, half:], lk[:, half:])
            new.append((ar * re - ai * im + l_re, ar * im + ai * re + l_im))
        return tuple(new)

    zero = jnp.zeros((1, half), F32)
    lax.fori_loop(0, nc, body, tuple((zero, zero) for _ in range(bsz)))


def _s5_out_kernel(u_ref, t_ref, sf_ref, gf_ref, sb_ref, gb_ref, d_ref, y_ref):
    u = u_ref[0]
    y = (_dot(u.astype(BF16), t_ref[0]) + _dot(sf_ref[0].astype(BF16), gf_ref[0])
         + _dot(sb_ref[0].astype(BF16), gb_ref[0]) + d_ref[0] * u.astype(F32))
    gelu = 0.5 * y * (1.0 + jnp.tanh(math.sqrt(2.0 / math.pi) * (y + 0.044715 * (y * y * y))))
    y_ref[0] = gelu.astype(y_ref.dtype)


def _s5_scan(p_s5, n_ctx, lam_re, lam_im, log_dt, b_re, b_im, c_re, c_im, d_skip):
    bsz, t, width = p_s5.shape
    g = width // S5_CH
    c = S5_CHUNK
    nc, ncc = t // c, n_ctx // c
    blk = c * S5_CH
    rows = bsz * nc
    tmat, hmat, gf, gb, ac_re, ac_im = _s5_operators(lam_re, lam_im, log_dt, b_re, b_im, c_re, c_im)
    u = jnp.transpose(p_s5.astype(F32).reshape(bsz, nc, c, g, S5_CH), (3, 0, 1, 2, 4)).reshape(g, rows, blk)
    per_g = lambda gi: (gi, 0, 0)
    cparams = pltpu.CompilerParams(dimension_semantics=("arbitrary",), vmem_limit_bytes=VMEM_LIMIT)
    loc = pl.pallas_call(
        _s5_local_kernel, grid=(g,),
        in_specs=[pl.BlockSpec((1, rows, blk), per_g), pl.BlockSpec((1, blk, 256), per_g)],
        out_specs=pl.BlockSpec((1, rows, 256), per_g),
        out_shape=jax.ShapeDtypeStruct((g, rows, 256), F32),
        compiler_params=cparams, name="s5_local",
    )(u, hmat)
    per_g4 = lambda gi: (gi, 0, 0, 0)
    st_shape = jax.ShapeDtypeStruct((g, bsz, nc, 256), F32)
    sf, sb = pl.pallas_call(
        functools.partial(_s5_state_kernel, nc=nc, ncc=ncc, bsz=bsz), grid=(g,),
        in_specs=[pl.BlockSpec((1, bsz, nc, 256), per_g4), pl.BlockSpec((1, 1, 128), per_g),
                  pl.BlockSpec((1, 1, 128), per_g)],
        out_specs=[pl.BlockSpec((1, bsz, nc, 256), per_g4)] * 2,
        out_shape=[st_shape, st_shape],
        compiler_params=cparams, name="s5_state",
    )(loc.reshape(g, bsz, nc, 256), ac_re, ac_im)
    d_flat = jnp.tile(d_skip.astype(F32).reshape(g, 1, S5_CH), (1, c, 1)).reshape(g, 1, blk)
    y = pl.pallas_call(
        _s5_out_kernel, grid=(g,),
        in_specs=[pl.BlockSpec((1, rows, blk), per_g), pl.BlockSpec((1, blk, blk), per_g),
                  pl.BlockSpec((1, rows, 256), per_g), pl.BlockSpec((1, 256, blk), per_g),
                  pl.BlockSpec((1, rows, 256), per_g), pl.BlockSpec((1, 256, blk), per_g),
                  pl.BlockSpec((1, 1, blk), per_g)],
        out_specs=pl.BlockSpec((1, rows, blk), per_g),
        out_shape=jax.ShapeDtypeStruct((g, rows, blk), F32),
        compiler_params=cparams, name="s5_out",
    )(u, tmat, sf.reshape(g, rows, 256), gf, sb.reshape(g, rows, 256), gb, d_flat)
    return jnp.transpose(y.reshape(g, bsz, nc, c, S5_CH), (1, 2, 3, 0, 4)).reshape(bsz, t, width)


IN_BM = 1280
IN_BN = 768
OUT_BM = 256
ROUTER_PAD = 128


def _row_block(t, target):
    bm = target
    while t % bm:
        bm -= CH
    return bm


def _norm_mod(x, mod_ref, first_row, n_ctx):
    xn = x * lax.rsqrt(jnp.mean(x * x, axis=-1, keepdims=True) + NORM_EPS)
    is_ctx = (first_row + lax.broadcasted_iota(jnp.int32, (x.shape[0], 1), 0)) < n_ctx
    scale = jnp.where(is_ctx, mod_ref[0, 0, 0:1, :], mod_ref[0, 1, 0:1, :])
    shift = jnp.where(is_ctx, mod_ref[0, 0, 1:2, :], mod_ref[0, 1, 1:2, :])
    return xn * scale + shift


def _in_proj_kernel(x_ref, mod_ref, w_ref, p_ref, h_ref, *, n_ctx, bm):
    @pl.when(pl.program_id(2) == 0)
    def _():
        sub = _row_block(bm, 256)

        def body(k, carry):
            rows = pl.ds(pl.multiple_of(k * sub, sub), sub)
            h = _norm_mod(x_ref[0, rows, :], mod_ref, pl.program_id(1) * bm + k * sub, n_ctx)
            h_ref[rows, :] = h.astype(BF16)
            return carry

        lax.fori_loop(0, bm // sub, body, 0)

    p_ref[0] = _dot(h_ref[...], w_ref[...]).astype(p_ref.dtype)


def _in_proj(tok, mod, w, n_ctx):
    bsz, t, d = tok.shape
    n = w.shape[1]
    bm = _row_block(t, IN_BM)
    return pl.pallas_call(
        functools.partial(_in_proj_kernel, n_ctx=n_ctx, bm=bm),
        grid=(bsz, t // bm, n // IN_BN),
        in_specs=[pl.BlockSpec((1, bm, d), lambda b, i, j: (b, i, 0), pipeline_mode=pl.Buffered(1)),
                  pl.BlockSpec((1, 2, 2, d), lambda b, i, j: (b, 0, 0, 0)),
                  pl.BlockSpec((d, IN_BN), lambda b, i, j: (0, j))],
        out_specs=pl.BlockSpec((1, bm, IN_BN), lambda b, i, j: (b, i, j)),
        out_shape=jax.ShapeDtypeStruct((bsz, t, n), BF16),
        scratch_shapes=[pltpu.VMEM((bm, d), BF16)],
        compiler_params=pltpu.CompilerParams(
            dimension_semantics=("arbitrary", "arbitrary", "arbitrary"), vmem_limit_bytes=VMEM_LIMIT),
        name="in_proj",
    )(tok, mod, w)


def _out_proj_kernel(s5_ref, wglu_ref, bglu_ref,
                     y2_ref, r_ref, kd_ref, v_ref, g_ref, rk_ref, lnw_ref, lnb_ref,
                     o2_ref, og_ref, hgn_ref, bd_ref,
                     x_ref, gate_ref, wout_ref, modf_ref, wr_ref, br_ref,
                     xo_ref, h_ref, lg_ref, *, n_ctx, bm):
    first = pl.program_id(1) * bm
    bd = bd_ref[...]
    inv_n = 1.0 / HEAD

    def head_sum(z):
        return _dot(z.astype(BF16), bd)

    ys = s5_ref[0].astype(F32)
    ys = ys * _sigmoid(_dot(ys.astype(BF16), wglu_ref[...]) + bglu_ref[...])
    y = y2_ref[0, 0].astype(F32) + y2_ref[1, 0].astype(F32)
    yc = y - head_sum(y) * inv_n
    var = head_sum(yc * yc) * inv_n
    k_bonus = 0.5 * (kd_ref[0, 0].astype(F32) + kd_ref[1, 0].astype(F32))
    bonus = head_sum(r_ref[0].astype(F32) * k_bonus * rk_ref[...]) * v_ref[0].astype(F32)
    yr = (yc * lax.rsqrt(var + RW_GN_EPS) * lnw_ref[...] + lnb_ref[...] + bonus) * g_ref[0].astype(F32)
    o = o2_ref[0, 0].astype(F32) + o2_ref[1, 0].astype(F32)
    yh = o * lax.rsqrt(head_sum(o * o) * inv_n + NORM_EPS) * hgn_ref[...] * _sigmoid(og_ref[0].astype(F32))

    m = _dot(jnp.concatenate([ys, yr, yh], axis=-1).astype(BF16), wout_ref[...])
    is_ctx = (first + lax.broadcasted_iota(jnp.int32, (bm, 1), 0)) < n_ctx
    x_new = x_ref[0] + jnp.where(is_ctx, gate_ref[0, 0:1, :], gate_ref[0, 1:2, :]) * m
    xo_ref[0] = x_new
    h = _norm_mod(x_new, modf_ref, first, n_ctx).astype(BF16)
    h_ref[0] = h
    lg_ref[0] = _dot(h, wr_ref[...]) + br_ref[...]


def _out_proj(s5y, w_glu, b_glu, y2, r, kd, v, g, r_k, ln_w, ln_b, o2, p, og_col, hg_norm,
              tok, gate, w_out, mod_f, w_router, b_router, n_ctx):
    bsz, t, d = tok.shape
    bm = _row_block(t, OUT_BM)
    s5w = s5y.shape[-1]
    w = r.shape[-1]
    row = lambda b, i: (b, i, 0)
    row2 = lambda b, i: (0, b, i, 0)
    c2 = lambda b, i: (0, 0)
    vec = lambda z: z.reshape(1, -1).astype(F32)
    spec_w = pl.BlockSpec((1, bm, w), row)
    spec_2w = pl.BlockSpec((2, 1, bm, w), row2)
    spec_vw = pl.BlockSpec((1, w), c2)
    return pl.pallas_call(
        functools.partial(_out_proj_kernel, n_ctx=n_ctx, bm=bm),
        grid=(bsz, t // bm),
        in_specs=[
            pl.BlockSpec((1, bm, s5w), row), pl.BlockSpec((s5w, s5w), c2), pl.BlockSpec((1, s5w), c2),
            spec_2w, spec_w, spec_2w, spec_w, spec_w, spec_vw, spec_vw, spec_vw,
            spec_2w, pl.BlockSpec((1, bm, w), lambda b, i: (b, i, og_col)), spec_vw, pl.BlockSpec((w, w), c2),
            pl.BlockSpec((1, bm, d), row), pl.BlockSpec((1, 2, d), lambda b, i: (b, 0, 0)),
            pl.BlockSpec(w_out.shape, c2), pl.BlockSpec((1, 2, 2, d), lambda b, i: (b, 0, 0, 0)),
            pl.BlockSpec((d, ROUTER_PAD), c2), pl.BlockSpec((1, ROUTER_PAD), c2),
        ],
        out_specs=[pl.BlockSpec((1, bm, d), row), pl.BlockSpec((1, bm, d), row),
                   pl.BlockSpec((1, bm, ROUTER_PAD), row)],
        out_shape=[jax.ShapeDtypeStruct((bsz, t, d), F32), jax.ShapeDtypeStruct((bsz, t, d), BF16),
                   jax.ShapeDtypeStruct((bsz, t, ROUTER_PAD), F32)],
        compiler_params=pltpu.CompilerParams(
            dimension_semantics=("arbitrary", "arbitrary"), vmem_limit_bytes=VMEM_LIMIT),
        name="out_proj",
    )(s5y, w_glu.astype(BF16), vec(b_glu), y2, r, kd, v, g, vec(r_k), vec(ln_w), vec(ln_b),
      o2, p, vec(hg_norm), _head_block_diag(w), tok, gate, w_out.astype(BF16), mod_f,
      w_router.astype(BF16), b_router)


def _final_norm_kernel(x_ref, g_ref, o_ref):
    x = x_ref[0]
    o_ref[0] = x * lax.rsqrt(jnp.mean(x * x, axis=-1, keepdims=True) + NORM_EPS) * g_ref[...]


def _final_norm(x, g):
    bsz, t, d = x.shape
    bm = _row_block(t, 512)
    return pl.pallas_call(
        _final_norm_kernel, grid=(bsz, t // bm),
        in_specs=[pl.BlockSpec((1, bm, d), lambda b, i: (b, i, 0)), pl.BlockSpec((1, d), lambda b, i: (0, 0))],
        out_specs=pl.BlockSpec((1, bm, d), lambda b, i: (b, i, 0)),
        out_shape=jax.ShapeDtypeStruct((bsz, t, d), F32),
        compiler_params=pltpu.CompilerParams(dimension_semantics=("arbitrary", "arbitrary")),
        name="final_norm",
    )(x, g.reshape(1, d).astype(F32))


def _moe_kernel(meta_ref, x_ref, w1_ref, w3_ref, w2_ref, y_ref, *, n_blocks):
    active = pl.program_id(0) < meta_ref[n_blocks]

    @pl.when(active)
    def _():
        x = x_ref[...]
        a = _dot(x, w1_ref[0])
        b = _dot(x, w3_ref[0])
        y_ref[...] = _dot((a * _sigmoid(a) * b).astype(BF16), w2_ref[0]).astype(y_ref.dtype)

    @pl.when(jnp.logical_not(active))
    def _():
        y_ref[...] = jnp.zeros_like(y_ref)


def _cast_kernel(x_ref, o_ref):
    o_ref[0] = x_ref[0, 0].astype(o_ref.dtype)


def _layer_to_bf16(w, layer):
    _, e, r, c = w.shape
    return pl.pallas_call(
        _cast_kernel, grid=(e,),
        in_specs=[pl.BlockSpec((1, 1, r, c), lambda i: (layer, i, 0, 0))],
        out_specs=pl.BlockSpec((1, r, c), lambda i: (i, 0, 0)),
        out_shape=jax.ShapeDtypeStruct((e, r, c), BF16),
        compiler_params=pltpu.CompilerParams(dimension_semantics=("arbitrary",), vmem_limit_bytes=VMEM_LIMIT),
        name="cast_bf16",
    )(w)


def _moe_experts(xg, meta, w1, w3, w2):
    n_rows, d = xg.shape
    n_blocks = n_rows // MOE_BLOCK
    de = w1.shape[2]
    return pl.pallas_call(
        functools.partial(_moe_kernel, n_blocks=n_blocks),
        grid_spec=pltpu.PrefetchScalarGridSpec(
            num_scalar_prefetch=1, grid=(n_blocks,),
            in_specs=[pl.BlockSpec((MOE_BLOCK, d), lambda i, m: (i, 0)),
                      pl.BlockSpec((1, d, de), lambda i, m: (m[i], 0, 0)),
                      pl.BlockSpec((1, d, de), lambda i, m: (m[i], 0, 0)),
                      pl.BlockSpec((1, de, d), lambda i, m: (m[i], 0, 0))],
            out_specs=pl.BlockSpec((MOE_BLOCK, d), lambda i, m: (i, 0))),
        out_shape=jax.ShapeDtypeStruct((n_rows, d), BF16),
        compiler_params=pltpu.CompilerParams(dimension_semantics=("arbitrary",), vmem_limit_bytes=VMEM_LIMIT),
        name="moe_experts",
    )(meta, xg, w1, w3, w2)


def _moe_ffn(h, logits, n_groups, n_experts, w1, w3, w2):
    n, d = h.shape
    epg = n_experts // n_groups
    group_logits = logits[:, :n_groups]
    g_sel = jnp.argmax(group_logits, axis=-1)
    g_gate = jnp.take_along_axis(jax.nn.softmax(group_logits, axis=-1), g_sel[:, None], axis=1)
    exp_logits = logits[:, n_groups:n_groups + n_experts].reshape(n, n_groups, epg)
    within = jnp.take_along_axis(exp_logits, g_sel[:, None, None], axis=1)[:, 0]
    top_val, top_idx = lax.top_k(within, TOP_K)
    gate = jax.nn.softmax(top_val, axis=-1) * g_gate
    e_flat = (g_sel[:, None] * epg + top_idx).reshape(-1).astype(jnp.int32)
    n_assign = n * TOP_K
    order = jnp.argsort(e_flat)
    rank = jnp.argsort(order).astype(jnp.int32)
    counts = jnp.sum(e_flat[:, None] == jnp.arange(n_experts, dtype=jnp.int32)[None, :], axis=0, dtype=jnp.int32)
    starts = jnp.cumsum(counts) - counts
    padded = (counts + MOE_BLOCK - 1) // MOE_BLOCK * MOE_BLOCK
    pad_ends = jnp.cumsum(padded)
    pad_starts = pad_ends - padded
    n_blocks = -(-n_assign // MOE_BLOCK) + n_experts
    n_rows = n_blocks * MOE_BLOCK
    block_exp = jnp.minimum(jnp.searchsorted(pad_ends, jnp.arange(n_blocks, dtype=jnp.int32) * MOE_BLOCK, side='right'), n_experts - 1).astype(jnp.int32)
    rows = jnp.arange(n_rows, dtype=jnp.int32)
    row_exp = jnp.repeat(block_exp, MOE_BLOCK)
    offset = rows - pad_starts[row_exp]
    src = jnp.clip(starts[row_exp] + offset, 0, n_assign - 1)
    row_tok = jnp.where(offset < counts[row_exp], order[src].astype(jnp.int32) // TOP_K, n)
    meta = jnp.concatenate([block_exp, (pad_ends[-1:] // MOE_BLOCK).astype(jnp.int32)])
    h_pad = jnp.concatenate([h, jnp.zeros((1, d), h.dtype)], axis=0)
    y_rows = _moe_experts(h_pad[row_tok], meta, w1, w3, w2)
    pos = (pad_starts[e_flat] + rank - starts[e_flat]).reshape(n, TOP_K)
    return gate[:, 0:1] * y_rows[pos[:, 0]] + gate[:, 1:2] * y_rows[pos[:, 1]]


def kernel(x, c, ctx, c_ctx, ada_w, ada_b, norm_mix, norm_ffn, w_in, w_out, s5_lam_re, s5_lam_im, s5_log_dt, s5_b_re, s5_b_im, s5_c_re, s5_c_im, s5_d, s5_w_glu, s5_b_glu, rw_mu, rw_w0, rw_w2, rw_a0, rw_a2, rw_g2, rw_k_k, rw_k_a, rw_r_k, rw_ln_w, rw_ln_b, hg_lb_logits, hg_norm, moe_w_group, moe_b_group, moe_w_expert, moe_b_expert, moe_w1, moe_w3, moe_w2, final_norm):
    bsz, n_lat, d = x.shape
    n_ctx = ctx.shape[1]
    depth = w_in.shape[0]
    s5_width = s5_d.shape[1]
    rw_width = rw_k_k.shape[1]
    rw_in = rw_mu.shape[2]
    hg_width = hg_norm.shape[1]
    n_groups = moe_w_group.shape[2]
    n_experts = moe_w_expert.shape[2]
    t = n_ctx + n_lat
    hg_col0 = -(-rw_in // hg_width)
    s5_off = (hg_col0 + 5) * hg_width
    n_proj = -(-(s5_off + s5_width) // IN_BN) * IN_BN
    pad1 = hg_col0 * hg_width - rw_in
    pad2 = n_proj - s5_off - s5_width

    silu_c = jax.nn.silu(c)
    silu_cc = jax.nn.silu(c_ctx)
    lb_all = jnp.cumsum(jax.nn.softmax(hg_lb_logits.astype(F32), axis=0), axis=0)
    lb_all = lb_all - lb_all[0:1]
    tok = jnp.concatenate([ctx, x], axis=1)
    for l in range(depth):
        last = l == depth - 1
        mod_lat = jnp.split(silu_c @ ada_w[l] + ada_b[l], 6, axis=-1)
        mod_ctx = [jnp.broadcast_to(z, (bsz, d)) for z in jnp.split(silu_cc @ ada_w[l] + ada_b[l], 6, axis=-1)]
        sh_m, sc_m, gt_m, sh_f, sc_f, gt_f = [jnp.stack([zc, zl], axis=1) for zc, zl in zip(mod_ctx, mod_lat)]
        mod_m = jnp.stack([norm_mix[l] * (1.0 + sc_m), sh_m], axis=2)
        mod_f = jnp.stack([norm_ffn[l] * (1.0 + sc_f), sh_f], axis=2)

        w_cols = [w_in[l][:, s5_width:s5_width + rw_in], jnp.zeros((d, pad1), F32),
                  w_in[l][:, s5_width + rw_in:], w_in[l][:, :s5_width], jnp.zeros((d, pad2), F32)]
        p = _in_proj(tok, mod_m, jnp.concatenate(w_cols, axis=1).astype(BF16), n_ctx)

        r, v, kk, g, lw, kd, bb = _rw_prep(p, n_ctx, rw_mu[l], rw_w0[l], rw_w2[l], rw_a0[l], rw_a2[l],
                                           rw_g2[l], rw_k_k[l], rw_k_a[l])
        y2 = _rw_scan(r, lw, kd, v, kk, bb, n_ctx)
        o2 = _hg_scan(p, lb_all[l], n_ctx, col0=hg_col0)
        s5y = _s5_scan(p[:, :, s5_off:s5_off + s5_width], n_ctx, s5_lam_re[l], s5_lam_im[l], s5_log_dt[l],
                       s5_b_re[l], s5_b_im[l], s5_c_re[l], s5_c_im[l], s5_d[l])

        w_router = jnp.concatenate([moe_w_group[l], moe_w_expert[l],
                                    jnp.zeros((d, ROUTER_PAD - n_groups - n_experts), F32)], axis=1)
        b_router = jnp.concatenate([moe_b_group[l], moe_b_expert[l],
                                    jnp.zeros((ROUTER_PAD - n_groups - n_experts,), F32)]).reshape(1, ROUTER_PAD)
        tok, h_ffn, logits = _out_proj(
            s5y, s5_w_glu[l], s5_b_glu[l], y2, r, kd, v, g, rw_r_k[l], rw_ln_w[l], rw_ln_b[l],
            o2, p, hg_col0 + 4, hg_norm[l], tok, gt_m, w_out[l], mod_f, w_router, b_router, n_ctx)

        experts = (_layer_to_bf16(moe_w1, l), _layer_to_bf16(moe_w3, l), _layer_to_bf16(moe_w2, l))
        if last:
            f = _moe_ffn(h_ffn[:, n_ctx:].reshape(bsz * n_lat, d), logits[:, n_ctx:].reshape(bsz * n_lat, ROUTER_PAD),
                         n_groups, n_experts, *experts).reshape(bsz, n_lat, d)
            x = tok[:, n_ctx:] + gt_f[:, 1:2] * f
        else:
            f = _moe_ffn(h_ffn.reshape(bsz * t, d), logits.reshape(bsz * t, ROUTER_PAD),
                         n_groups, n_experts, *experts).reshape(bsz, t, d)
            tok = jnp.concatenate([tok[:, :n_ctx] + gt_f[:, 0:1] * f[:, :n_ctx],
                                   tok[:, n_ctx:] + gt_f[:, 1:2] * f[:, n_ctx:]], axis=1)
    return _final_norm(x, final_norm)
```

```python
import functools
import math

import numpy as np
import jax
import jax.numpy as jnp
from jax import lax
from jax.experimental import pallas as pl
from jax.experimental.pallas import tpu as pltpu

F32 = jnp.float32
BF16 = jnp.bfloat16

GRID_W = 64
CH = GRID_W
HEAD = 64
S5_CH = 16
S5_STATE = 64
S5_CHUNK = 32
RW_LORA = 64
RW_GN_EPS = 64e-5
NORM_EPS = 1e-6
TOP_K = 2
MOE_BLOCK = 256
VMEM_LIMIT = 48 * 1024 * 1024


def _dot(a, b):
    return jnp.dot(a, b, preferred_element_type=F32)


def _dot_nt(a, b):
    return lax.dot_general(a, b, (((1,), (1,)), ((), ())), preferred_element_type=F32)


def _dot_tn(a, b):
    return lax.dot_general(a, b, (((0,), (0,)), ((), ())), preferred_element_type=F32)


def _split3(x):
    hi = x.astype(BF16)
    r1 = x - hi.astype(F32)
    mid = r1.astype(BF16)
    lo = (r1 - mid.astype(F32)).astype(BF16)
    return hi, mid, lo


def _dot_sel(m01, x):
    hi, mid, lo = _split3(x)
    return _dot(m01, hi) + _dot(m01, mid) + _dot(m01, lo)


def _dot_hp(a, b):
    a_hi = a.astype(BF16)
    a_lo = (a - a_hi.astype(F32)).astype(BF16)
    b_hi = b.astype(BF16)
    b_lo = (b - b_hi.astype(F32)).astype(BF16)
    return _dot(a_hi, b_hi) + _dot(a_hi, b_lo) + _dot(a_lo, b_hi)


def _sigmoid(x):
    return 1.0 / (1.0 + jnp.exp(-x))


def _scan_chunk(d, i, ncc, nc):
    back = jnp.where(i < ncc, ncc - 1 - i, nc + ncc - 1 - i)
    return jnp.where(d == 0, i, back)


def _half_block_masks(row, col, sgn):
    masks = []
    n = 2
    while n <= CH:
        same = (row // n) == (col // n)
        t_hi = ((row % n) >= n // 2).astype(jnp.int32)
        s_hi = ((col % n) >= n // 2).astype(jnp.int32)
        masks.append(jnp.logical_and(same, (t_hi - s_hi) * sgn == 1))
        n *= 2
    return masks


def _head_block_diag(width):
    idx = np.arange(width) // HEAD
    return jnp.asarray(idx[:, None] == idx[None, :], dtype=BF16)


def _rw_prep_kernel(cur_ref, prev_ref, next_ref, mu_ref, w0_ref, w2_ref, a0_ref, a2_ref, g2_ref,
                    kkw_ref, ka_ref, bd_ref,
                    r_out, v_out, kk_out, g_out, lw_out, kd_out, bb_out, *, ncc, nc, width):
    n = pl.program_id(1)
    cur = cur_ref[0].astype(F32)
    prv = prev_ref[0].astype(F32)
    nxt = next_ref[0].astype(F32)
    is_ctx = n < ncc
    row = lax.broadcasted_iota(jnp.int32, (CH, 1), 0)
    zero_row = jnp.zeros((1, cur.shape[1]), F32)
    first_fill = jnp.where(jnp.logical_and(is_ctx, n > 0), prv[CH - 1:CH, :], zero_row)
    last_fill = jnp.where(jnp.logical_and(is_ctx, n < ncc - 1), nxt[0:1, :], zero_row)
    nb_prev = jnp.where(row == 0, first_fill, pltpu.roll(cur, 1, 0))
    nb_next = jnp.where(row == CH - 1, last_fill, pltpu.roll(cur, CH - 1, 0))
    mu = mu_ref[...]
    zero_mu = jnp.zeros_like(mu[0:1])
    w_up = jnp.where(jnp.logical_and(jnp.logical_not(is_ctx), n > ncc), mu[2:3], zero_mu)
    w_down = jnp.where(jnp.logical_and(jnp.logical_not(is_ctx), n < nc - 1), mu[3:4], zero_mu)
    w_cur = 1.0 - mu[0:1] - mu[1:2] - jnp.where(is_ctx, zero_mu, mu[2:3] + mu[3:4])
    x = w_cur * cur + mu[0:1] * nb_prev + mu[1:2] * nb_next + w_up * prv + w_down * nxt

    w = width
    r = x[:, 0:w]
    k = x[:, w:2 * w]
    v = x[:, 2 * w:3 * w]
    wd = x[:, 3 * w:3 * w + 2 * RW_LORA]
    ad = x[:, 3 * w + 2 * RW_LORA:3 * w + 4 * RW_LORA]
    gd = x[:, 3 * w + 4 * RW_LORA:]

    r_out[0] = r.astype(r_out.dtype)
    v_out[0] = v.astype(v_out.dtype)
    g_out[0] = _dot_hp(_sigmoid(gd), g2_ref[...]).astype(g_out.dtype)
    kk = k * kkw_ref[...]
    kk2 = kk * kk
    kk2_hi = kk2.astype(BF16)
    kk2_lo = (kk2 - kk2_hi.astype(F32)).astype(BF16)
    ss = _dot(kk2_hi, bd_ref[...]) + _dot(kk2_lo, bd_ref[...])
    kk = kk / jnp.maximum(jnp.sqrt(ss), 1e-12)
    kk_out[0] = kk.astype(kk_out.dtype)
    ka = ka_ref[...]
    for d in range(2):
        w_raw = w0_ref[d:d + 1, :] + _dot_hp(jnp.tanh(wd[:, d * RW_LORA:(d + 1) * RW_LORA]), w2_ref[d])
        nw = -w_raw
        softplus = jnp.maximum(nw, 0.0) + jnp.log(1.0 + jnp.exp(-jnp.abs(nw)))
        lw_out[d, 0] = -jnp.exp(-softplus - 0.5)
        a = _sigmoid(a0_ref[d:d + 1, :] + _dot_hp(ad[:, d * RW_LORA:(d + 1) * RW_LORA], a2_ref[d]))
        kd_out[d, 0] = (k * (1.0 + (a - 1.0) * ka)).astype(kd_out.dtype)
        bb_out[d, 0] = (kk * a).astype(bb_out.dtype)


def _rw_prep(p_rw, n_ctx, mu, w0, w2, a0, a2, g2, k_k, k_a):
    bsz, t, _ = p_rw.shape
    win = mu.shape[1]
    width = k_k.shape[0]
    nc = t // CH
    ncc = n_ctx // CH
    tok = lambda b, n: (b, n, 0)
    full2 = lambda b, n: (0, 0)
    full3 = lambda b, n: (0, 0, 0)
    blk_in = (1, CH, win)
    out1 = jax.ShapeDtypeStruct((bsz, t, width), BF16)
    out2 = jax.ShapeDtypeStruct((2, bsz, t, width), BF16)
    out_lw = jax.ShapeDtypeStruct((2, bsz, t, width), F32)
    spec1 = pl.BlockSpec((1, CH, width), tok)
    spec2 = pl.BlockSpec((2, 1, CH, width), lambda b, n: (0, b, n, 0))
    return pl.pallas_call(
        functools.partial(_rw_prep_kernel, ncc=ncc, nc=nc, width=width),
        grid=(bsz, nc),
        in_specs=[
            pl.BlockSpec(blk_in, tok),
            pl.BlockSpec(blk_in, lambda b, n: (b, jnp.maximum(n - 1, 0), 0)),
            pl.BlockSpec(blk_in, lambda b, n: (b, jnp.minimum(n + 1, nc - 1), 0)),
            pl.BlockSpec(mu.shape, full2),
            pl.BlockSpec(w0.shape, full2),
            pl.BlockSpec(w2.shape, full3),
            pl.BlockSpec(a0.shape, full2),
            pl.BlockSpec(a2.shape, full3),
            pl.BlockSpec(g2.shape, full2),
            pl.BlockSpec((1, width), full2),
            pl.BlockSpec((1, width), full2),
            pl.BlockSpec((width, width), full2),
        ],
        out_specs=[spec1, spec1, spec1, spec1, spec2, spec2, spec2],
        out_shape=[out1, out1, out1, out1, out_lw, out2, out2],
        compiler_params=pltpu.CompilerParams(
            dimension_semantics=("arbitrary", "arbitrary"), vmem_limit_bytes=VMEM_LIMIT),
        name="rwkv_prep",
    )(p_rw, p_rw, p_rw, mu, w0, w2, a0, a2, g2, k_k.reshape(1, width), k_a.reshape(1, width),
      _head_block_diag(width))


def _rw_scan_kernel(r_ref, lw_ref, kd_ref, v_ref, kk_ref, bb_ref, y_ref, s_ref, *, heads, bsz):
    d = pl.program_id(0)
    i = pl.program_id(1)

    @pl.when(i == 0)
    def _():
        s_ref[...] = jnp.zeros_like(s_ref)

    row = lax.broadcasted_iota(jnp.int32, (CH, CH), 0)
    col = lax.broadcasted_iota(jnp.int32, (CH, CH), 1)
    sgn = jnp.where(d == 0, 1, -1)
    strict = (row - col) * sgn > 0
    diag = row == col
    incl = jnp.logical_or(strict, diag)
    tri = jnp.where(incl, 1.0, 0.0).astype(BF16)
    eye = jnp.where(diag, 1.0, 0.0)
    halves = _half_block_masks(row, col, sgn)

    hs = range(bsz * heads)
    sls = [slice(h * HEAD, (h + 1) * HEAD) for h in range(heads)] * bsz
    s0 = [s_ref[h] for h in hs]
    v, kkr, bk, b_w, k_w, wtot = [], [], [], [], [], []
    for b in range(bsz):
        lw = lw_ref[0, b]
        cw = _dot_sel(tri, lw)
        ctot = jnp.sum(lw, axis=0, keepdims=True)
        e_ncw = jnp.exp(-cw)
        e_rem = jnp.exp(ctot - cw)
        r_t = (r_ref[b].astype(F32) * jnp.exp(cw)).astype(BF16)
        kk_t = (kk_ref[b].astype(F32) * jnp.exp(cw - lw)).astype(BF16)
        bb = bb_ref[0, b].astype(F32)
        kd = kd_ref[0, b].astype(F32)
        b_t = (bb * e_ncw).astype(BF16)
        k_t = (kd * e_ncw).astype(BF16)
        bw_all = (bb * e_rem).astype(BF16)
        kw_all = (kd * e_rem).astype(BF16)
        wt_all = jnp.exp(ctot)
        v_all = v_ref[b]
        for sl in sls[:heads]:
            v.append(v_all[:, sl])
            kkr.append(jnp.concatenate([kk_t[:, sl], r_t[:, sl]], axis=0))
            bk.append(jnp.concatenate([b_t[:, sl], k_t[:, sl]], axis=0))
            b_w.append(bw_all[:, sl])
            k_w.append(kw_all[:, sl])
            wtot.append(wt_all[:, sl])
    att = [_dot_nt(kkr[h], bk[h]) for h in hs]
    proj = [_dot_nt(kkr[h], s0[h].astype(BF16)) for h in hs]
    a_ab = [jnp.where(strict, att[h][:CH, :CH], 0.0) for h in hs]
    a_ak = [jnp.where(strict, att[h][:CH, CH:], 0.0).astype(BF16) for h in hs]
    a_rb = [jnp.where(incl, att[h][CH:, :CH], 0.0).astype(BF16) for h in hs]
    a_rk = [jnp.where(incl, att[h][CH:, CH:], 0.0).astype(BF16) for h in hs]
    inv = [eye - jnp.where(halves[0], a_ab[h], 0.0) for h in hs]
    for lvl in range(1, len(halves)):
        inv_b = [inv[h].astype(BF16) for h in hs]
        a_l = [jnp.where(halves[lvl], a_ab[h], 0.0).astype(BF16) for h in hs]
        left = [_dot(inv_b[h], a_l[h]).astype(BF16) for h in hs]
        inv = [inv[h] - _dot(left[h], inv_b[h]) for h in hs]
    rhs = [(proj[h][:CH] + _dot(a_ak[h], v[h])).astype(BF16) for h in hs]
    u_b = [(-_dot(inv[h].astype(BF16), rhs[h])).astype(BF16) for h in hs]
    y = [proj[h][CH:] + _dot(a_rb[h], u_b[h]) + _dot(a_rk[h], v[h]) for h in hs]
    s_new = [s0[h] * wtot[h] + _dot_tn(u_b[h], b_w[h]) + _dot_tn(v[h], k_w[h]) for h in hs]
    for h in hs:
        y_ref[0, h // heads, :, sls[h]] = y[h].astype(y_ref.dtype)
        s_ref[h] = s_new[h]


def _rw_scan(r, lw, kd, v, kk, bb, n_ctx):
    bsz, t, width = r.shape
    heads = width // HEAD
    nc = t // CH
    ncc = n_ctx // CH
    s1 = pl.BlockSpec((bsz, CH, width), lambda d, i: (0, _scan_chunk(d, i, ncc, nc), 0))
    s2 = pl.BlockSpec((1, bsz, CH, width), lambda d, i: (d, 0, _scan_chunk(d, i, ncc, nc), 0))
    return pl.pallas_call(
        functools.partial(_rw_scan_kernel, heads=heads, bsz=bsz),
        grid=(2, nc),
        in_specs=[s1, s2, s2, s1, s1, s2],
        out_specs=s2,
        out_shape=jax.ShapeDtypeStruct((2, bsz, t, width), BF16),
        scratch_shapes=[pltpu.VMEM((bsz * heads, HEAD, HEAD), F32)],
        compiler_params=pltpu.CompilerParams(
            dimension_semantics=("arbitrary", "arbitrary"), vmem_limit_bytes=VMEM_LIMIT),
        name="rwkv_scan",
    )(r, lw, kd, v, kk, bb)


HG_LEVELS = 6


def _hg_decay_matrices():
    out = np.zeros((2, HG_LEVELS + 2, CH, CH), np.float32)
    pos = np.arange(CH)
    for d in range(2):
        for t in range(CH):
            if d == 0:
                out[d, 0, t, :t + 1] = 1.0
                out[d, 1, t, t + 1:] = 1.0
            else:
                out[d, 0, t, t:] = 1.0
                out[d, 1, t, :t] = 1.0
            for l in range(1, HG_LEVELS + 1):
                n = 1 << l
                mid = (t // n) * n + n // 2
                if d == 0:
                    sel = (pos >= mid) & (pos <= t) if t >= mid else (pos > t) & (pos < mid)
                else:
                    sel = (pos >= t) & (pos < mid) if t < mid else (pos >= mid) & (pos < t)
                out[d, 1 + l, t, sel] = 1.0
    return jnp.asarray(out.reshape(2, (HG_LEVELS + 2) * CH, CH), dtype=BF16)


def _hg_scan_kernel(q_ref, f_ref, i_ref, lb_ref, dm_ref, o_ref, s_ref, *, heads, bsz):
    d = pl.program_id(0)
    step = pl.program_id(1)

    @pl.when(step == 0)
    def _():
        s_ref[...] = jnp.zeros_like(s_ref)

    row = lax.broadcasted_iota(jnp.int32, (CH, CH), 0)
    col = lax.broadcasted_iota(jnp.int32, (CH, CH), 1)
    sgn = jnp.where(d == 0, 1, -1)

    lb = lb_ref[...]
    masks = _half_block_masks(row, col, sgn)
    lvl_masks = [row == col] + masks
    hs = range(bsz * heads)
    sls = [slice(h * HEAD, (h + 1) * HEAD) for h in range(heads)] * bsz
    q_lvl, k_lvl, q_in, k_out, etot, v_all = [], [], [], [], [], []
    for b in range(bsz):
        f = f_ref[b].astype(F32)
        q = q_ref[b].astype(F32)
        q = q * _sigmoid(q)
        log_sig = jnp.minimum(f, 0.0) - jnp.log(1.0 + jnp.exp(-jnp.abs(f)))
        hi_term = jnp.log(1.0 - lb) + log_sig
        lo_term = jnp.log(jnp.where(lb > 0.0, lb, 1.0))
        mx = jnp.maximum(hi_term, lo_term)
        lf_mix = mx + jnp.log(1.0 + jnp.exp(-jnp.abs(hi_term - lo_term)))
        lf = jnp.where(lb > 0.0, lf_mix, hi_term)
        key = (1.0 - lb) * _sigmoid(-f)
        ex = jnp.exp(_dot_sel(dm_ref[0], lf))
        q_in.append((q * ex[0:CH]).astype(BF16))
        k_out.append((key * ex[CH:2 * CH]).astype(BF16))
        etot.append(jnp.exp(jnp.sum(lf, axis=0, keepdims=True)))
        v_all.append(i_ref[b].astype(BF16))
        q_lvl.append([q.astype(BF16)] + [(q * ex[(1 + l) * CH:(2 + l) * CH]).astype(BF16)
                                         for l in range(1, HG_LEVELS + 1)])
        k_lvl.append([key.astype(BF16)] + [(key * ex[(1 + l) * CH:(2 + l) * CH]).astype(BF16)
                                           for l in range(1, HG_LEVELS + 1)])

    bi = [h // heads for h in hs]
    s0 = [s_ref[h] for h in hs]
    sc = [jnp.zeros((CH, CH), F32) for _ in hs]
    for l in range(HG_LEVELS + 1):
        sc = [sc[h] + jnp.where(lvl_masks[l], _dot_nt(q_lvl[bi[h]][l][:, sls[h]], k_lvl[bi[h]][l][:, sls[h]]), 0.0)
              for h in hs]
    o = [_dot(sc[h].astype(BF16), v_all[bi[h]][:, sls[h]]) + _dot_nt(q_in[bi[h]][:, sls[h]], s0[h].astype(BF16))
         for h in hs]
    s_new = [s0[h] * etot[bi[h]][:, sls[h]] + _dot_tn(v_all[bi[h]][:, sls[h]], k_out[bi[h]][:, sls[h]]) for h in hs]
    for h in hs:
        o_ref[0, bi[h], :, sls[h]] = o[h].astype(o_ref.dtype)
        s_ref[h] = s_new[h]


def _hg_scan(p_hg, lb, n_ctx, col0=0):
    bsz, t, _ = p_hg.shape
    width = lb.shape[0]
    heads = width // HEAD
    nc = t // CH
    ncc = n_ctx // CH
    dm = _hg_decay_matrices()
    chunk = lambda d, i: _scan_chunk(d, i, ncc, nc)
    blk = (bsz, CH, width)
    return pl.pallas_call(
        functools.partial(_hg_scan_kernel, heads=heads, bsz=bsz),
        grid=(2, nc),
        in_specs=[
            pl.BlockSpec(blk, lambda d, i: (0, chunk(d, i), col0)),
            pl.BlockSpec(blk, lambda d, i: (0, chunk(d, i), col0 + 1 + d)),
            pl.BlockSpec(blk, lambda d, i: (0, chunk(d, i), col0 + 3)),
            pl.BlockSpec((1, width), lambda d, i: (0, 0)),
            pl.BlockSpec((1,) + dm.shape[1:], lambda d, i: (d, 0, 0)),
        ],
        out_specs=pl.BlockSpec((1, bsz, CH, width), lambda d, i: (d, 0, chunk(d, i), 0)),
        out_shape=jax.ShapeDtypeStruct((2, bsz, t, width), BF16),
        scratch_shapes=[pltpu.VMEM((bsz * heads, HEAD, HEAD), F32)],
        compiler_params=pltpu.CompilerParams(
            dimension_semantics=("arbitrary", "arbitrary"), vmem_limit_bytes=VMEM_LIMIT),
        name="hgrn2_scan",
    )(p_hg, p_hg, p_hg, lb.reshape(1, width), dm)


def _s5_operators(lam_re, lam_im, log_dt, b_re, b_im, c_re, c_im):
    c = S5_CHUNK
    hp = lax.Precision.HIGHEST
    dt = jnp.exp(log_dt.astype(F32))[..., None]
    lam_re = lam_re.astype(F32)
    lam_im = lam_im.astype(F32)
    mag = jnp.exp(lam_re * dt)
    ang = lam_im * dt
    a_re, a_im = mag * jnp.cos(ang), mag * jnp.sin(ang)
    den = lam_re * lam_re + lam_im * lam_im
    f_re = ((a_re - 1.0) * lam_re + a_im * lam_im) / den
    f_im = (a_im * lam_re - (a_re - 1.0) * lam_im) / den
    bb_re = f_re[..., None] * b_re - f_im[..., None] * b_im
    bb_im = f_re[..., None] * b_im + f_im[..., None] * b_re
    j = jnp.arange(c + 1, dtype=F32)[None, None, :, None]
    pmag = jnp.exp((lam_re * dt)[:, :, None, :] * j)
    pang = ang[:, :, None, :] * j
    pw_re, pw_im = pmag * jnp.cos(pang), pmag * jnp.sin(pang)
    ca_re = c_re[:, :, None] * pw_re[:, :, :, None] - c_im[:, :, None] * pw_im[:, :, :, None]
    ca_im = c_re[:, :, None] * pw_im[:, :, :, None] + c_im[:, :, None] * pw_re[:, :, :, None]
    kern = (jnp.einsum('dgjcp,dgpk->dgjck', ca_re, bb_re, precision=hp)
            - jnp.einsum('dgjcp,dgpk->dgjck', ca_im, bb_im, precision=hp))
    g = kern.shape[1]
    by_lag = jnp.concatenate([kern[1][:, c - 1:0:-1], kern[0][:, 0:1] + kern[1][:, 0:1], kern[0][:, 1:c]], axis=1)
    tpos = np.arange(c)
    lag = tpos[None, :] - tpos[:, None]
    onehot = jnp.asarray((lag[None] + c - 1) == np.arange(2 * c - 1)[:, None, None], dtype=F32)
    tmat = jnp.einsum('gjoc,jab->gacbo', by_lag, onehot, precision=hp).reshape(g, c * S5_CH, c * S5_CH)

    def in_to_state(pr, pi, d):
        hr = pr[:, :, None, :] * jnp.swapaxes(bb_re[d], 1, 2)[:, None] - pi[:, :, None, :] * jnp.swapaxes(bb_im[d], 1, 2)[:, None]
        hi = pr[:, :, None, :] * jnp.swapaxes(bb_im[d], 1, 2)[:, None] + pi[:, :, None, :] * jnp.swapaxes(bb_re[d], 1, 2)[:, None]
        return hr.reshape(g, c * S5_CH, -1), hi.reshape(g, c * S5_CH, -1)

    hf_re, hf_im = in_to_state(pw_re[0, :, c - 1::-1][:, :c], pw_im[0, :, c - 1::-1][:, :c], 0)
    hb_re, hb_im = in_to_state(pw_re[1, :, :c], pw_im[1, :, :c], 1)
    hmat = jnp.concatenate([hf_re, hb_re, hf_im, hb_im], axis=-1)

    def state_to_out(cr, ci):
        return (jnp.transpose(cr, (0, 3, 1, 2)).reshape(g, -1, c * S5_CH),
                -jnp.transpose(ci, (0, 3, 1, 2)).reshape(g, -1, c * S5_CH))

    gf_re, gf_im = state_to_out(ca_re[0, :, 1:c + 1], ca_im[0, :, 1:c + 1])
    gb_re, gb_im = state_to_out(ca_re[1, :, c:0:-1], ca_im[1, :, c:0:-1])
    z = jnp.zeros_like(gf_re)
    gf = jnp.concatenate([gf_re, z, gf_im, z], axis=1)
    gb = jnp.concatenate([z, gb_re, z, gb_im], axis=1)
    ac_re = jnp.concatenate([pw_re[0, :, c], pw_re[1, :, c]], axis=-1)[:, None, :]
    ac_im = jnp.concatenate([pw_im[0, :, c], pw_im[1, :, c]], axis=-1)[:, None, :]
    return tmat.astype(BF16), hmat.astype(BF16), gf.astype(BF16), gb.astype(BF16), ac_re, ac_im


def _s5_local_kernel(u_ref, h_ref, loc_ref):
    loc_ref[0] = _dot(u_ref[0].astype(BF16), h_ref[0])


def _s5_state_kernel(loc_ref, ar_ref, ai_ref, sf_ref, sb_ref, *, nc, ncc, bsz):
    ar = ar_ref[0]
    ai = ai_ref[0]
    half = ar.shape[-1]
    is_fwd = lax.broadcasted_iota(jnp.int32, (1, half), 1) < half // 2

    def body(i, carry):
        nb = jnp.where(i < ncc, ncc - 1 - i, nc + ncc - 1 - i)
        new = []
        for b in range(bsz):
            re, im = carry[b]
            st = jnp.concatenate([re, im], axis=1)
            sf_ref[0, b, pl.ds(i, 1), :] = st
            sb_ref[0, b, pl.ds(nb, 1), :] = st
            lf = loc_ref[0, b, pl.ds(i, 1), :]
            lk = loc_ref[0, b, pl.ds(nb, 1), :]
            l_re = jnp.where(is_fwd, lf[:, :half], lk[:, :half])
            l_im = jnp.where(is_fwd, lf[:, half:], lk[:, half:])
            new.append((ar * re - ai * im + l_re, ar * im + ai * re + l_im))
        return tuple(new)

    zero = jnp.zeros((1, half), F32)
    lax.fori_loop(0, nc, body, tuple((zero, zero) for _ in range(bsz)))


def _s5_out_kernel(u_ref, t_ref, sf_ref, gf_ref, sb_ref, gb_ref, d_ref, y_ref):
    u = u_ref[0]
    y = (_dot(u.astype(BF16), t_ref[0]) + _dot(sf_ref[0].astype(BF16), gf_ref[0])
         + _dot(sb_ref[0].astype(BF16), gb_ref[0]) + d_ref[0] * u.astype(F32))
    gelu = 0.5 * y * (1.0 + jnp.tanh(math.sqrt(2.0 / math.pi) * (y + 0.044715 * (y * y * y))))
    y_ref[0] = gelu.astype(y_ref.dtype)


def _s5_scan(p_s5, n_ctx, lam_re, lam_im, log_dt, b_re, b_im, c_re, c_im, d_skip):
    bsz, t, width = p_s5.shape
    g = width // S5_CH
    c = S5_CHUNK
    nc, ncc = t // c, n_ctx // c
    blk = c * S5_CH
    rows = bsz * nc
    tmat, hmat, gf, gb, ac_re, ac_im = _s5_operators(lam_re, lam_im, log_dt, b_re, b_im, c_re, c_im)
    u = jnp.transpose(p_s5.astype(F32).reshape(bsz, nc, c, g, S5_CH), (3, 0, 1, 2, 4)).reshape(g, rows, blk)
    per_g = lambda gi: (gi, 0, 0)
    cparams = pltpu.CompilerParams(dimension_semantics=("arbitrary",), vmem_limit_bytes=VMEM_LIMIT)
    loc = pl.pallas_call(
        _s5_local_kernel, grid=(g,),
        in_specs=[pl.BlockSpec((1, rows, blk), per_g), pl.BlockSpec((1, blk, 256), per_g)],
        out_specs=pl.BlockSpec((1, rows, 256), per_g),
        out_shape=jax.ShapeDtypeStruct((g, rows, 256), F32),
        compiler_params=cparams, name="s5_local",
    )(u, hmat)
    per_g4 = lambda gi: (gi, 0, 0, 0)
    st_shape = jax.ShapeDtypeStruct((g, bsz, nc, 256), F32)
    sf, sb = pl.pallas_call(
        functools.partial(_s5_state_kernel, nc=nc, ncc=ncc, bsz=bsz), grid=(g,),
        in_specs=[pl.BlockSpec((1, bsz, nc, 256), per_g4), pl.BlockSpec((1, 1, 128), per_g),
                  pl.BlockSpec((1, 1, 128), per_g)],
        out_specs=[pl.BlockSpec((1, bsz, nc, 256), per_g4)] * 2,
        out_shape=[st_shape, st_shape],
        compiler_params=cparams, name="s5_state",
    )(loc.reshape(g, bsz, nc, 256), ac_re, ac_im)
    d_flat = jnp.tile(d_skip.astype(F32).reshape(g, 1, S5_CH), (1, c, 1)).reshape(g, 1, blk)
    y = pl.pallas_call(
        _s5_out_kernel, grid=(g,),
        in_specs=[pl.BlockSpec((1, rows, blk), per_g), pl.BlockSpec((1, blk, blk), per_g),
                  pl.BlockSpec((1, rows, 256), per_g), pl.BlockSpec((1, 256, blk), per_g),
                  pl.BlockSpec((1, rows, 256), per_g), pl.BlockSpec((1, 256, blk), per_g),
                  pl.BlockSpec((1, 1, blk), per_g)],
        out_specs=pl.BlockSpec((1, rows, blk), per_g),
        out_shape=jax.ShapeDtypeStruct((g, rows, blk), F32),
        compiler_params=cparams, name="s5_out",
    )(u, tmat, sf.reshape(g, rows, 256), gf, sb.reshape(g, rows, 256), gb, d_flat)
    return jnp.transpose(y.reshape(g, bsz, nc, c, S5_CH), (1, 2, 3, 0, 4)).reshape(bsz, t, width)


IN_BM = 1280
IN_BN = 768
OUT_BM = 256
ROUTER_PAD = 128


def _row_block(t, target):
    bm = target
    while t % bm:
        bm -= CH
    return bm


def _norm_mod(x, mod_ref, first_row, n_ctx):
    xn = x * lax.rsqrt(jnp.mean(x * x, axis=-1, keepdims=True) + NORM_EPS)
    is_ctx = (first_row + lax.broadcasted_iota(jnp.int32, (x.shape[0], 1), 0)) < n_ctx
    scale = jnp.where(is_ctx, mod_ref[0, 0, 0:1, :], mod_ref[0, 1, 0:1, :])
    shift = jnp.where(is_ctx, mod_ref[0, 0, 1:2, :], mod_ref[0, 1, 1:2, :])
    return xn * scale + shift


def _in_proj_kernel(x_ref, mod_ref, w_ref, p_ref, h_ref, *, n_ctx, bm):
    @pl.when(pl.program_id(2) == 0)
    def _():
        sub = _row_block(bm, 256)

        def body(k, carry):
            rows = pl.ds(pl.multiple_of(k * sub, sub), sub)
            h = _norm_mod(x_ref[0, rows, :], mod_ref, pl.program_id(1) * bm + k * sub, n_ctx)
            h_ref[rows, :] = h.astype(BF16)
            return carry

        lax.fori_loop(0, bm // sub, body, 0)

    p_ref[0] = _dot(h_ref[...], w_ref[...]).astype(p_ref.dtype)


def _in_proj(tok, mod, w, n_ctx):
    bsz, t, d = tok.shape
    n = w.shape[1]
    bm = _row_block(t, IN_BM)
    return pl.pallas_call(
        functools.partial(_in_proj_kernel, n_ctx=n_ctx, bm=bm),
        grid=(bsz, t // bm, n // IN_BN),
        in_specs=[pl.BlockSpec((1, bm, d), lambda b, i, j: (b, i, 0), pipeline_mode=pl.Buffered(1)),
                  pl.BlockSpec((1, 2, 2, d), lambda b, i, j: (b, 0, 0, 0)),
                  pl.BlockSpec((d, IN_BN), lambda b, i, j: (0, j))],
        out_specs=pl.BlockSpec((1, bm, IN_BN), lambda b, i, j: (b, i, j)),
        out_shape=jax.ShapeDtypeStruct((bsz, t, n), F32),
        scratch_shapes=[pltpu.VMEM((bm, d), BF16)],
        compiler_params=pltpu.CompilerParams(
            dimension_semantics=("arbitrary", "arbitrary", "arbitrary"), vmem_limit_bytes=VMEM_LIMIT),
        name="in_proj",
    )(tok, mod, w)


def _out_proj_kernel(s5_ref, wglu_ref, bglu_ref,
                     y2_ref, r_ref, kd_ref, v_ref, g_ref, rk_ref, lnw_ref, lnb_ref,
                     o2_ref, og_ref, hgn_ref, bd_ref,
                     x_ref, gate_ref, wout_ref, modf_ref, wr_ref, br_ref,
                     xo_ref, h_ref, lg_ref, *, n_ctx, bm):
    first = pl.program_id(1) * bm
    bd = bd_ref[...]
    inv_n = 1.0 / HEAD

    def head_sum(z):
        return _dot(z.astype(BF16), bd)

    ys = s5_ref[0].astype(F32)
    ys = ys * _sigmoid(_dot(ys.astype(BF16), wglu_ref[...]) + bglu_ref[...])
    y = y2_ref[0, 0].astype(F32) + y2_ref[1, 0].astype(F32)
    yc = y - head_sum(y) * inv_n
    var = head_sum(yc * yc) * inv_n
    k_bonus = 0.5 * (kd_ref[0, 0].astype(F32) + kd_ref[1, 0].astype(F32))
    bonus = head_sum(r_ref[0].astype(F32) * k_bonus * rk_ref[...]) * v_ref[0].astype(F32)
    yr = (yc * lax.rsqrt(var + RW_GN_EPS) * lnw_ref[...] + lnb_ref[...] + bonus) * g_ref[0].astype(F32)
    o = o2_ref[0, 0].astype(F32) + o2_ref[1, 0].astype(F32)
    yh = o * lax.rsqrt(head_sum(o * o) * inv_n + NORM_EPS) * hgn_ref[...] * _sigmoid(og_ref[0].astype(F32))

    m = _dot(jnp.concatenate([ys, yr, yh], axis=-1).astype(BF16), wout_ref[...])
    is_ctx = (first + lax.broadcasted_iota(jnp.int32, (bm, 1), 0)) < n_ctx
    x_new = x_ref[0] + jnp.where(is_ctx, gate_ref[0, 0:1, :], gate_ref[0, 1:2, :]) * m
    xo_ref[0] = x_new
    h = _norm_mod(x_new, modf_ref, first, n_ctx).astype(BF16)
    h_ref[0] = h
    lg_ref[0] = _dot(h, wr_ref[...]) + br_ref[...]


def _out_proj(s5y, w_glu, b_glu, y2, r, kd, v, g, r_k, ln_w, ln_b, o2, p, og_col, hg_norm,
              tok, gate, w_out, mod_f, w_router, b_router, n_ctx):
    bsz, t, d = tok.shape
    bm = _row_block(t, OUT_BM)
    s5w = s5y.shape[-1]
    w = r.shape[-1]
    row = lambda b, i: (b, i, 0)
    row2 = lambda b, i: (0, b, i, 0)
    c2 = lambda b, i: (0, 0)
    vec = lambda z: z.reshape(1, -1).astype(F32)
    spec_w = pl.BlockSpec((1, bm, w), row)
    spec_2w = pl.BlockSpec((2, 1, bm, w), row2)
    spec_vw = pl.BlockSpec((1, w), c2)
    return pl.pallas_call(
        functools.partial(_out_proj_kernel, n_ctx=n_ctx, bm=bm),
        grid=(bsz, t // bm),
        in_specs=[
            pl.BlockSpec((1, bm, s5w), row), pl.BlockSpec((s5w, s5w), c2), pl.BlockSpec((1, s5w), c2),
            spec_2w, spec_w, spec_2w, spec_w, spec_w, spec_vw, spec_vw, spec_vw,
            spec_2w, pl.BlockSpec((1, bm, w), lambda b, i: (b, i, og_col)), spec_vw, pl.BlockSpec((w, w), c2),
            pl.BlockSpec((1, bm, d), row), pl.BlockSpec((1, 2, d), lambda b, i: (b, 0, 0)),
            pl.BlockSpec(w_out.shape, c2), pl.BlockSpec((1, 2, 2, d), lambda b, i: (b, 0, 0, 0)),
            pl.BlockSpec((d, ROUTER_PAD), c2), pl.BlockSpec((1, ROUTER_PAD), c2),
        ],
        out_specs=[pl.BlockSpec((1, bm, d), row), pl.BlockSpec((1, bm, d), row),
                   pl.BlockSpec((1, bm, ROUTER_PAD), row)],
        out_shape=[jax.ShapeDtypeStruct((bsz, t, d), F32), jax.ShapeDtypeStruct((bsz, t, d), BF16),
                   jax.ShapeDtypeStruct((bsz, t, ROUTER_PAD), F32)],
        compiler_params=pltpu.CompilerParams(
            dimension_semantics=("arbitrary", "arbitrary"), vmem_limit_bytes=VMEM_LIMIT),
        name="out_proj",
    )(s5y, w_glu.astype(BF16), vec(b_glu), y2, r, kd, v, g, vec(r_k), vec(ln_w), vec(ln_b),
      o2, p, vec(hg_norm), _head_block_diag(w), tok, gate, w_out.astype(BF16), mod_f,
      w_router.astype(BF16), b_router)


def _final_norm_kernel(x_ref, g_ref, o_ref):
    x = x_ref[0]
    o_ref[0] = x * lax.rsqrt(jnp.mean(x * x, axis=-1, keepdims=True) + NORM_EPS) * g_ref[...]


def _final_norm(x, g):
    bsz, t, d = x.shape
    bm = _row_block(t, 512)
    return pl.pallas_call(
        _final_norm_kernel, grid=(bsz, t // bm),
        in_specs=[pl.BlockSpec((1, bm, d), lambda b, i: (b, i, 0)), pl.BlockSpec((1, d), lambda b, i: (0, 0))],
        out_specs=pl.BlockSpec((1, bm, d), lambda b, i: (b, i, 0)),
        out_shape=jax.ShapeDtypeStruct((bsz, t, d), F32),
        compiler_params=pltpu.CompilerParams(dimension_semantics=("arbitrary", "arbitrary")),
        name="final_norm",
    )(x, g.reshape(1, d).astype(F32))


def _moe_kernel(meta_ref, x_ref, w1_ref, w3_ref, w2_ref, y_ref, *, n_blocks):
    active = pl.program_id(0) < meta_ref[n_blocks]

    @pl.when(active)
    def _():
        x = x_ref[...]
        a = _dot(x, w1_ref[0])
        b = _dot(x, w3_ref[0])
        y_ref[...] = _dot((a * _sigmoid(a) * b).astype(BF16), w2_ref[0]).astype(y_ref.dtype)

    @pl.when(jnp.logical_not(active))
    def _():
        y_ref[...] = jnp.zeros_like(y_ref)


def _cast_kernel(x_ref, o_ref):
    o_ref[0] = x_ref[0, 0].astype(o_ref.dtype)


def _layer_to_bf16(w, layer):
    _, e, r, c = w.shape
    return pl.pallas_call(
        _cast_kernel, grid=(e,),
        in_specs=[pl.BlockSpec((1, 1, r, c), lambda i: (layer, i, 0, 0))],
        out_specs=pl.BlockSpec((1, r, c), lambda i: (i, 0, 0)),
        out_shape=jax.ShapeDtypeStruct((e, r, c), BF16),
        compiler_params=pltpu.CompilerParams(dimension_semantics=("arbitrary",), vmem_limit_bytes=VMEM_LIMIT),
        name="cast_bf16",
    )(w)


def _moe_experts(xg, meta, w1, w3, w2):
    n_rows, d = xg.shape
    n_blocks = n_rows // MOE_BLOCK
    de = w1.shape[2]
    return pl.pallas_call(
        functools.partial(_moe_kernel, n_blocks=n_blocks),
        grid_spec=pltpu.PrefetchScalarGridSpec(
            num_scalar_prefetch=1, grid=(n_blocks,),
            in_specs=[pl.BlockSpec((MOE_BLOCK, d), lambda i, m: (i, 0)),
                      pl.BlockSpec((1, d, de), lambda i, m: (m[i], 0, 0)),
                      pl.BlockSpec((1, d, de), lambda i, m: (m[i], 0, 0)),
                      pl.BlockSpec((1, de, d), lambda i, m: (m[i], 0, 0))],
            out_specs=pl.BlockSpec((MOE_BLOCK, d), lambda i, m: (i, 0))),
        out_shape=jax.ShapeDtypeStruct((n_rows, d), BF16),
        compiler_params=pltpu.CompilerParams(dimension_semantics=("arbitrary",), vmem_limit_bytes=VMEM_LIMIT),
        name="moe_experts",
    )(meta, xg, w1, w3, w2)


def _moe_ffn(h, logits, n_groups, n_experts, w1, w3, w2):
    n, d = h.shape
    epg = n_experts // n_groups
    group_logits = logits[:, :n_groups]
    g_sel = jnp.argmax(group_logits, axis=-1)
    g_gate = jnp.take_along_axis(jax.nn.softmax(group_logits, axis=-1), g_sel[:, None], axis=1)
    exp_logits = logits[:, n_groups:n_groups + n_experts].reshape(n, n_groups, epg)
    within = jnp.take_along_axis(exp_logits, g_sel[:, None, None], axis=1)[:, 0]
    top_val, top_idx = lax.top_k(within, TOP_K)
    gate = jax.nn.softmax(top_val, axis=-1) * g_gate
    e_flat = (g_sel[:, None] * epg + top_idx).reshape(-1).astype(jnp.int32)
    n_assign = n * TOP_K
    order = jnp.argsort(e_flat)
    rank = jnp.argsort(order).astype(jnp.int32)
    counts = jnp.sum(e_flat[:, None] == jnp.arange(n_experts, dtype=jnp.int32)[None, :], axis=0, dtype=jnp.int32)
    starts = jnp.cumsum(counts) - counts
    padded = (counts + MOE_BLOCK - 1) // MOE_BLOCK * MOE_BLOCK
    pad_ends = jnp.cumsum(padded)
    pad_starts = pad_ends - padded
    n_blocks = -(-n_assign // MOE_BLOCK) + n_experts
    n_rows = n_blocks * MOE_BLOCK
    block_start = jnp.arange(n_blocks, dtype=jnp.int32) * MOE_BLOCK
    block_exp = jnp.minimum(jnp.sum(pad_ends[None, :] <= block_start[:, None], axis=1, dtype=jnp.int32), n_experts - 1)
    rows = jnp.arange(n_rows, dtype=jnp.int32)
    row_exp = jnp.repeat(block_exp, MOE_BLOCK)
    offset = rows - pad_starts[row_exp]
    src = jnp.clip(starts[row_exp] + offset, 0, n_assign - 1)
    row_tok = jnp.where(offset < counts[row_exp], order[src].astype(jnp.int32) // TOP_K, 0)
    meta = jnp.concatenate([block_exp, (pad_ends[-1:] // MOE_BLOCK).astype(jnp.int32)])
    y_rows = _moe_experts(h[row_tok], meta, w1, w3, w2)
    pos = (pad_starts[e_flat] + rank - starts[e_flat]).reshape(n, TOP_K)
    return gate[:, 0:1] * y_rows[pos[:, 0]] + gate[:, 1:2] * y_rows[pos[:, 1]]


def kernel(x, c, ctx, c_ctx, ada_w, ada_b, norm_mix, norm_ffn, w_in, w_out, s5_lam_re, s5_lam_im, s5_log_dt, s5_b_re, s5_b_im, s5_c_re, s5_c_im, s5_d, s5_w_glu, s5_b_glu, rw_mu, rw_w0, rw_w2, rw_a0, rw_a2, rw_g2, rw_k_k, rw_k_a, rw_r_k, rw_ln_w, rw_ln_b, hg_lb_logits, hg_norm, moe_w_group, moe_b_group, moe_w_expert, moe_b_expert, moe_w1, moe_w3, moe_w2, final_norm):
    bsz, n_lat, d = x.shape
    n_ctx = ctx.shape[1]
    depth = w_in.shape[0]
    s5_width = s5_d.shape[1]
    rw_width = rw_k_k.shape[1]
    rw_in = rw_mu.shape[2]
    hg_width = hg_norm.shape[1]
    n_groups = moe_w_group.shape[2]
    n_experts = moe_w_expert.shape[2]
    t = n_ctx + n_lat
    hg_col0 = -(-rw_in // hg_width)
    s5_off = (hg_col0 + 5) * hg_width
    n_proj = -(-(s5_off + s5_width) // IN_BN) * IN_BN
    pad1 = hg_col0 * hg_width - rw_in
    pad2 = n_proj - s5_off - s5_width

    silu_c = jax.nn.silu(c)
    silu_cc = jax.nn.silu(c_ctx)
    lb_all = jnp.cumsum(jax.nn.softmax(hg_lb_logits.astype(F32), axis=0), axis=0)
    lb_all = lb_all - lb_all[0:1]
    tok = jnp.concatenate([ctx, x], axis=1)
    for l in range(depth):
        last = l == depth - 1
        mod_lat = jnp.split(silu_c @ ada_w[l] + ada_b[l], 6, axis=-1)
        mod_ctx = [jnp.broadcast_to(z, (bsz, d)) for z in jnp.split(silu_cc @ ada_w[l] + ada_b[l], 6, axis=-1)]
        sh_m, sc_m, gt_m, sh_f, sc_f, gt_f = [jnp.stack([zc, zl], axis=1) for zc, zl in zip(mod_ctx, mod_lat)]
        mod_m = jnp.stack([norm_mix[l] * (1.0 + sc_m), sh_m], axis=2)
        mod_f = jnp.stack([norm_ffn[l] * (1.0 + sc_f), sh_f], axis=2)

        w_cols = [w_in[l][:, s5_width:s5_width + rw_in], jnp.zeros((d, pad1), F32),
                  w_in[l][:, s5_width + rw_in:], w_in[l][:, :s5_width], jnp.zeros((d, pad2), F32)]
        p = _in_proj(tok, mod_m, jnp.concatenate(w_cols, axis=1).astype(BF16), n_ctx)

        r, v, kk, g, lw, kd, bb = _rw_prep(p, n_ctx, rw_mu[l], rw_w0[l], rw_w2[l], rw_a0[l], rw_a2[l],
                                           rw_g2[l], rw_k_k[l], rw_k_a[l])
        y2 = _rw_scan(r, lw, kd, v, kk, bb, n_ctx)
        o2 = _hg_scan(p, lb_all[l], n_ctx, col0=hg_col0)
        s5y = _s5_scan(p[:, :, s5_off:s5_off + s5_width], n_ctx, s5_lam_re[l], s5_lam_im[l], s5_log_dt[l],
                       s5_b_re[l], s5_b_im[l], s5_c_re[l], s5_c_im[l], s5_d[l])

        w_router = jnp.concatenate([moe_w_group[l], moe_w_expert[l],
                                    jnp.zeros((d, ROUTER_PAD - n_groups - n_experts), F32)], axis=1)
        b_router = jnp.concatenate([moe_b_group[l], moe_b_expert[l],
                                    jnp.zeros((ROUTER_PAD - n_groups - n_experts,), F32)]).reshape(1, ROUTER_PAD)
        tok, h_ffn, logits = _out_proj(
            s5y, s5_w_glu[l], s5_b_glu[l], y2, r, kd, v, g, rw_r_k[l], rw_ln_w[l], rw_ln_b[l],
            o2, p, hg_col0 + 4, hg_norm[l], tok, gt_m, w_out[l], mod_f, w_router, b_router, n_ctx)

        experts = (_layer_to_bf16(moe_w1, l), _layer_to_bf16(moe_w3, l), _layer_to_bf16(moe_w2, l))
        if last:
            f = _moe_ffn(h_ffn[:, n_ctx:].reshape(bsz * n_lat, d), logits[:, n_ctx:].reshape(bsz * n_lat, ROUTER_PAD),
                         n_groups, n_experts, *experts).reshape(bsz, n_lat, d)
            x = tok[:, n_ctx:] + gt_f[:, 1:2] * f
        else:
            f = _moe_ffn(h_ffn.reshape(bsz * t, d), logits.reshape(bsz * t, ROUTER_PAD),
                         n_groups, n_experts, *experts).reshape(bsz, t, d)
            tok = jnp.concatenate([tok[:, :n_ctx] + gt_f[:, 0:1] * f[:, :n_ctx],
                                   tok[:, n_ctx:] + gt_f[:, 1:2] * f[:, n_ctx:]], axis=1)
    return _final_norm(x, final_norm)
```

```python
import functools
import math

import numpy as np
import jax
import jax.numpy as jnp
from jax import lax
from jax.experimental import pallas as pl
from jax.experimental.pallas import tpu as pltpu

F32 = jnp.float32
BF16 = jnp.bfloat16

GRID_W = 64
CH = GRID_W
HEAD = 64
S5_CH = 16
S5_STATE = 64
S5_CHUNK = 32
RW_LORA = 64
RW_GN_EPS = 64e-5
NORM_EPS = 1e-6
TOP_K = 2
MOE_BLOCK = 256
VMEM_LIMIT = 48 * 1024 * 1024


def _dot(a, b):
    return jnp.dot(a, b, preferred_element_type=F32)


def _dot_nt(a, b):
    return lax.dot_general(a, b, (((1,), (1,)), ((), ())), preferred_element_type=F32)


def _dot_tn(a, b):
    return lax.dot_general(a, b, (((0,), (0,)), ((), ())), preferred_element_type=F32)


def _split3(x):
    hi = x.astype(BF16)
    r1 = x - hi.astype(F32)
    mid = r1.astype(BF16)
    lo = (r1 - mid.astype(F32)).astype(BF16)
    return hi, mid, lo


def _dot_sel(m01, x):
    hi, mid, lo = _split3(x)
    return _dot(m01, hi) + _dot(m01, mid) + _dot(m01, lo)


def _dot_hp(a, b):
    a_hi = a.astype(BF16)
    a_lo = (a - a_hi.astype(F32)).astype(BF16)
    b_hi = b.astype(BF16)
    b_lo = (b - b_hi.astype(F32)).astype(BF16)
    return _dot(a_hi, b_hi) + _dot(a_hi, b_lo) + _dot(a_lo, b_hi)


def _sigmoid(x):
    return 1.0 / (1.0 + jnp.exp(-x))


def _scan_chunk(d, i, ncc, nc):
    back = jnp.where(i < ncc, ncc - 1 - i, nc + ncc - 1 - i)
    return jnp.where(d == 0, i, back)


def _half_block_masks(row, col, sgn):
    masks = []
    n = 2
    while n <= CH:
        same = (row // n) == (col // n)
        t_hi = ((row % n) >= n // 2).astype(jnp.int32)
        s_hi = ((col % n) >= n // 2).astype(jnp.int32)
        masks.append(jnp.logical_and(same, (t_hi - s_hi) * sgn == 1))
        n *= 2
    return masks


def _head_block_diag(width):
    idx = np.arange(width) // HEAD
    return jnp.asarray(idx[:, None] == idx[None, :], dtype=BF16)


def _rw_prep_kernel(cur_ref, prev_ref, next_ref, mu_ref, w0_ref, w2_ref, a0_ref, a2_ref, g2_ref,
                    kkw_ref, ka_ref, bd_ref,
                    r_out, v_out, kk_out, g_out, lw_out, kd_out, bb_out, *, ncc, nc, width):
    n = pl.program_id(1)
    cur = cur_ref[0].astype(F32)
    prv = prev_ref[0].astype(F32)
    nxt = next_ref[0].astype(F32)
    is_ctx = n < ncc
    row = lax.broadcasted_iota(jnp.int32, (CH, 1), 0)
    zero_row = jnp.zeros((1, cur.shape[1]), F32)
    first_fill = jnp.where(jnp.logical_and(is_ctx, n > 0), prv[CH - 1:CH, :], zero_row)
    last_fill = jnp.where(jnp.logical_and(is_ctx, n < ncc - 1), nxt[0:1, :], zero_row)
    nb_prev = jnp.where(row == 0, first_fill, pltpu.roll(cur, 1, 0))
    nb_next = jnp.where(row == CH - 1, last_fill, pltpu.roll(cur, CH - 1, 0))
    mu = mu_ref[...]
    zero_mu = jnp.zeros_like(mu[0:1])
    w_up = jnp.where(jnp.logical_and(jnp.logical_not(is_ctx), n > ncc), mu[2:3], zero_mu)
    w_down = jnp.where(jnp.logical_and(jnp.logical_not(is_ctx), n < nc - 1), mu[3:4], zero_mu)
    w_cur = 1.0 - mu[0:1] - mu[1:2] - jnp.where(is_ctx, zero_mu, mu[2:3] + mu[3:4])
    x = w_cur * cur + mu[0:1] * nb_prev + mu[1:2] * nb_next + w_up * prv + w_down * nxt

    w = width
    r = x[:, 0:w]
    k = x[:, w:2 * w]
    v = x[:, 2 * w:3 * w]
    wd = x[:, 3 * w:3 * w + 2 * RW_LORA]
    ad = x[:, 3 * w + 2 * RW_LORA:3 * w + 4 * RW_LORA]
    gd = x[:, 3 * w + 4 * RW_LORA:]

    r_out[0] = r.astype(r_out.dtype)
    v_out[0] = v.astype(v_out.dtype)
    g_out[0] = _dot_hp(_sigmoid(gd), g2_ref[...]).astype(g_out.dtype)
    kk = k * kkw_ref[...]
    kk2 = kk * kk
    kk2_hi = kk2.astype(BF16)
    kk2_lo = (kk2 - kk2_hi.astype(F32)).astype(BF16)
    ss = _dot(kk2_hi, bd_ref[...]) + _dot(kk2_lo, bd_ref[...])
    kk = kk / jnp.maximum(jnp.sqrt(ss), 1e-12)
    kk_out[0] = kk.astype(kk_out.dtype)
    ka = ka_ref[...]
    for d in range(2):
        w_raw = w0_ref[d:d + 1, :] + _dot_hp(jnp.tanh(wd[:, d * RW_LORA:(d + 1) * RW_LORA]), w2_ref[d])
        nw = -w_raw
        softplus = jnp.maximum(nw, 0.0) + jnp.log(1.0 + jnp.exp(-jnp.abs(nw)))
        lw_out[d, 0] = -jnp.exp(-softplus - 0.5)
        a = _sigmoid(a0_ref[d:d + 1, :] + _dot_hp(ad[:, d * RW_LORA:(d + 1) * RW_LORA], a2_ref[d]))
        kd_out[d, 0] = (k * (1.0 + (a - 1.0) * ka)).astype(kd_out.dtype)
        bb_out[d, 0] = (kk * a).astype(bb_out.dtype)


def _rw_prep(p_rw, n_ctx, mu, w0, w2, a0, a2, g2, k_k, k_a):
    bsz, t, _ = p_rw.shape
    win = mu.shape[1]
    width = k_k.shape[0]
    nc = t // CH
    ncc = n_ctx // CH
    tok = lambda b, n: (b, n, 0)
    full2 = lambda b, n: (0, 0)
    full3 = lambda b, n: (0, 0, 0)
    blk_in = (1, CH, win)
    out1 = jax.ShapeDtypeStruct((bsz, t, width), BF16)
    out2 = jax.ShapeDtypeStruct((2, bsz, t, width), BF16)
    out_lw = jax.ShapeDtypeStruct((2, bsz, t, width), F32)
    spec1 = pl.BlockSpec((1, CH, width), tok)
    spec2 = pl.BlockSpec((2, 1, CH, width), lambda b, n: (0, b, n, 0))
    return pl.pallas_call(
        functools.partial(_rw_prep_kernel, ncc=ncc, nc=nc, width=width),
        grid=(bsz, nc),
        in_specs=[
            pl.BlockSpec(blk_in, tok),
            pl.BlockSpec(blk_in, lambda b, n: (b, jnp.maximum(n - 1, 0), 0)),
            pl.BlockSpec(blk_in, lambda b, n: (b, jnp.minimum(n + 1, nc - 1), 0)),
            pl.BlockSpec(mu.shape, full2),
            pl.BlockSpec(w0.shape, full2),
            pl.BlockSpec(w2.shape, full3),
            pl.BlockSpec(a0.shape, full2),
            pl.BlockSpec(a2.shape, full3),
            pl.BlockSpec(g2.shape, full2),
            pl.BlockSpec((1, width), full2),
            pl.BlockSpec((1, width), full2),
            pl.BlockSpec((width, width), full2),
        ],
        out_specs=[spec1, spec1, spec1, spec1, spec2, spec2, spec2],
        out_shape=[out1, out1, out1, out1, out_lw, out2, out2],
        compiler_params=pltpu.CompilerParams(
            dimension_semantics=("arbitrary", "arbitrary"), vmem_limit_bytes=VMEM_LIMIT),
        name="rwkv_prep",
    )(p_rw, p_rw, p_rw, mu, w0, w2, a0, a2, g2, k_k.reshape(1, width), k_a.reshape(1, width),
      _head_block_diag(width))


def _rw_scan_kernel(r_ref, lw_ref, kd_ref, v_ref, kk_ref, bb_ref, y_ref, s_ref, *, heads, bsz):
    d = pl.program_id(0)
    i = pl.program_id(1)

    @pl.when(i == 0)
    def _():
        s_ref[...] = jnp.zeros_like(s_ref)

    row = lax.broadcasted_iota(jnp.int32, (CH, CH), 0)
    col = lax.broadcasted_iota(jnp.int32, (CH, CH), 1)
    sgn = jnp.where(d == 0, 1, -1)
    strict = (row - col) * sgn > 0
    diag = row == col
    incl = jnp.logical_or(strict, diag)
    tri = jnp.where(incl, 1.0, 0.0).astype(BF16)
    eye = jnp.where(diag, 1.0, 0.0)
    halves = _half_block_masks(row, col, sgn)

    hs = range(bsz * heads)
    sls = [slice(h * HEAD, (h + 1) * HEAD) for h in range(heads)] * bsz
    s0 = [s_ref[h] for h in hs]
    v, kkr, bk, b_w, k_w, wtot = [], [], [], [], [], []
    for b in range(bsz):
        lw = lw_ref[0, b]
        cw = _dot_sel(tri, lw)
        ctot = jnp.sum(lw, axis=0, keepdims=True)
        e_ncw = jnp.exp(-cw)
        e_rem = jnp.exp(ctot - cw)
        r_t = (r_ref[b].astype(F32) * jnp.exp(cw)).astype(BF16)
        kk_t = (kk_ref[b].astype(F32) * jnp.exp(cw - lw)).astype(BF16)
        bb = bb_ref[0, b].astype(F32)
        kd = kd_ref[0, b].astype(F32)
        b_t = (bb * e_ncw).astype(BF16)
        k_t = (kd * e_ncw).astype(BF16)
        bw_all = (bb * e_rem).astype(BF16)
        kw_all = (kd * e_rem).astype(BF16)
        wt_all = jnp.exp(ctot)
        v_all = v_ref[b]
        for sl in sls[:heads]:
            v.append(v_all[:, sl])
            kkr.append(jnp.concatenate([kk_t[:, sl], r_t[:, sl]], axis=0))
            bk.append(jnp.concatenate([b_t[:, sl], k_t[:, sl]], axis=0))
            b_w.append(bw_all[:, sl])
            k_w.append(kw_all[:, sl])
            wtot.append(wt_all[:, sl])
    att = [_dot_nt(kkr[h], bk[h]) for h in hs]
    proj = [_dot_nt(kkr[h], s0[h].astype(BF16)) for h in hs]
    a_ab = [jnp.where(strict, att[h][:CH, :CH], 0.0) for h in hs]
    a_ak = [jnp.where(strict, att[h][:CH, CH:], 0.0).astype(BF16) for h in hs]
    a_rb = [jnp.where(incl, att[h][CH:, :CH], 0.0).astype(BF16) for h in hs]
    a_rk = [jnp.where(incl, att[h][CH:, CH:], 0.0).astype(BF16) for h in hs]
    inv = [eye - jnp.where(halves[0], a_ab[h], 0.0) for h in hs]
    for lvl in range(1, len(halves)):
        inv_b = [inv[h].astype(BF16) for h in hs]
        a_l = [jnp.where(halves[lvl], a_ab[h], 0.0).astype(BF16) for h in hs]
        left = [_dot(inv_b[h], a_l[h]).astype(BF16) for h in hs]
        inv = [inv[h] - _dot(left[h], inv_b[h]) for h in hs]
    rhs = [(proj[h][:CH] + _dot(a_ak[h], v[h])).astype(BF16) for h in hs]
    u_b = [(-_dot(inv[h].astype(BF16), rhs[h])).astype(BF16) for h in hs]
    y = [proj[h][CH:] + _dot(a_rb[h], u_b[h]) + _dot(a_rk[h], v[h]) for h in hs]
    s_new = [s0[h] * wtot[h] + _dot_tn(u_b[h], b_w[h]) + _dot_tn(v[h], k_w[h]) for h in hs]
    for h in hs:
        y_ref[0, h // heads, :, sls[h]] = y[h].astype(y_ref.dtype)
        s_ref[h] = s_new[h]


def _rw_scan(r, lw, kd, v, kk, bb, n_ctx):
    bsz, t, width = r.shape
    heads = width // HEAD
    nc = t // CH
    ncc = n_ctx // CH
    s1 = pl.BlockSpec((bsz, CH, width), lambda d, i: (0, _scan_chunk(d, i, ncc, nc), 0))
    s2 = pl.BlockSpec((1, bsz, CH, width), lambda d, i: (d, 0, _scan_chunk(d, i, ncc, nc), 0))
    return pl.pallas_call(
        functools.partial(_rw_scan_kernel, heads=heads, bsz=bsz),
        grid=(2, nc),
        in_specs=[s1, s2, s2, s1, s1, s2],
        out_specs=s2,
        out_shape=jax.ShapeDtypeStruct((2, bsz, t, width), BF16),
        scratch_shapes=[pltpu.VMEM((bsz * heads, HEAD, HEAD), F32)],
        compiler_params=pltpu.CompilerParams(
            dimension_semantics=("arbitrary", "arbitrary"), vmem_limit_bytes=VMEM_LIMIT),
        name="rwkv_scan",
    )(r, lw, kd, v, kk, bb)


HG_LEVELS = 6


def _hg_decay_matrices():
    out = np.zeros((2, HG_LEVELS + 2, CH, CH), np.float32)
    pos = np.arange(CH)
    for d in range(2):
        for t in range(CH):
            if d == 0:
                out[d, 0, t, :t + 1] = 1.0
                out[d, 1, t, t + 1:] = 1.0
            else:
                out[d, 0, t, t:] = 1.0
                out[d, 1, t, :t] = 1.0
            for l in range(1, HG_LEVELS + 1):
                n = 1 << l
                mid = (t // n) * n + n // 2
                if d == 0:
                    sel = (pos >= mid) & (pos <= t) if t >= mid else (pos > t) & (pos < mid)
                else:
                    sel = (pos >= t) & (pos < mid) if t < mid else (pos >= mid) & (pos < t)
                out[d, 1 + l, t, sel] = 1.0
    return jnp.asarray(out.reshape(2, (HG_LEVELS + 2) * CH, CH), dtype=BF16)


def _hg_scan_kernel(q_ref, f_ref, i_ref, lb_ref, dm_ref, o_ref, s_ref, *, heads, bsz):
    d = pl.program_id(0)
    step = pl.program_id(1)

    @pl.when(step == 0)
    def _():
        s_ref[...] = jnp.zeros_like(s_ref)

    row = lax.broadcasted_iota(jnp.int32, (CH, CH), 0)
    col = lax.broadcasted_iota(jnp.int32, (CH, CH), 1)
    sgn = jnp.where(d == 0, 1, -1)

    lb = lb_ref[...]
    masks = _half_block_masks(row, col, sgn)
    lvl_masks = [row == col] + masks
    hs = range(bsz * heads)
    sls = [slice(h * HEAD, (h + 1) * HEAD) for h in range(heads)] * bsz
    q_lvl, k_lvl, q_in, k_out, etot, v_all = [], [], [], [], [], []
    for b in range(bsz):
        f = f_ref[b].astype(F32)
        q = q_ref[b].astype(F32)
        q = q * _sigmoid(q)
        log_sig = jnp.minimum(f, 0.0) - jnp.log(1.0 + jnp.exp(-jnp.abs(f)))
        hi_term = jnp.log(1.0 - lb) + log_sig
        lo_term = jnp.log(jnp.where(lb > 0.0, lb, 1.0))
        mx = jnp.maximum(hi_term, lo_term)
        lf_mix = mx + jnp.log(1.0 + jnp.exp(-jnp.abs(hi_term - lo_term)))
        lf = jnp.where(lb > 0.0, lf_mix, hi_term)
        key = (1.0 - lb) * _sigmoid(-f)
        ex = jnp.exp(_dot_sel(dm_ref[0], lf))
        q_in.append((q * ex[0:CH]).astype(BF16))
        k_out.append((key * ex[CH:2 * CH]).astype(BF16))
        etot.append(jnp.exp(jnp.sum(lf, axis=0, keepdims=True)))
        v_all.append(i_ref[b].astype(BF16))
        q_lvl.append([q.astype(BF16)] + [(q * ex[(1 + l) * CH:(2 + l) * CH]).astype(BF16)
                                         for l in range(1, HG_LEVELS + 1)])
        k_lvl.append([key.astype(BF16)] + [(key * ex[(1 + l) * CH:(2 + l) * CH]).astype(BF16)
                                           for l in range(1, HG_LEVELS + 1)])

    bi = [h // heads for h in hs]
    s0 = [s_ref[h] for h in hs]
    sc = [jnp.zeros((CH, CH), F32) for _ in hs]
    for l in range(HG_LEVELS + 1):
        sc = [sc[h] + jnp.where(lvl_masks[l], _dot_nt(q_lvl[bi[h]][l][:, sls[h]], k_lvl[bi[h]][l][:, sls[h]]), 0.0)
              for h in hs]
    o = [_dot(sc[h].astype(BF16), v_all[bi[h]][:, sls[h]]) + _dot_nt(q_in[bi[h]][:, sls[h]], s0[h].astype(BF16))
         for h in hs]
    s_new = [s0[h] * etot[bi[h]][:, sls[h]] + _dot_tn(v_all[bi[h]][:, sls[h]], k_out[bi[h]][:, sls[h]]) for h in hs]
    for h in hs:
        o_ref[0, bi[h], :, sls[h]] = o[h].astype(o_ref.dtype)
        s_ref[h] = s_new[h]


def _hg_scan(p_hg, lb, n_ctx, col0=0):
    bsz, t, _ = p_hg.shape
    width = lb.shape[0]
    heads = width // HEAD
    nc = t // CH
    ncc = n_ctx // CH
    dm = _hg_decay_matrices()
    chunk = lambda d, i: _scan_chunk(d, i, ncc, nc)
    blk = (bsz, CH, width)
    return pl.pallas_call(
        functools.partial(_hg_scan_kernel, heads=heads, bsz=bsz),
        grid=(2, nc),
        in_specs=[
            pl.BlockSpec(blk, lambda d, i: (0, chunk(d, i), col0)),
            pl.BlockSpec(blk, lambda d, i: (0, chunk(d, i), col0 + 1 + d)),
            pl.BlockSpec(blk, lambda d, i: (0, chunk(d, i), col0 + 3)),
            pl.BlockSpec((1, width), lambda d, i: (0, 0)),
            pl.BlockSpec((1,) + dm.shape[1:], lambda d, i: (d, 0, 0)),
        ],
        out_specs=pl.BlockSpec((1, bsz, CH, width), lambda d, i: (d, 0, chunk(d, i), 0)),
        out_shape=jax.ShapeDtypeStruct((2, bsz, t, width), BF16),
        scratch_shapes=[pltpu.VMEM((bsz * heads, HEAD, HEAD), F32)],
        compiler_params=pltpu.CompilerParams(
            dimension_semantics=("arbitrary", "arbitrary"), vmem_limit_bytes=VMEM_LIMIT),
        name="hgrn2_scan",
    )(p_hg, p_hg, p_hg, lb.reshape(1, width), dm)


def _s5_operators(lam_re, lam_im, log_dt, b_re, b_im, c_re, c_im):
    c = S5_CHUNK
    hp = lax.Precision.HIGHEST
    dt = jnp.exp(log_dt.astype(F32))[..., None]
    lam_re = lam_re.astype(F32)
    lam_im = lam_im.astype(F32)
    mag = jnp.exp(lam_re * dt)
    ang = lam_im * dt
    a_re, a_im = mag * jnp.cos(ang), mag * jnp.sin(ang)
    den = lam_re * lam_re + lam_im * lam_im
    f_re = ((a_re - 1.0) * lam_re + a_im * lam_im) / den
    f_im = (a_im * lam_re - (a_re - 1.0) * lam_im) / den
    bb_re = f_re[..., None] * b_re - f_im[..., None] * b_im
    bb_im = f_re[..., None] * b_im + f_im[..., None] * b_re
    j = jnp.arange(c + 1, dtype=F32)[None, None, :, None]
    pmag = jnp.exp((lam_re * dt)[:, :, None, :] * j)
    pang = ang[:, :, None, :] * j
    pw_re, pw_im = pmag * jnp.cos(pang), pmag * jnp.sin(pang)
    ca_re = c_re[:, :, None] * pw_re[:, :, :, None] - c_im[:, :, None] * pw_im[:, :, :, None]
    ca_im = c_re[:, :, None] * pw_im[:, :, :, None] + c_im[:, :, None] * pw_re[:, :, :, None]
    kern = (jnp.einsum('dgjcp,dgpk->dgjck', ca_re, bb_re, precision=hp)
            - jnp.einsum('dgjcp,dgpk->dgjck', ca_im, bb_im, precision=hp))
    g = kern.shape[1]
    by_lag = jnp.concatenate([kern[1][:, c - 1:0:-1], kern[0][:, 0:1] + kern[1][:, 0:1], kern[0][:, 1:c]], axis=1)
    tpos = np.arange(c)
    lag = tpos[None, :] - tpos[:, None]
    onehot = jnp.asarray((lag[None] + c - 1) == np.arange(2 * c - 1)[:, None, None], dtype=F32)
    tmat = jnp.einsum('gjoc,jab->gacbo', by_lag, onehot, precision=hp).reshape(g, c * S5_CH, c * S5_CH)

    def in_to_state(pr, pi, d):
        hr = pr[:, :, None, :] * jnp.swapaxes(bb_re[d], 1, 2)[:, None] - pi[:, :, None, :] * jnp.swapaxes(bb_im[d], 1, 2)[:, None]
        hi = pr[:, :, None, :] * jnp.swapaxes(bb_im[d], 1, 2)[:, None] + pi[:, :, None, :] * jnp.swapaxes(bb_re[d], 1, 2)[:, None]
        return hr.reshape(g, c * S5_CH, -1), hi.reshape(g, c * S5_CH, -1)

    hf_re, hf_im = in_to_state(pw_re[0, :, c - 1::-1][:, :c], pw_im[0, :, c - 1::-1][:, :c], 0)
    hb_re, hb_im = in_to_state(pw_re[1, :, :c], pw_im[1, :, :c], 1)
    hmat = jnp.concatenate([hf_re, hb_re, hf_im, hb_im], axis=-1)

    def state_to_out(cr, ci):
        return (jnp.transpose(cr, (0, 3, 1, 2)).reshape(g, -1, c * S5_CH),
                -jnp.transpose(ci, (0, 3, 1, 2)).reshape(g, -1, c * S5_CH))

    gf_re, gf_im = state_to_out(ca_re[0, :, 1:c + 1], ca_im[0, :, 1:c + 1])
    gb_re, gb_im = state_to_out(ca_re[1, :, c:0:-1], ca_im[1, :, c:0:-1])
    z = jnp.zeros_like(gf_re)
    gf = jnp.concatenate([gf_re, z, gf_im, z], axis=1)
    gb = jnp.concatenate([z, gb_re, z, gb_im], axis=1)
    ac_re = jnp.concatenate([pw_re[0, :, c], pw_re[1, :, c]], axis=-1)[:, None, :]
    ac_im = jnp.concatenate([pw_im[0, :, c], pw_im[1, :, c]], axis=-1)[:, None, :]
    return tmat.astype(BF16), hmat.astype(BF16), gf.astype(BF16), gb.astype(BF16), ac_re, ac_im


def _s5_local_kernel(u_ref, h_ref, loc_ref):
    loc_ref[0] = _dot(u_ref[0].astype(BF16), h_ref[0])


def _s5_state_kernel(loc_ref, ar_ref, ai_ref, sf_ref, sb_ref, *, nc, ncc, bsz):
    ar = ar_ref[0]
    ai = ai_ref[0]
    half = ar.shape[-1]
    is_fwd = lax.broadcasted_iota(jnp.int32, (1, half), 1) < half // 2

    def body(i, carry):
        nb = jnp.where(i < ncc, ncc - 1 - i, nc + ncc - 1 - i)
        new = []
        for b in range(bsz):
            re, im = carry[b]
            st = jnp.concatenate([re, im], axis=1)
            sf_ref[0, b, pl.ds(i, 1), :] = st
            sb_ref[0, b, pl.ds(nb, 1), :] = st
            lf = loc_ref[0, b, pl.ds(i, 1), :]
            lk = loc_ref[0, b, pl.ds(nb, 1), :]
            l_re = jnp.where(is_fwd, lf[:, :half], lk[:, :half])
            l_im = jnp.where(is_fwd, lf[:, half:], lk[:, half:])
            new.append((ar * re - ai * im + l_re, ar * im + ai * re + l_im))
        return tuple(new)

    zero = jnp.zeros((1, half), F32)
    lax.fori_loop(0, nc, body, tuple((zero, zero) for _ in range(bsz)))


def _s5_out_kernel(u_ref, t_ref, sf_ref, gf_ref, sb_ref, gb_ref, d_ref, y_ref):
    u = u_ref[0]
    y = (_dot(u.astype(BF16), t_ref[0]) + _dot(sf_ref[0].astype(BF16), gf_ref[0])
         + _dot(sb_ref[0].astype(BF16), gb_ref[0]) + d_ref[0] * u.astype(F32))
    gelu = 0.5 * y * (1.0 + jnp.tanh(math.sqrt(2.0 / math.pi) * (y + 0.044715 * (y * y * y))))
    y_ref[0] = gelu.astype(y_ref.dtype)


def _s5_scan(p_s5, n_ctx, lam_re, lam_im, log_dt, b_re, b_im, c_re, c_im, d_skip):
    bsz, t, width = p_s5.shape
    g = width // S5_CH
    c = S5_CHUNK
    nc, ncc = t // c, n_ctx // c
    blk = c * S5_CH
    rows = bsz * nc
    tmat, hmat, gf, gb, ac_re, ac_im = _s5_operators(lam_re, lam_im, log_dt, b_re, b_im, c_re, c_im)
    u = jnp.transpose(p_s5.astype(F32).reshape(bsz, nc, c, g, S5_CH), (3, 0, 1, 2, 4)).reshape(g, rows, blk)
    per_g = lambda gi: (gi, 0, 0)
    cparams = pltpu.CompilerParams(dimension_semantics=("arbitrary",), vmem_limit_bytes=VMEM_LIMIT)
    loc = pl.pallas_call(
        _s5_local_kernel, grid=(g,),
        in_specs=[pl.BlockSpec((1, rows, blk), per_g), pl.BlockSpec((1, blk, 256), per_g)],
        out_specs=pl.BlockSpec((1, rows, 256), per_g),
        out_shape=jax.ShapeDtypeStruct((g, rows, 256), F32),
        compiler_params=cparams, name="s5_local",
    )(u, hmat)
    per_g4 = lambda gi: (gi, 0, 0, 0)
    st_shape = jax.ShapeDtypeStruct((g, bsz, nc, 256), F32)
    sf, sb = pl.pallas_call(
        functools.partial(_s5_state_kernel, nc=nc, ncc=ncc, bsz=bsz), grid=(g,),
        in_specs=[pl.BlockSpec((1, bsz, nc, 256), per_g4), pl.BlockSpec((1, 1, 128), per_g),
                  pl.BlockSpec((1, 1, 128), per_g)],
        out_specs=[pl.BlockSpec((1, bsz, nc, 256), per_g4)] * 2,
        out_shape=[st_shape, st_shape],
        compiler_params=cparams, name="s5_state",
    )(loc.reshape(g, bsz, nc, 256), ac_re, ac_im)
    d_flat = jnp.tile(d_skip.astype(F32).reshape(g, 1, S5_CH), (1, c, 1)).reshape(g, 1, blk)
    y = pl.pallas_call(
        _s5_out_kernel, grid=(g,),
        in_specs=[pl.BlockSpec((1, rows, blk), per_g), pl.BlockSpec((1, blk, blk), per_g),
                  pl.BlockSpec((1, rows, 256), per_g), pl.BlockSpec((1, 256, blk), per_g),
                  pl.BlockSpec((1, rows, 256), per_g), pl.BlockSpec((1, 256, blk), per_g),
                  pl.BlockSpec((1, 1, blk), per_g)],
        out_specs=pl.BlockSpec((1, rows, blk), per_g),
        out_shape=jax.ShapeDtypeStruct((g, rows, blk), F32),
        compiler_params=cparams, name="s5_out",
    )(u, tmat, sf.reshape(g, rows, 256), gf, sb.reshape(g, rows, 256), gb, d_flat)
    return jnp.transpose(y.reshape(g, bsz, nc, c, S5_CH), (1, 2, 3, 0, 4)).reshape(bsz, t, width)


IN_BM = 1280
IN_BN = 768
OUT_BM = 256
ROUTER_PAD = 128


def _row_block(t, target):
    bm = target
    while t % bm:
        bm -= CH
    return bm


def _norm_mod(x, mod_ref, first_row, n_ctx):
    xn = x * lax.rsqrt(jnp.mean(x * x, axis=-1, keepdims=True) + NORM_EPS)
    is_ctx = (first_row + lax.broadcasted_iota(jnp.int32, (x.shape[0], 1), 0)) < n_ctx
    scale = jnp.where(is_ctx, mod_ref[0, 0, 0:1, :], mod_ref[0, 1, 0:1, :])
    shift = jnp.where(is_ctx, mod_ref[0, 0, 1:2, :], mod_ref[0, 1, 1:2, :])
    return xn * scale + shift


def _in_proj_kernel(x_ref, mod_ref, w_ref, p_ref, h_ref, *, n_ctx, bm):
    @pl.when(pl.program_id(2) == 0)
    def _():
        sub = _row_block(bm, 256)

        def body(k, carry):
            rows = pl.ds(pl.multiple_of(k * sub, sub), sub)
            h = _norm_mod(x_ref[0, rows, :], mod_ref, pl.program_id(1) * bm + k * sub, n_ctx)
            h_ref[rows, :] = h.astype(BF16)
            return carry

        lax.fori_loop(0, bm // sub, body, 0)

    p_ref[0] = _dot(h_ref[...], w_ref[...]).astype(p_ref.dtype)


def _in_proj(tok, mod, w, n_ctx):
    bsz, t, d = tok.shape
    n = w.shape[1]
    bm = _row_block(t, IN_BM)
    return pl.pallas_call(
        functools.partial(_in_proj_kernel, n_ctx=n_ctx, bm=bm),
        grid=(bsz, t // bm, n // IN_BN),
        in_specs=[pl.BlockSpec((1, bm, d), lambda b, i, j: (b, i, 0), pipeline_mode=pl.Buffered(1)),
                  pl.BlockSpec((1, 2, 2, d), lambda b, i, j: (b, 0, 0, 0)),
                  pl.BlockSpec((d, IN_BN), lambda b, i, j: (0, j))],
        out_specs=pl.BlockSpec((1, bm, IN_BN), lambda b, i, j: (b, i, j)),
        out_shape=jax.ShapeDtypeStruct((bsz, t, n), F32),
        scratch_shapes=[pltpu.VMEM((bm, d), BF16)],
        compiler_params=pltpu.CompilerParams(
            dimension_semantics=("arbitrary", "arbitrary", "arbitrary"), vmem_limit_bytes=VMEM_LIMIT),
        name="in_proj",
    )(tok, mod, w)


def _out_proj_kernel(s5_ref, wglu_ref, bglu_ref,
                     y2_ref, r_ref, kd_ref, v_ref, g_ref, rk_ref, lnw_ref, lnb_ref,
                     o2_ref, og_ref, hgn_ref, bd_ref,
                     x_ref, gate_ref, wout_ref, modf_ref, wr_ref, br_ref,
                     xo_ref, h_ref, lg_ref, *, n_ctx, bm):
    first = pl.program_id(1) * bm
    bd = bd_ref[...]
    inv_n = 1.0 / HEAD

    def head_sum(z):
        return _dot(z.astype(BF16), bd)

    ys = s5_ref[0].astype(F32)
    ys = ys * _sigmoid(_dot(ys.astype(BF16), wglu_ref[...]) + bglu_ref[...])
    y = y2_ref[0, 0].astype(F32) + y2_ref[1, 0].astype(F32)
    yc = y - head_sum(y) * inv_n
    var = head_sum(yc * yc) * inv_n
    k_bonus = 0.5 * (kd_ref[0, 0].astype(F32) + kd_ref[1, 0].astype(F32))
    bonus = head_sum(r_ref[0].astype(F32) * k_bonus * rk_ref[...]) * v_ref[0].astype(F32)
    yr = (yc * lax.rsqrt(var + RW_GN_EPS) * lnw_ref[...] + lnb_ref[...] + bonus) * g_ref[0].astype(F32)
    o = o2_ref[0, 0].astype(F32) + o2_ref[1, 0].astype(F32)
    yh = o * lax.rsqrt(head_sum(o * o) * inv_n + NORM_EPS) * hgn_ref[...] * _sigmoid(og_ref[0].astype(F32))

    m = _dot(jnp.concatenate([ys, yr, yh], axis=-1).astype(BF16), wout_ref[...])
    is_ctx = (first + lax.broadcasted_iota(jnp.int32, (bm, 1), 0)) < n_ctx
    x_new = x_ref[0] + jnp.where(is_ctx, gate_ref[0, 0:1, :], gate_ref[0, 1:2, :]) * m
    xo_ref[0] = x_new
    h = _norm_mod(x_new, modf_ref, first, n_ctx).astype(BF16)
    h_ref[0] = h
    lg_ref[0] = _dot(h, wr_ref[...]) + br_ref[...]


def _out_proj(s5y, w_glu, b_glu, y2, r, kd, v, g, r_k, ln_w, ln_b, o2, p, og_col, hg_norm,
              tok, gate, w_out, mod_f, w_router, b_router, n_ctx):
    bsz, t, d = tok.shape
    bm = _row_block(t, OUT_BM)
    s5w = s5y.shape[-1]
    w = r.shape[-1]
    row = lambda b, i: (b, i, 0)
    row2 = lambda b, i: (0, b, i, 0)
    c2 = lambda b, i: (0, 0)
    vec = lambda z: z.reshape(1, -1).astype(F32)
    spec_w = pl.BlockSpec((1, bm, w), row)
    spec_2w = pl.BlockSpec((2, 1, bm, w), row2)
    spec_vw = pl.BlockSpec((1, w), c2)
    return pl.pallas_call(
        functools.partial(_out_proj_kernel, n_ctx=n_ctx, bm=bm),
        grid=(bsz, t // bm),
        in_specs=[
            pl.BlockSpec((1, bm, s5w), row), pl.BlockSpec((s5w, s5w), c2), pl.BlockSpec((1, s5w), c2),
            spec_2w, spec_w, spec_2w, spec_w, spec_w, spec_vw, spec_vw, spec_vw,
            spec_2w, pl.BlockSpec((1, bm, w), lambda b, i: (b, i, og_col)), spec_vw, pl.BlockSpec((w, w), c2),
            pl.BlockSpec((1, bm, d), row), pl.BlockSpec((1, 2, d), lambda b, i: (b, 0, 0)),
            pl.BlockSpec(w_out.shape, c2), pl.BlockSpec((1, 2, 2, d), lambda b, i: (b, 0, 0, 0)),
            pl.BlockSpec((d, ROUTER_PAD), c2), pl.BlockSpec((1, ROUTER_PAD), c2),
        ],
        out_specs=[pl.BlockSpec((1, bm, d), row), pl.BlockSpec((1, bm, d), row),
                   pl.BlockSpec((1, bm, ROUTER_PAD), row)],
        out_shape=[jax.ShapeDtypeStruct((bsz, t, d), F32), jax.ShapeDtypeStruct((bsz, t, d), BF16),
                   jax.ShapeDtypeStruct((bsz, t, ROUTER_PAD), F32)],
        compiler_params=pltpu.CompilerParams(
            dimension_semantics=("arbitrary", "arbitrary"), vmem_limit_bytes=VMEM_LIMIT),
        name="out_proj",
    )(s5y, w_glu.astype(BF16), vec(b_glu), y2, r, kd, v, g, vec(r_k), vec(ln_w), vec(ln_b),
      o2, p, vec(hg_norm), _head_block_diag(w), tok, gate, w_out.astype(BF16), mod_f,
      w_router.astype(BF16), b_router)


def _final_norm_kernel(x_ref, g_ref, o_ref):
    x = x_ref[0]
    o_ref[0] = x * lax.rsqrt(jnp.mean(x * x, axis=-1, keepdims=True) + NORM_EPS) * g_ref[...]


def _final_norm(x, g):
    bsz, t, d = x.shape
    bm = _row_block(t, 512)
    return pl.pallas_call(
        _final_norm_kernel, grid=(bsz, t // bm),
        in_specs=[pl.BlockSpec((1, bm, d), lambda b, i: (b, i, 0)), pl.BlockSpec((1, d), lambda b, i: (0, 0))],
        out_specs=pl.BlockSpec((1, bm, d), lambda b, i: (b, i, 0)),
        out_shape=jax.ShapeDtypeStruct((bsz, t, d), F32),
        compiler_params=pltpu.CompilerParams(dimension_semantics=("arbitrary", "arbitrary")),
        name="final_norm",
    )(x, g.reshape(1, d).astype(F32))


def _moe_kernel(meta_ref, x_ref, w1_ref, w3_ref, w2_ref, y_ref, *, n_blocks):
    active = pl.program_id(0) < meta_ref[n_blocks]

    @pl.when(active)
    def _():
        x = x_ref[...]
        a = _dot(x, w1_ref[0])
        b = _dot(x, w3_ref[0])
        y_ref[...] = _dot((a * _sigmoid(a) * b).astype(BF16), w2_ref[0]).astype(y_ref.dtype)

    @pl.when(jnp.logical_not(active))
    def _():
        y_ref[...] = jnp.zeros_like(y_ref)


def _cast_kernel(x_ref, o_ref):
    o_ref[0] = x_ref[0, 0].astype(o_ref.dtype)


def _layer_to_bf16(w, layer):
    _, e, r, c = w.shape
    return pl.pallas_call(
        _cast_kernel, grid=(e,),
        in_specs=[pl.BlockSpec((1, 1, r, c), lambda i: (layer, i, 0, 0))],
        out_specs=pl.BlockSpec((1, r, c), lambda i: (i, 0, 0)),
        out_shape=jax.ShapeDtypeStruct((e, r, c), BF16),
        compiler_params=pltpu.CompilerParams(dimension_semantics=("arbitrary",), vmem_limit_bytes=VMEM_LIMIT),
        name="cast_bf16",
    )(w)


def _moe_experts(xg, meta, w1, w3, w2):
    n_rows, d = xg.shape
    n_blocks = n_rows // MOE_BLOCK
    de = w1.shape[2]
    return pl.pallas_call(
        functools.partial(_moe_kernel, n_blocks=n_blocks),
        grid_spec=pltpu.PrefetchScalarGridSpec(
            num_scalar_prefetch=1, grid=(n_blocks,),
            in_specs=[pl.BlockSpec((MOE_BLOCK, d), lambda i, m: (i, 0)),
                      pl.BlockSpec((1, d, de), lambda i, m: (m[i], 0, 0)),
                      pl.BlockSpec((1, d, de), lambda i, m: (m[i], 0, 0)),
                      pl.BlockSpec((1, de, d), lambda i, m: (m[i], 0, 0))],
            out_specs=pl.BlockSpec((MOE_BLOCK, d), lambda i, m: (i, 0))),
        out_shape=jax.ShapeDtypeStruct((n_rows, d), F32),
        compiler_params=pltpu.CompilerParams(dimension_semantics=("arbitrary",), vmem_limit_bytes=VMEM_LIMIT),
        name="moe_experts",
    )(meta, xg, w1, w3, w2)


def _moe_ffn(h, logits, n_groups, n_experts, w1, w3, w2):
    n, d = h.shape
    epg = n_experts // n_groups
    group_logits = logits[:, :n_groups]
    g_sel = jnp.argmax(group_logits, axis=-1)
    g_gate = jnp.take_along_axis(jax.nn.softmax(group_logits, axis=-1), g_sel[:, None], axis=1)
    exp_logits = logits[:, n_groups:n_groups + n_experts].reshape(n, n_groups, epg)
    within = jnp.take_along_axis(exp_logits, g_sel[:, None, None], axis=1)[:, 0]
    top_val, top_idx = lax.top_k(within, TOP_K)
    gate = jax.nn.softmax(top_val, axis=-1) * g_gate
    e_flat = (g_sel[:, None] * epg + top_idx).reshape(-1).astype(jnp.int32)
    n_assign = n * TOP_K
    order = jnp.argsort(e_flat)
    rank = jnp.argsort(order).astype(jnp.int32)
    counts = jnp.sum(e_flat[:, None] == jnp.arange(n_experts, dtype=jnp.int32)[None, :], axis=0, dtype=jnp.int32)
    starts = jnp.cumsum(counts) - counts
    padded = (counts + MOE_BLOCK - 1) // MOE_BLOCK * MOE_BLOCK
    pad_ends = jnp.cumsum(padded)
    pad_starts = pad_ends - padded
    n_blocks = -(-n_assign // MOE_BLOCK) + n_experts
    n_rows = n_blocks * MOE_BLOCK
    block_start = jnp.arange(n_blocks, dtype=jnp.int32) * MOE_BLOCK
    block_exp = jnp.minimum(jnp.sum(pad_ends[None, :] <= block_start[:, None], axis=1, dtype=jnp.int32), n_experts - 1)
    rows = jnp.arange(n_rows, dtype=jnp.int32)
    row_exp = jnp.repeat(block_exp, MOE_BLOCK)
    offset = rows - pad_starts[row_exp]
    src = jnp.clip(starts[row_exp] + offset, 0, n_assign - 1)
    row_tok = jnp.where(offset < counts[row_exp], order[src].astype(jnp.int32) // TOP_K, 0)
    meta = jnp.concatenate([block_exp, (pad_ends[-1:] // MOE_BLOCK).astype(jnp.int32)])
    y_rows = _moe_experts(h[row_tok], meta, w1, w3, w2)
    pos = (pad_starts[e_flat] + rank - starts[e_flat]).reshape(n, TOP_K)
    return gate[:, 0:1] * y_rows[pos[:, 0]] + gate[:, 1:2] * y_rows[pos[:, 1]]


def kernel(x, c, ctx, c_ctx, ada_w, ada_b, norm_mix, norm_ffn, w_in, w_out, s5_lam_re, s5_lam_im, s5_log_dt, s5_b_re, s5_b_im, s5_c_re, s5_c_im, s5_d, s5_w_glu, s5_b_glu, rw_mu, rw_w0, rw_w2, rw_a0, rw_a2, rw_g2, rw_k_k, rw_k_a, rw_r_k, rw_ln_w, rw_ln_b, hg_lb_logits, hg_norm, moe_w_group, moe_b_group, moe_w_expert, moe_b_expert, moe_w1, moe_w3, moe_w2, final_norm):
    bsz, n_lat, d = x.shape
    n_ctx = ctx.shape[1]
    depth = w_in.shape[0]
    s5_width = s5_d.shape[1]
    rw_width = rw_k_k.shape[1]
    rw_in = rw_mu.shape[2]
    hg_width = hg_norm.shape[1]
    n_groups = moe_w_group.shape[2]
    n_experts = moe_w_expert.shape[2]
    t = n_ctx + n_lat
    hg_col0 = -(-rw_in // hg_width)
    s5_off = (hg_col0 + 5) * hg_width
    n_proj = -(-(s5_off + s5_width) // IN_BN) * IN_BN
    pad1 = hg_col0 * hg_width - rw_in
    pad2 = n_proj - s5_off - s5_width

    silu_c = jax.nn.silu(c)
    silu_cc = jax.nn.silu(c_ctx)
    lb_all = jnp.cumsum(jax.nn.softmax(hg_lb_logits.astype(F32), axis=0), axis=0)
    lb_all = lb_all - lb_all[0:1]
    tok = jnp.concatenate([ctx, x], axis=1)
    for l in range(depth):
        last = l == depth - 1
        mod_lat = jnp.split(silu_c @ ada_w[l] + ada_b[l], 6, axis=-1)
        mod_ctx = [jnp.broadcast_to(z, (bsz, d)) for z in jnp.split(silu_cc @ ada_w[l] + ada_b[l], 6, axis=-1)]
        sh_m, sc_m, gt_m, sh_f, sc_f, gt_f = [jnp.stack([zc, zl], axis=1) for zc, zl in zip(mod_ctx, mod_lat)]
        mod_m = jnp.stack([norm_mix[l] * (1.0 + sc_m), sh_m], axis=2)
        mod_f = jnp.stack([norm_ffn[l] * (1.0 + sc_f), sh_f], axis=2)

        w_cols = [w_in[l][:, s5_width:s5_width + rw_in], jnp.zeros((d, pad1), F32),
                  w_in[l][:, s5_width + rw_in:], w_in[l][:, :s5_width], jnp.zeros((d, pad2), F32)]
        p = _in_proj(tok, mod_m, jnp.concatenate(w_cols, axis=1).astype(BF16), n_ctx)

        r, v, kk, g, lw, kd, bb = _rw_prep(p, n_ctx, rw_mu[l], rw_w0[l], rw_w2[l], rw_a0[l], rw_a2[l],
                                           rw_g2[l], rw_k_k[l], rw_k_a[l])
        y2 = _rw_scan(r, lw, kd, v, kk, bb, n_ctx)
        o2 = _hg_scan(p, lb_all[l], n_ctx, col0=hg_col0)
        s5y = _s5_scan(p[:, :, s5_off:s5_off + s5_width], n_ctx, s5_lam_re[l], s5_lam_im[l], s5_log_dt[l],
                       s5_b_re[l], s5_b_im[l], s5_c_re[l], s5_c_im[l], s5_d[l])

        w_router = jnp.concatenate([moe_w_group[l], moe_w_expert[l],
                                    jnp.zeros((d, ROUTER_PAD - n_groups - n_experts), F32)], axis=1)
        b_router = jnp.concatenate([moe_b_group[l], moe_b_expert[l],
                                    jnp.zeros((ROUTER_PAD - n_groups - n_experts,), F32)]).reshape(1, ROUTER_PAD)
        tok, h_ffn, logits = _out_proj(
            s5y, s5_w_glu[l], s5_b_glu[l], y2, r, kd, v, g, rw_r_k[l], rw_ln_w[l], rw_ln_b[l],
            o2, p, hg_col0 + 4, hg_norm[l], tok, gt_m, w_out[l], mod_f, w_router, b_router, n_ctx)

        experts = (_layer_to_bf16(moe_w1, l), _layer_to_bf16(moe_w3, l), _layer_to_bf16(moe_w2, l))
        if last:
            f = _moe_ffn(h_ffn[:, n_ctx:].reshape(bsz * n_lat, d), logits[:, n_ctx:].reshape(bsz * n_lat, ROUTER_PAD),
                         n_groups, n_experts, *experts).reshape(bsz, n_lat, d)
            x = tok[:, n_ctx:] + gt_f[:, 1:2] * f
        else:
            f = _moe_ffn(h_ffn.reshape(bsz * t, d), logits.reshape(bsz * t, ROUTER_PAD),
                         n_groups, n_experts, *experts).reshape(bsz, t, d)
            tok = jnp.concatenate([tok[:, :n_ctx] + gt_f[:, 0:1] * f[:, :n_ctx],
                                   tok[:, n_ctx:] + gt_f[:, 1:2] * f[:, n_ctx:]], axis=1)
    return _final_norm(x, final_norm)
```

```python
import functools
import math

import numpy as np
import jax
import jax.numpy as jnp
from jax import lax
from jax.experimental import pallas as pl
from jax.experimental.pallas import tpu as pltpu

F32 = jnp.float32
BF16 = jnp.bfloat16

GRID_W = 64
CH = GRID_W
HEAD = 64
S5_CH = 16
S5_STATE = 64
S5_CHUNK = 32
RW_LORA = 64
RW_GN_EPS = 64e-5
NORM_EPS = 1e-6
TOP_K = 2
MOE_BLOCK = 256
VMEM_LIMIT = 48 * 1024 * 1024


def _dot(a, b):
    return jnp.dot(a, b, preferred_element_type=F32)


def _dot_nt(a, b):
    return lax.dot_general(a, b, (((1,), (1,)), ((), ())), preferred_element_type=F32)


def _dot_tn(a, b):
    return lax.dot_general(a, b, (((0,), (0,)), ((), ())), preferred_element_type=F32)


def _split3(x):
    hi = x.astype(BF16)
    r1 = x - hi.astype(F32)
    mid = r1.astype(BF16)
    lo = (r1 - mid.astype(F32)).astype(BF16)
    return hi, mid, lo


def _dot_sel(m01, x):
    hi, mid, lo = _split3(x)
    return _dot(m01, hi) + _dot(m01, mid) + _dot(m01, lo)


def _dot_hp(a, b):
    a_hi = a.astype(BF16)
    a_lo = (a - a_hi.astype(F32)).astype(BF16)
    b_hi = b.astype(BF16)
    b_lo = (b - b_hi.astype(F32)).astype(BF16)
    return _dot(a_hi, b_hi) + _dot(a_hi, b_lo) + _dot(a_lo, b_hi)


def _sigmoid(x):
    return 1.0 / (1.0 + jnp.exp(-x))


def _scan_chunk(d, i, ncc, nc):
    back = jnp.where(i < ncc, ncc - 1 - i, nc + ncc - 1 - i)
    return jnp.where(d == 0, i, back)


def _half_block_masks(row, col, sgn):
    masks = []
    n = 2
    while n <= CH:
        same = (row // n) == (col // n)
        t_hi = ((row % n) >= n // 2).astype(jnp.int32)
        s_hi = ((col % n) >= n // 2).astype(jnp.int32)
        masks.append(jnp.logical_and(same, (t_hi - s_hi) * sgn == 1))
        n *= 2
    return masks


def _cast_plan(w, layer, n_steps):
    _, e, r, c = w.shape
    total = e * r
    rows = 16
    while rows * n_steps < total or total % rows:
        rows *= 2
    last = total // rows - 1
    slab = lambda d, i: jnp.minimum(d * (n_steps // 2) + i, last)
    return (w.reshape(w.shape[0], total, c),
            pl.BlockSpec((1, rows, c), lambda d, i: (layer, slab(d, i), 0)),
            pl.BlockSpec((rows, c), lambda d, i: (slab(d, i), 0)),
            jax.ShapeDtypeStruct((total, c), BF16))


def _cast_slabs(in_refs, out_refs):
    for wi, wo in zip(in_refs, out_refs):
        wo[...] = wi[0].astype(wo.dtype)


def _head_block_diag(width):
    idx = np.arange(width) // HEAD
    return jnp.asarray(idx[:, None] == idx[None, :], dtype=BF16)


def _rw_prep_kernel(cur_ref, prev_ref, next_ref, mu_ref, w0_ref, w2_ref, a0_ref, a2_ref, g2_ref,
                    kkw_ref, ka_ref, bd_ref,
                    r_out, v_out, kk_out, g_out, lw_out, kd_out, bb_out, *, ncc, nc, width):
    n = pl.program_id(1)
    cur = cur_ref[0].astype(F32)
    prv = prev_ref[0].astype(F32)
    nxt = next_ref[0].astype(F32)
    is_ctx = n < ncc
    row = lax.broadcasted_iota(jnp.int32, (CH, 1), 0)
    zero_row = jnp.zeros((1, cur.shape[1]), F32)
    first_fill = jnp.where(jnp.logical_and(is_ctx, n > 0), prv[CH - 1:CH, :], zero_row)
    last_fill = jnp.where(jnp.logical_and(is_ctx, n < ncc - 1), nxt[0:1, :], zero_row)
    nb_prev = jnp.where(row == 0, first_fill, pltpu.roll(cur, 1, 0))
    nb_next = jnp.where(row == CH - 1, last_fill, pltpu.roll(cur, CH - 1, 0))
    mu = mu_ref[...]
    zero_mu = jnp.zeros_like(mu[0:1])
    w_up = jnp.where(jnp.logical_and(jnp.logical_not(is_ctx), n > ncc), mu[2:3], zero_mu)
    w_down = jnp.where(jnp.logical_and(jnp.logical_not(is_ctx), n < nc - 1), mu[3:4], zero_mu)
    w_cur = 1.0 - mu[0:1] - mu[1:2] - jnp.where(is_ctx, zero_mu, mu[2:3] + mu[3:4])
    x = w_cur * cur + mu[0:1] * nb_prev + mu[1:2] * nb_next + w_up * prv + w_down * nxt

    w = width
    r = x[:, 0:w]
    k = x[:, w:2 * w]
    v = x[:, 2 * w:3 * w]
    wd = x[:, 3 * w:3 * w + 2 * RW_LORA]
    ad = x[:, 3 * w + 2 * RW_LORA:3 * w + 4 * RW_LORA]
    gd = x[:, 3 * w + 4 * RW_LORA:]

    r_out[0] = r.astype(r_out.dtype)
    v_out[0] = v.astype(v_out.dtype)
    g_out[0] = _dot_hp(_sigmoid(gd), g2_ref[...]).astype(g_out.dtype)
    kk = k * kkw_ref[...]
    kk2 = kk * kk
    kk2_hi = kk2.astype(BF16)
    kk2_lo = (kk2 - kk2_hi.astype(F32)).astype(BF16)
    ss = _dot(kk2_hi, bd_ref[...]) + _dot(kk2_lo, bd_ref[...])
    kk = kk / jnp.maximum(jnp.sqrt(ss), 1e-12)
    kk_out[0] = kk.astype(kk_out.dtype)
    ka = ka_ref[...]
    for d in range(2):
        w_raw = w0_ref[d:d + 1, :] + _dot_hp(jnp.tanh(wd[:, d * RW_LORA:(d + 1) * RW_LORA]), w2_ref[d])
        nw = -w_raw
        softplus = jnp.maximum(nw, 0.0) + jnp.log(1.0 + jnp.exp(-jnp.abs(nw)))
        lw_out[d, 0] = -jnp.exp(-softplus - 0.5)
        a = _sigmoid(a0_ref[d:d + 1, :] + _dot_hp(ad[:, d * RW_LORA:(d + 1) * RW_LORA], a2_ref[d]))
        kd_out[d, 0] = (k * (1.0 + (a - 1.0) * ka)).astype(kd_out.dtype)
        bb_out[d, 0] = (kk * a).astype(bb_out.dtype)


def _rw_prep(p_rw, n_ctx, mu, w0, w2, a0, a2, g2, k_k, k_a):
    bsz, t, _ = p_rw.shape
    win = mu.shape[1]
    width = k_k.shape[0]
    nc = t // CH
    ncc = n_ctx // CH
    tok = lambda b, n: (b, n, 0)
    full2 = lambda b, n: (0, 0)
    full3 = lambda b, n: (0, 0, 0)
    blk_in = (1, CH, win)
    out1 = jax.ShapeDtypeStruct((bsz, t, width), BF16)
    out2 = jax.ShapeDtypeStruct((2, bsz, t, width), BF16)
    out_lw = jax.ShapeDtypeStruct((2, bsz, t, width), F32)
    spec1 = pl.BlockSpec((1, CH, width), tok)
    spec2 = pl.BlockSpec((2, 1, CH, width), lambda b, n: (0, b, n, 0))
    return pl.pallas_call(
        functools.partial(_rw_prep_kernel, ncc=ncc, nc=nc, width=width),
        grid=(bsz, nc),
        in_specs=[
            pl.BlockSpec(blk_in, tok),
            pl.BlockSpec(blk_in, lambda b, n: (b, jnp.maximum(n - 1, 0), 0)),
            pl.BlockSpec(blk_in, lambda b, n: (b, jnp.minimum(n + 1, nc - 1), 0)),
            pl.BlockSpec(mu.shape, full2),
            pl.BlockSpec(w0.shape, full2),
            pl.BlockSpec(w2.shape, full3),
            pl.BlockSpec(a0.shape, full2),
            pl.BlockSpec(a2.shape, full3),
            pl.BlockSpec(g2.shape, full2),
            pl.BlockSpec((1, width), full2),
            pl.BlockSpec((1, width), full2),
            pl.BlockSpec((width, width), full2),
        ],
        out_specs=[spec1, spec1, spec1, spec1, spec2, spec2, spec2],
        out_shape=[out1, out1, out1, out1, out_lw, out2, out2],
        compiler_params=pltpu.CompilerParams(
            dimension_semantics=("arbitrary", "arbitrary"), vmem_limit_bytes=VMEM_LIMIT),
        name="rwkv_prep",
    )(p_rw, p_rw, p_rw, mu, w0, w2, a0, a2, g2, k_k.reshape(1, width), k_a.reshape(1, width),
      _head_block_diag(width))


def _rw_scan_kernel(r_ref, lw_ref, kd_ref, v_ref, kk_ref, bb_ref, *rest, heads, bsz, n_cast):
    y_ref, s_ref = rest[n_cast], rest[-1]
    _cast_slabs(rest[:n_cast], rest[n_cast + 1:-1])
    d = pl.program_id(0)
    i = pl.program_id(1)

    @pl.when(i == 0)
    def _():
        s_ref[...] = jnp.zeros_like(s_ref)

    row = lax.broadcasted_iota(jnp.int32, (CH, CH), 0)
    col = lax.broadcasted_iota(jnp.int32, (CH, CH), 1)
    sgn = jnp.where(d == 0, 1, -1)
    strict = (row - col) * sgn > 0
    diag = row == col
    incl = jnp.logical_or(strict, diag)
    tri = jnp.where(incl, 1.0, 0.0).astype(BF16)
    eye = jnp.where(diag, 1.0, 0.0)
    halves = _half_block_masks(row, col, sgn)

    hs = range(bsz * heads)
    sls = [slice(h * HEAD, (h + 1) * HEAD) for h in range(heads)] * bsz
    s0 = [s_ref[h] for h in hs]
    v, kkr, bk, b_w, k_w, wtot = [], [], [], [], [], []
    for b in range(bsz):
        lw = lw_ref[0, b]
        cw = _dot_sel(tri, lw)
        ctot = jnp.sum(lw, axis=0, keepdims=True)
        e_ncw = jnp.exp(-cw)
        e_rem = jnp.exp(ctot - cw)
        r_t = (r_ref[b].astype(F32) * jnp.exp(cw)).astype(BF16)
        kk_t = (kk_ref[b].astype(F32) * jnp.exp(cw - lw)).astype(BF16)
        bb = bb_ref[0, b].astype(F32)
        kd = kd_ref[0, b].astype(F32)
        b_t = (bb * e_ncw).astype(BF16)
        k_t = (kd * e_ncw).astype(BF16)
        bw_all = (bb * e_rem).astype(BF16)
        kw_all = (kd * e_rem).astype(BF16)
        wt_all = jnp.exp(ctot)
        v_all = v_ref[b]
        for sl in sls[:heads]:
            v.append(v_all[:, sl])
            kkr.append(jnp.concatenate([kk_t[:, sl], r_t[:, sl]], axis=0))
            bk.append(jnp.concatenate([b_t[:, sl], k_t[:, sl]], axis=0))
            b_w.append(bw_all[:, sl])
            k_w.append(kw_all[:, sl])
            wtot.append(wt_all[:, sl])
    att = [_dot_nt(kkr[h], bk[h]) for h in hs]
    proj = [_dot_nt(kkr[h], s0[h].astype(BF16)) for h in hs]
    a_ab = [jnp.where(strict, att[h][:CH, :CH], 0.0) for h in hs]
    a_ak = [jnp.where(strict, att[h][:CH, CH:], 0.0).astype(BF16) for h in hs]
    a_rb = [jnp.where(incl, att[h][CH:, :CH], 0.0).astype(BF16) for h in hs]
    a_rk = [jnp.where(incl, att[h][CH:, CH:], 0.0).astype(BF16) for h in hs]
    inv = [eye - jnp.where(halves[0], a_ab[h], 0.0) for h in hs]
    for lvl in range(1, len(halves)):
        inv_b = [inv[h].astype(BF16) for h in hs]
        a_l = [jnp.where(halves[lvl], a_ab[h], 0.0).astype(BF16) for h in hs]
        left = [_dot(inv_b[h], a_l[h]).astype(BF16) for h in hs]
        inv = [inv[h] - _dot(left[h], inv_b[h]) for h in hs]
    rhs = [(proj[h][:CH] + _dot(a_ak[h], v[h])).astype(BF16) for h in hs]
    u_b = [(-_dot(inv[h].astype(BF16), rhs[h])).astype(BF16) for h in hs]
    y = [proj[h][CH:] + _dot(a_rb[h], u_b[h]) + _dot(a_rk[h], v[h]) for h in hs]
    s_new = [s0[h] * wtot[h] + _dot_tn(u_b[h], b_w[h]) + _dot_tn(v[h], k_w[h]) for h in hs]
    for h in hs:
        y_ref[0, h // heads, :, sls[h]] = y[h].astype(y_ref.dtype)
        s_ref[h] = s_new[h]


def _rw_scan(r, lw, kd, v, kk, bb, n_ctx, casts=()):
    bsz, t, width = r.shape
    heads = width // HEAD
    nc = t // CH
    ncc = n_ctx // CH
    s1 = pl.BlockSpec((bsz, CH, width), lambda d, i: (0, _scan_chunk(d, i, ncc, nc), 0))
    s2 = pl.BlockSpec((1, bsz, CH, width), lambda d, i: (d, 0, _scan_chunk(d, i, ncc, nc), 0))
    plans = [_cast_plan(w, layer, 2 * nc) for w, layer in casts]
    return pl.pallas_call(
        functools.partial(_rw_scan_kernel, heads=heads, bsz=bsz, n_cast=len(plans)),
        grid=(2, nc),
        in_specs=[s1, s2, s2, s1, s1, s2] + [p[1] for p in plans],
        out_specs=[s2] + [p[2] for p in plans],
        out_shape=[jax.ShapeDtypeStruct((2, bsz, t, width), BF16)] + [p[3] for p in plans],
        scratch_shapes=[pltpu.VMEM((bsz * heads, HEAD, HEAD), F32)],
        compiler_params=pltpu.CompilerParams(
            dimension_semantics=("arbitrary", "arbitrary"), vmem_limit_bytes=VMEM_LIMIT),
        name="rwkv_scan",
    )(r, lw, kd, v, kk, bb, *[p[0] for p in plans])


HG_LEVELS = 6


def _hg_decay_matrices():
    out = np.zeros((2, HG_LEVELS + 2, CH, CH), np.float32)
    pos = np.arange(CH)
    for d in range(2):
        for t in range(CH):
            if d == 0:
                out[d, 0, t, :t + 1] = 1.0
                out[d, 1, t, t + 1:] = 1.0
            else:
                out[d, 0, t, t:] = 1.0
                out[d, 1, t, :t] = 1.0
            for l in range(1, HG_LEVELS + 1):
                n = 1 << l
                mid = (t // n) * n + n // 2
                if d == 0:
                    sel = (pos >= mid) & (pos <= t) if t >= mid else (pos > t) & (pos < mid)
                else:
                    sel = (pos >= t) & (pos < mid) if t < mid else (pos >= mid) & (pos < t)
                out[d, 1 + l, t, sel] = 1.0
    return jnp.asarray(out.reshape(2, (HG_LEVELS + 2) * CH, CH), dtype=BF16)


def _hg_scan_kernel(q_ref, f_ref, i_ref, lb_ref, dm_ref, *rest, heads, bsz, n_cast):
    o_ref, s_ref = rest[n_cast], rest[-1]
    _cast_slabs(rest[:n_cast], rest[n_cast + 1:-1])
    d = pl.program_id(0)
    step = pl.program_id(1)

    @pl.when(step == 0)
    def _():
        s_ref[...] = jnp.zeros_like(s_ref)

    row = lax.broadcasted_iota(jnp.int32, (CH, CH), 0)
    col = lax.broadcasted_iota(jnp.int32, (CH, CH), 1)
    sgn = jnp.where(d == 0, 1, -1)

    lb = lb_ref[...]
    masks = _half_block_masks(row, col, sgn)
    lvl_masks = [row == col] + masks
    hs = range(bsz * heads)
    sls = [slice(h * HEAD, (h + 1) * HEAD) for h in range(heads)] * bsz
    q_lvl, k_lvl, q_in, k_out, etot, v_all = [], [], [], [], [], []
    for b in range(bsz):
        f = f_ref[b].astype(F32)
        q = q_ref[b].astype(F32)
        q = q * _sigmoid(q)
        log_sig = jnp.minimum(f, 0.0) - jnp.log(1.0 + jnp.exp(-jnp.abs(f)))
        hi_term = jnp.log(1.0 - lb) + log_sig
        lo_term = jnp.log(jnp.where(lb > 0.0, lb, 1.0))
        mx = jnp.maximum(hi_term, lo_term)
        lf_mix = mx + jnp.log(1.0 + jnp.exp(-jnp.abs(hi_term - lo_term)))
        lf = jnp.where(lb > 0.0, lf_mix, hi_term)
        key = (1.0 - lb) * _sigmoid(-f)
        ex = jnp.exp(_dot_sel(dm_ref[0], lf))
        q_in.append((q * ex[0:CH]).astype(BF16))
        k_out.append((key * ex[CH:2 * CH]).astype(BF16))
        etot.append(jnp.exp(jnp.sum(lf, axis=0, keepdims=True)))
        v_all.append(i_ref[b].astype(BF16))
        q_lvl.append([q.astype(BF16)] + [(q * ex[(1 + l) * CH:(2 + l) * CH]).astype(BF16)
                                         for l in range(1, HG_LEVELS + 1)])
        k_lvl.append([key.astype(BF16)] + [(key * ex[(1 + l) * CH:(2 + l) * CH]).astype(BF16)
                                           for l in range(1, HG_LEVELS + 1)])

    bi = [h // heads for h in hs]
    s0 = [s_ref[h] for h in hs]
    sc = [jnp.zeros((CH, CH), F32) for _ in hs]
    for l in range(HG_LEVELS + 1):
        sc = [sc[h] + jnp.where(lvl_masks[l], _dot_nt(q_lvl[bi[h]][l][:, sls[h]], k_lvl[bi[h]][l][:, sls[h]]), 0.0)
              for h in hs]
    o = [_dot(sc[h].astype(BF16), v_all[bi[h]][:, sls[h]]) + _dot_nt(q_in[bi[h]][:, sls[h]], s0[h].astype(BF16))
         for h in hs]
    s_new = [s0[h] * etot[bi[h]][:, sls[h]] + _dot_tn(v_all[bi[h]][:, sls[h]], k_out[bi[h]][:, sls[h]]) for h in hs]
    for h in hs:
        o_ref[0, bi[h], :, sls[h]] = o[h].astype(o_ref.dtype)
        s_ref[h] = s_new[h]


def _hg_scan(p_hg, lb, n_ctx, col0=0, casts=()):
    bsz, t, _ = p_hg.shape
    width = lb.shape[0]
    heads = width // HEAD
    nc = t // CH
    ncc = n_ctx // CH
    dm = _hg_decay_matrices()
    chunk = lambda d, i: _scan_chunk(d, i, ncc, nc)
    blk = (bsz, CH, width)
    plans = [_cast_plan(w, layer, 2 * nc) for w, layer in casts]
    return pl.pallas_call(
        functools.partial(_hg_scan_kernel, heads=heads, bsz=bsz, n_cast=len(plans)),
        grid=(2, nc),
        in_specs=[
            pl.BlockSpec(blk, lambda d, i: (0, chunk(d, i), col0)),
            pl.BlockSpec(blk, lambda d, i: (0, chunk(d, i), col0 + 1 + d)),
            pl.BlockSpec(blk, lambda d, i: (0, chunk(d, i), col0 + 3)),
            pl.BlockSpec((1, width), lambda d, i: (0, 0)),
            pl.BlockSpec((1,) + dm.shape[1:], lambda d, i: (d, 0, 0)),
        ] + [p[1] for p in plans],
        out_specs=[pl.BlockSpec((1, bsz, CH, width), lambda d, i: (d, 0, chunk(d, i), 0))] + [p[2] for p in plans],
        out_shape=[jax.ShapeDtypeStruct((2, bsz, t, width), BF16)] + [p[3] for p in plans],
        scratch_shapes=[pltpu.VMEM((bsz * heads, HEAD, HEAD), F32)],
        compiler_params=pltpu.CompilerParams(
            dimension_semantics=("arbitrary", "arbitrary"), vmem_limit_bytes=VMEM_LIMIT),
        name="hgrn2_scan",
    )(p_hg, p_hg, p_hg, lb.reshape(1, width), dm, *[p[0] for p in plans])


def _s5_operators(lam_re, lam_im, log_dt, b_re, b_im, c_re, c_im):
    c = S5_CHUNK
    hp = lax.Precision.HIGHEST
    dt = jnp.exp(log_dt.astype(F32))[..., None]
    lam_re = lam_re.astype(F32)
    lam_im = lam_im.astype(F32)
    mag = jnp.exp(lam_re * dt)
    ang = lam_im * dt
    a_re, a_im = mag * jnp.cos(ang), mag * jnp.sin(ang)
    den = lam_re * lam_re + lam_im * lam_im
    f_re = ((a_re - 1.0) * lam_re + a_im * lam_im) / den
    f_im = (a_im * lam_re - (a_re - 1.0) * lam_im) / den
    bb_re = f_re[..., None] * b_re - f_im[..., None] * b_im
    bb_im = f_re[..., None] * b_im + f_im[..., None] * b_re
    j = jnp.arange(c + 1, dtype=F32)[None, None, :, None]
    pmag = jnp.exp((lam_re * dt)[:, :, None, :] * j)
    pang = ang[:, :, None, :] * j
    pw_re, pw_im = pmag * jnp.cos(pang), pmag * jnp.sin(pang)
    ca_re = c_re[:, :, None] * pw_re[:, :, :, None] - c_im[:, :, None] * pw_im[:, :, :, None]
    ca_im = c_re[:, :, None] * pw_im[:, :, :, None] + c_im[:, :, None] * pw_re[:, :, :, None]
    kern = (jnp.einsum('dgjcp,dgpk->dgjck', ca_re, bb_re, precision=hp)
            - jnp.einsum('dgjcp,dgpk->dgjck', ca_im, bb_im, precision=hp))
    g = kern.shape[1]
    by_lag = jnp.concatenate([kern[1][:, c - 1:0:-1], kern[0][:, 0:1] + kern[1][:, 0:1], kern[0][:, 1:c]], axis=1)
    tpos = np.arange(c)
    lag = tpos[None, :] - tpos[:, None]
    onehot = jnp.asarray((lag[None] + c - 1) == np.arange(2 * c - 1)[:, None, None], dtype=F32)
    tmat = jnp.einsum('gjoc,jab->gacbo', by_lag, onehot, precision=hp).reshape(g, c * S5_CH, c * S5_CH)

    def in_to_state(pr, pi, d):
        hr = pr[:, :, None, :] * jnp.swapaxes(bb_re[d], 1, 2)[:, None] - pi[:, :, None, :] * jnp.swapaxes(bb_im[d], 1, 2)[:, None]
        hi = pr[:, :, None, :] * jnp.swapaxes(bb_im[d], 1, 2)[:, None] + pi[:, :, None, :] * jnp.swapaxes(bb_re[d], 1, 2)[:, None]
        return hr.reshape(g, c * S5_CH, -1), hi.reshape(g, c * S5_CH, -1)

    hf_re, hf_im = in_to_state(pw_re[0, :, c - 1::-1][:, :c], pw_im[0, :, c - 1::-1][:, :c], 0)
    hb_re, hb_im = in_to_state(pw_re[1, :, :c], pw_im[1, :, :c], 1)
    hmat = jnp.concatenate([hf_re, hb_re, hf_im, hb_im], axis=-1)

    def state_to_out(cr, ci):
        return (jnp.transpose(cr, (0, 3, 1, 2)).reshape(g, -1, c * S5_CH),
                -jnp.transpose(ci, (0, 3, 1, 2)).reshape(g, -1, c * S5_CH))

    gf_re, gf_im = state_to_out(ca_re[0, :, 1:c + 1], ca_im[0, :, 1:c + 1])
    gb_re, gb_im = state_to_out(ca_re[1, :, c:0:-1], ca_im[1, :, c:0:-1])
    z = jnp.zeros_like(gf_re)
    gf = jnp.concatenate([gf_re, z, gf_im, z], axis=1)
    gb = jnp.concatenate([z, gb_re, z, gb_im], axis=1)
    ac_re = jnp.concatenate([pw_re[0, :, c], pw_re[1, :, c]], axis=-1)[:, None, :]
    ac_im = jnp.concatenate([pw_im[0, :, c], pw_im[1, :, c]], axis=-1)[:, None, :]
    return tmat.astype(BF16), hmat.astype(BF16), gf.astype(BF16), gb.astype(BF16), ac_re, ac_im


def _s5_local_kernel(u_ref, h_ref, loc_ref):
    loc_ref[0] = _dot(u_ref[0].astype(BF16), h_ref[0])


def _s5_state_kernel(loc_ref, ar_ref, ai_ref, sf_ref, sb_ref, *, nc, ncc, bsz):
    ar = ar_ref[0]
    ai = ai_ref[0]
    half = ar.shape[-1]
    is_fwd = lax.broadcasted_iota(jnp.int32, (1, half), 1) < half // 2

    def body(i, carry):
        nb = jnp.where(i < ncc, ncc - 1 - i, nc + ncc - 1 - i)
        new = []
        for b in range(bsz):
            re, im = carry[b]
            st = jnp.concatenate([re, im], axis=1)
            sf_ref[0, b, pl.ds(i, 1), :] = st
            sb_ref[0, b, pl.ds(nb, 1), :] = st
            lf = loc_ref[0, b, pl.ds(i, 1), :]
            lk = loc_ref[0, b, pl.ds(nb, 1), :]
            l_re = jnp.where(is_fwd, lf[:, :half], lk[:, :half])
            l_im = jnp.where(is_fwd, lf[:, half:], lk[:, half:])
            new.append((ar * re - ai * im + l_re, ar * im + ai * re + l_im))
        return tuple(new)

    zero = jnp.zeros((1, half), F32)
    lax.fori_loop(0, nc, body, tuple((zero, zero) for _ in range(bsz)))


def _s5_out_kernel(u_ref, t_ref, sf_ref, gf_ref, sb_ref, gb_ref, d_ref, y_ref):
    u = u_ref[0]
    y = (_dot(u.astype(BF16), t_ref[0]) + _dot(sf_ref[0].astype(BF16), gf_ref[0])
         + _dot(sb_ref[0].astype(BF16), gb_ref[0]) + d_ref[0] * u.astype(F32))
    gelu = 0.5 * y * (1.0 + jnp.tanh(math.sqrt(2.0 / math.pi) * (y + 0.044715 * (y * y * y))))
    y_ref[0] = gelu.astype(y_ref.dtype)


def _s5_scan(p_s5, n_ctx, lam_re, lam_im, log_dt, b_re, b_im, c_re, c_im, d_skip):
    bsz, t, width = p_s5.shape
    g = width // S5_CH
    c = S5_CHUNK
    nc, ncc = t // c, n_ctx // c
    blk = c * S5_CH
    rows = bsz * nc
    tmat, hmat, gf, gb, ac_re, ac_im = _s5_operators(lam_re, lam_im, log_dt, b_re, b_im, c_re, c_im)
    u = jnp.transpose(p_s5.astype(F32).reshape(bsz, nc, c, g, S5_CH), (3, 0, 1, 2, 4)).reshape(g, rows, blk)
    per_g = lambda gi: (gi, 0, 0)
    cparams = pltpu.CompilerParams(dimension_semantics=("arbitrary",), vmem_limit_bytes=VMEM_LIMIT)
    loc = pl.pallas_call(
        _s5_local_kernel, grid=(g,),
        in_specs=[pl.BlockSpec((1, rows, blk), per_g), pl.BlockSpec((1, blk, 256), per_g)],
        out_specs=pl.BlockSpec((1, rows, 256), per_g),
        out_shape=jax.ShapeDtypeStruct((g, rows, 256), F32),
        compiler_params=cparams, name="s5_local",
    )(u, hmat)
    per_g4 = lambda gi: (gi, 0, 0, 0)
    st_shape = jax.ShapeDtypeStruct((g, bsz, nc, 256), F32)
    sf, sb = pl.pallas_call(
        functools.partial(_s5_state_kernel, nc=nc, ncc=ncc, bsz=bsz), grid=(g,),
        in_specs=[pl.BlockSpec((1, bsz, nc, 256), per_g4), pl.BlockSpec((1, 1, 128), per_g),
                  pl.BlockSpec((1, 1, 128), per_g)],
        out_specs=[pl.BlockSpec((1, bsz, nc, 256), per_g4)] * 2,
        out_shape=[st_shape, st_shape],
        compiler_params=cparams, name="s5_state",
    )(loc.reshape(g, bsz, nc, 256), ac_re, ac_im)
    d_flat = jnp.tile(d_skip.astype(F32).reshape(g, 1, S5_CH), (1, c, 1)).reshape(g, 1, blk)
    y = pl.pallas_call(
        _s5_out_kernel, grid=(g,),
        in_specs=[pl.BlockSpec((1, rows, blk), per_g), pl.BlockSpec((1, blk, blk), per_g),
                  pl.BlockSpec((1, rows, 256), per_g), pl.BlockSpec((1, 256, blk), per_g),
                  pl.BlockSpec((1, rows, 256), per_g), pl.BlockSpec((1, 256, blk), per_g),
                  pl.BlockSpec((1, 1, blk), per_g)],
        out_specs=pl.BlockSpec((1, rows, blk), per_g),
        out_shape=jax.ShapeDtypeStruct((g, rows, blk), F32),
        compiler_params=cparams, name="s5_out",
    )(u, tmat, sf.reshape(g, rows, 256), gf, sb.reshape(g, rows, 256), gb, d_flat)
    return jnp.transpose(y.reshape(g, bsz, nc, c, S5_CH), (1, 2, 3, 0, 4)).reshape(bsz, t, width)


IN_BM = 1280
IN_BN = 768
OUT_BM = 256
ROUTER_PAD = 128


def _row_block(t, target):
    bm = target
    while t % bm:
        bm -= CH
    return bm


def _norm_mod(x, mod_ref, first_row, n_ctx):
    xn = x * lax.rsqrt(jnp.mean(x * x, axis=-1, keepdims=True) + NORM_EPS)
    is_ctx = (first_row + lax.broadcasted_iota(jnp.int32, (x.shape[0], 1), 0)) < n_ctx
    scale = jnp.where(is_ctx, mod_ref[0, 0, 0:1, :], mod_ref[0, 1, 0:1, :])
    shift = jnp.where(is_ctx, mod_ref[0, 0, 1:2, :], mod_ref[0, 1, 1:2, :])
    return xn * scale + shift


def _in_proj_kernel(x_ref, mod_ref, w_ref, p_ref, h_ref, *, n_ctx, bm):
    @pl.when(pl.program_id(2) == 0)
    def _():
        sub = _row_block(bm, 256)

        def body(k, carry):
            rows = pl.ds(pl.multiple_of(k * sub, sub), sub)
            h = _norm_mod(x_ref[0, rows, :], mod_ref, pl.program_id(1) * bm + k * sub, n_ctx)
            h_ref[rows, :] = h.astype(BF16)
            return carry

        lax.fori_loop(0, bm // sub, body, 0)

    p_ref[0] = _dot(h_ref[...], w_ref[...]).astype(p_ref.dtype)


def _in_proj(tok, mod, w, n_ctx):
    bsz, t, d = tok.shape
    n = w.shape[1]
    bm = _row_block(t, IN_BM)
    return pl.pallas_call(
        functools.partial(_in_proj_kernel, n_ctx=n_ctx, bm=bm),
        grid=(bsz, t // bm, n // IN_BN),
        in_specs=[pl.BlockSpec((1, bm, d), lambda b, i, j: (b, i, 0), pipeline_mode=pl.Buffered(1)),
                  pl.BlockSpec((1, 2, 2, d), lambda b, i, j: (b, 0, 0, 0)),
                  pl.BlockSpec((d, IN_BN), lambda b, i, j: (0, j))],
        out_specs=pl.BlockSpec((1, bm, IN_BN), lambda b, i, j: (b, i, j)),
        out_shape=jax.ShapeDtypeStruct((bsz, t, n), F32),
        scratch_shapes=[pltpu.VMEM((bm, d), BF16)],
        compiler_params=pltpu.CompilerParams(
            dimension_semantics=("arbitrary", "arbitrary", "arbitrary"), vmem_limit_bytes=VMEM_LIMIT),
        name="in_proj",
    )(tok, mod, w)


def _out_proj_kernel(s5_ref, wglu_ref, bglu_ref,
                     y2_ref, r_ref, kd_ref, v_ref, g_ref, rk_ref, lnw_ref, lnb_ref,
                     o2_ref, og_ref, hgn_ref, bd_ref,
                     x_ref, gate_ref, wout_ref, modf_ref, wr_ref, br_ref,
                     xo_ref, h_ref, lg_ref, *, n_ctx, bm):
    first = pl.program_id(1) * bm
    bd = bd_ref[...]
    inv_n = 1.0 / HEAD

    def head_sum(z):
        return _dot(z.astype(BF16), bd)

    ys = s5_ref[0].astype(F32)
    ys = ys * _sigmoid(_dot(ys.astype(BF16), wglu_ref[...]) + bglu_ref[...])
    y = y2_ref[0, 0].astype(F32) + y2_ref[1, 0].astype(F32)
    yc = y - head_sum(y) * inv_n
    var = head_sum(yc * yc) * inv_n
    k_bonus = 0.5 * (kd_ref[0, 0].astype(F32) + kd_ref[1, 0].astype(F32))
    bonus = head_sum(r_ref[0].astype(F32) * k_bonus * rk_ref[...]) * v_ref[0].astype(F32)
    yr = (yc * lax.rsqrt(var + RW_GN_EPS) * lnw_ref[...] + lnb_ref[...] + bonus) * g_ref[0].astype(F32)
    o = o2_ref[0, 0].astype(F32) + o2_ref[1, 0].astype(F32)
    yh = o * lax.rsqrt(head_sum(o * o) * inv_n + NORM_EPS) * hgn_ref[...] * _sigmoid(og_ref[0].astype(F32))

    m = _dot(jnp.concatenate([ys, yr, yh], axis=-1).astype(BF16), wout_ref[...])
    is_ctx = (first + lax.broadcasted_iota(jnp.int32, (bm, 1), 0)) < n_ctx
    x_new = x_ref[0] + jnp.where(is_ctx, gate_ref[0, 0:1, :], gate_ref[0, 1:2, :]) * m
    xo_ref[0] = x_new
    h = _norm_mod(x_new, modf_ref, first, n_ctx).astype(BF16)
    h_ref[0] = h
    lg_ref[0] = _dot(h, wr_ref[...]) + br_ref[...]


def _out_proj(s5y, w_glu, b_glu, y2, r, kd, v, g, r_k, ln_w, ln_b, o2, p, og_col, hg_norm,
              tok, gate, w_out, mod_f, w_router, b_router, n_ctx):
    bsz, t, d = tok.shape
    bm = _row_block(t, OUT_BM)
    s5w = s5y.shape[-1]
    w = r.shape[-1]
    row = lambda b, i: (b, i, 0)
    row2 = lambda b, i: (0, b, i, 0)
    c2 = lambda b, i: (0, 0)
    vec = lambda z: z.reshape(1, -1).astype(F32)
    spec_w = pl.BlockSpec((1, bm, w), row)
    spec_2w = pl.BlockSpec((2, 1, bm, w), row2)
    spec_vw = pl.BlockSpec((1, w), c2)
    return pl.pallas_call(
        functools.partial(_out_proj_kernel, n_ctx=n_ctx, bm=bm),
        grid=(bsz, t // bm),
        in_specs=[
            pl.BlockSpec((1, bm, s5w), row), pl.BlockSpec((s5w, s5w), c2), pl.BlockSpec((1, s5w), c2),
            spec_2w, spec_w, spec_2w, spec_w, spec_w, spec_vw, spec_vw, spec_vw,
            spec_2w, pl.BlockSpec((1, bm, w), lambda b, i: (b, i, og_col)), spec_vw, pl.BlockSpec((w, w), c2),
            pl.BlockSpec((1, bm, d), row), pl.BlockSpec((1, 2, d), lambda b, i: (b, 0, 0)),
            pl.BlockSpec(w_out.shape, c2), pl.BlockSpec((1, 2, 2, d), lambda b, i: (b, 0, 0, 0)),
            pl.BlockSpec((d, ROUTER_PAD), c2), pl.BlockSpec((1, ROUTER_PAD), c2),
        ],
        out_specs=[pl.BlockSpec((1, bm, d), row), pl.BlockSpec((1, bm, d), row),
                   pl.BlockSpec((1, bm, ROUTER_PAD), row)],
        out_shape=[jax.ShapeDtypeStruct((bsz, t, d), F32), jax.ShapeDtypeStruct((bsz, t, d), BF16),
                   jax.ShapeDtypeStruct((bsz, t, ROUTER_PAD), F32)],
        compiler_params=pltpu.CompilerParams(
            dimension_semantics=("arbitrary", "arbitrary"), vmem_limit_bytes=VMEM_LIMIT),
        name="out_proj",
    )(s5y, w_glu.astype(BF16), vec(b_glu), y2, r, kd, v, g, vec(r_k), vec(ln_w), vec(ln_b),
      o2, p, vec(hg_norm), _head_block_diag(w), tok, gate, w_out.astype(BF16), mod_f,
      w_router.astype(BF16), b_router)


def _final_norm_kernel(x_ref, g_ref, o_ref):
    x = x_ref[0]
    o_ref[0] = x * lax.rsqrt(jnp.mean(x * x, axis=-1, keepdims=True) + NORM_EPS) * g_ref[...]


def _final_norm(x, g):
    bsz, t, d = x.shape
    bm = _row_block(t, 512)
    return pl.pallas_call(
        _final_norm_kernel, grid=(bsz, t // bm),
        in_specs=[pl.BlockSpec((1, bm, d), lambda b, i: (b, i, 0)), pl.BlockSpec((1, d), lambda b, i: (0, 0))],
        out_specs=pl.BlockSpec((1, bm, d), lambda b, i: (b, i, 0)),
        out_shape=jax.ShapeDtypeStruct((bsz, t, d), F32),
        compiler_params=pltpu.CompilerParams(dimension_semantics=("arbitrary", "arbitrary")),
        name="final_norm",
    )(x, g.reshape(1, d).astype(F32))


def _moe_kernel(meta_ref, x_ref, w1_ref, w3_ref, w2_ref, y_ref, *, n_blocks):
    active = pl.program_id(0) < meta_ref[n_blocks]

    @pl.when(active)
    def _():
        x = x_ref[...]
        a = _dot(x, w1_ref[0])
        b = _dot(x, w3_ref[0])
        y_ref[...] = _dot((a * _sigmoid(a) * b).astype(BF16), w2_ref[0]).astype(y_ref.dtype)

    @pl.when(jnp.logical_not(active))
    def _():
        y_ref[...] = jnp.zeros_like(y_ref)


def _moe_experts(xg, meta, w1, w3, w2):
    n_rows, d = xg.shape
    n_blocks = n_rows // MOE_BLOCK
    de = w1.shape[2]
    return pl.pallas_call(
        functools.partial(_moe_kernel, n_blocks=n_blocks),
        grid_spec=pltpu.PrefetchScalarGridSpec(
            num_scalar_prefetch=1, grid=(n_blocks,),
            in_specs=[pl.BlockSpec((MOE_BLOCK, d), lambda i, m: (i, 0)),
                      pl.BlockSpec((1, d, de), lambda i, m: (m[i], 0, 0)),
                      pl.BlockSpec((1, d, de), lambda i, m: (m[i], 0, 0)),
                      pl.BlockSpec((1, de, d), lambda i, m: (m[i], 0, 0))],
            out_specs=pl.BlockSpec((MOE_BLOCK, d), lambda i, m: (i, 0))),
        out_shape=jax.ShapeDtypeStruct((n_rows, d), F32),
        compiler_params=pltpu.CompilerParams(dimension_semantics=("arbitrary",), vmem_limit_bytes=VMEM_LIMIT),
        name="moe_experts",
    )(meta, xg, w1, w3, w2)


def _moe_ffn(h, logits, n_groups, n_experts, w1, w3, w2):
    n, d = h.shape
    epg = n_experts // n_groups
    group_logits = logits[:, :n_groups]
    g_sel = jnp.argmax(group_logits, axis=-1)
    g_gate = jnp.take_along_axis(jax.nn.softmax(group_logits, axis=-1), g_sel[:, None], axis=1)
    exp_logits = logits[:, n_groups:n_groups + n_experts].reshape(n, n_groups, epg)
    within = jnp.take_along_axis(exp_logits, g_sel[:, None, None], axis=1)[:, 0]
    top_val, top_idx = lax.top_k(within, TOP_K)
    gate = jax.nn.softmax(top_val, axis=-1) * g_gate
    e_flat = (g_sel[:, None] * epg + top_idx).reshape(-1).astype(jnp.int32)
    n_assign = n * TOP_K
    order = jnp.argsort(e_flat)
    rank = jnp.argsort(order).astype(jnp.int32)
    counts = jnp.sum(e_flat[:, None] == jnp.arange(n_experts, dtype=jnp.int32)[None, :], axis=0, dtype=jnp.int32)
    starts = jnp.cumsum(counts) - counts
    padded = (counts + MOE_BLOCK - 1) // MOE_BLOCK * MOE_BLOCK
    pad_ends = jnp.cumsum(padded)
    pad_starts = pad_ends - padded
    n_blocks = -(-n_assign // MOE_BLOCK) + n_experts
    n_rows = n_blocks * MOE_BLOCK
    block_start = jnp.arange(n_blocks, dtype=jnp.int32) * MOE_BLOCK
    block_exp = jnp.minimum(jnp.sum(pad_ends[None, :] <= block_start[:, None], axis=1, dtype=jnp.int32), n_experts - 1)
    rows = jnp.arange(n_rows, dtype=jnp.int32)
    row_exp = jnp.repeat(block_exp, MOE_BLOCK)
    offset = rows - pad_starts[row_exp]
    src = jnp.clip(starts[row_exp] + offset, 0, n_assign - 1)
    row_tok = jnp.where(offset < counts[row_exp], order[src].astype(jnp.int32) // TOP_K, 0)
    meta = jnp.concatenate([block_exp, (pad_ends[-1:] // MOE_BLOCK).astype(jnp.int32)])
    y_rows = _moe_experts(h[row_tok], meta, w1, w3, w2)
    pos = (pad_starts[e_flat] + rank - starts[e_flat]).reshape(n, TOP_K)
    return gate[:, 0:1] * y_rows[pos[:, 0]] + gate[:, 1:2] * y_rows[pos[:, 1]]


def kernel(x, c, ctx, c_ctx, ada_w, ada_b, norm_mix, norm_ffn, w_in, w_out, s5_lam_re, s5_lam_im, s5_log_dt, s5_b_re, s5_b_im, s5_c_re, s5_c_im, s5_d, s5_w_glu, s5_b_glu, rw_mu, rw_w0, rw_w2, rw_a0, rw_a2, rw_g2, rw_k_k, rw_k_a, rw_r_k, rw_ln_w, rw_ln_b, hg_lb_logits, hg_norm, moe_w_group, moe_b_group, moe_w_expert, moe_b_expert, moe_w1, moe_w3, moe_w2, final_norm):
    bsz, n_lat, d = x.shape
    n_ctx = ctx.shape[1]
    depth = w_in.shape[0]
    s5_width = s5_d.shape[1]
    rw_width = rw_k_k.shape[1]
    rw_in = rw_mu.shape[2]
    hg_width = hg_norm.shape[1]
    n_groups = moe_w_group.shape[2]
    n_experts = moe_w_expert.shape[2]
    t = n_ctx + n_lat
    hg_col0 = -(-rw_in // hg_width)
    s5_off = (hg_col0 + 5) * hg_width
    n_proj = -(-(s5_off + s5_width) // IN_BN) * IN_BN
    pad1 = hg_col0 * hg_width - rw_in
    pad2 = n_proj - s5_off - s5_width

    silu_c = jax.nn.silu(c)
    silu_cc = jax.nn.silu(c_ctx)
    lb_all = jnp.cumsum(jax.nn.softmax(hg_lb_logits.astype(F32), axis=0), axis=0)
    lb_all = lb_all - lb_all[0:1]
    tok = jnp.concatenate([ctx, x], axis=1)
    for l in range(depth):
        last = l == depth - 1
        mod_lat = jnp.split(silu_c @ ada_w[l] + ada_b[l], 6, axis=-1)
        mod_ctx = [jnp.broadcast_to(z, (bsz, d)) for z in jnp.split(silu_cc @ ada_w[l] + ada_b[l], 6, axis=-1)]
        sh_m, sc_m, gt_m, sh_f, sc_f, gt_f = [jnp.stack([zc, zl], axis=1) for zc, zl in zip(mod_ctx, mod_lat)]
        mod_m = jnp.stack([norm_mix[l] * (1.0 + sc_m), sh_m], axis=2)
        mod_f = jnp.stack([norm_ffn[l] * (1.0 + sc_f), sh_f], axis=2)

        w_cols = [w_in[l][:, s5_width:s5_width + rw_in], jnp.zeros((d, pad1), F32),
                  w_in[l][:, s5_width + rw_in:], w_in[l][:, :s5_width], jnp.zeros((d, pad2), F32)]
        p = _in_proj(tok, mod_m, jnp.concatenate(w_cols, axis=1).astype(BF16), n_ctx)

        r, v, kk, g, lw, kd, bb = _rw_prep(p, n_ctx, rw_mu[l], rw_w0[l], rw_w2[l], rw_a0[l], rw_a2[l],
                                           rw_g2[l], rw_k_k[l], rw_k_a[l])
        y2, w1_b, w3_b = _rw_scan(r, lw, kd, v, kk, bb, n_ctx, casts=((moe_w1, l), (moe_w3, l)))
        o2, w2_b = _hg_scan(p, lb_all[l], n_ctx, col0=hg_col0, casts=((moe_w2, l),))
        experts = (w1_b.reshape(moe_w1.shape[1:]), w3_b.reshape(moe_w3.shape[1:]), w2_b.reshape(moe_w2.shape[1:]))
        s5y = _s5_scan(p[:, :, s5_off:s5_off + s5_width], n_ctx, s5_lam_re[l], s5_lam_im[l], s5_log_dt[l],
                       s5_b_re[l], s5_b_im[l], s5_c_re[l], s5_c_im[l], s5_d[l])

        w_router = jnp.concatenate([moe_w_group[l], moe_w_expert[l],
                                    jnp.zeros((d, ROUTER_PAD - n_groups - n_experts), F32)], axis=1)
        b_router = jnp.concatenate([moe_b_group[l], moe_b_expert[l],
                                    jnp.zeros((ROUTER_PAD - n_groups - n_experts,), F32)]).reshape(1, ROUTER_PAD)
        tok, h_ffn, logits = _out_proj(
            s5y, s5_w_glu[l], s5_b_glu[l], y2, r, kd, v, g, rw_r_k[l], rw_ln_w[l], rw_ln_b[l],
            o2, p, hg_col0 + 4, hg_norm[l], tok, gt_m, w_out[l], mod_f, w_router, b_router, n_ctx)

        if last:
            f = _moe_ffn(h_ffn[:, n_ctx:].reshape(bsz * n_lat, d), logits[:, n_ctx:].reshape(bsz * n_lat, ROUTER_PAD),
                         n_groups, n_experts, *experts).reshape(bsz, n_lat, d)
            x = tok[:, n_ctx:] + gt_f[:, 1:2] * f
        else:
            f = _moe_ffn(h_ffn.reshape(bsz * t, d), logits.reshape(bsz * t, ROUTER_PAD),
                         n_groups, n_experts, *experts).reshape(bsz, t, d)
            tok = jnp.concatenate([tok[:, :n_ctx] + gt_f[:, 0:1] * f[:, :n_ctx],
                                   tok[:, n_ctx:] + gt_f[:, 1:2] * f[:, n_ctx:]], axis=1)
    return _final_norm(x, final_norm)
```

```python
import functools
import math

import numpy as np
import jax
import jax.numpy as jnp
from jax import lax
from jax.experimental import pallas as pl
from jax.experimental.pallas import tpu as pltpu

F32 = jnp.float32
BF16 = jnp.bfloat16

GRID_W = 64
CH = GRID_W
HEAD = 64
S5_CH = 16
S5_STATE = 64
S5_CHUNK = 32
RW_LORA = 64
RW_GN_EPS = 64e-5
NORM_EPS = 1e-6
TOP_K = 2
MOE_BLOCK = 256
VMEM_LIMIT = 48 * 1024 * 1024


def _dot(a, b):
    return jnp.dot(a, b, preferred_element_type=F32)


def _dot_nt(a, b):
    return lax.dot_general(a, b, (((1,), (1,)), ((), ())), preferred_element_type=F32)


def _dot_tn(a, b):
    return lax.dot_general(a, b, (((0,), (0,)), ((), ())), preferred_element_type=F32)


def _split3(x):
    hi = x.astype(BF16)
    r1 = x - hi.astype(F32)
    mid = r1.astype(BF16)
    lo = (r1 - mid.astype(F32)).astype(BF16)
    return hi, mid, lo


def _dot_sel(m01, x):
    hi, mid, lo = _split3(x)
    return _dot(m01, hi) + _dot(m01, mid) + _dot(m01, lo)


def _dot_hp(a, b):
    a_hi = a.astype(BF16)
    a_lo = (a - a_hi.astype(F32)).astype(BF16)
    b_hi = b.astype(BF16)
    b_lo = (b - b_hi.astype(F32)).astype(BF16)
    return _dot(a_hi, b_hi) + _dot(a_hi, b_lo) + _dot(a_lo, b_hi)


def _sigmoid(x):
    return 1.0 / (1.0 + jnp.exp(-x))


def _scan_chunk(d, i, ncc, nc):
    back = jnp.where(i < ncc, ncc - 1 - i, nc + ncc - 1 - i)
    return jnp.where(d == 0, i, back)


def _half_block_masks(row, col, sgn):
    masks = []
    n = 2
    while n <= CH:
        same = (row // n) == (col // n)
        t_hi = ((row % n) >= n // 2).astype(jnp.int32)
        s_hi = ((col % n) >= n // 2).astype(jnp.int32)
        masks.append(jnp.logical_and(same, (t_hi - s_hi) * sgn == 1))
        n *= 2
    return masks


def _cast_plan(w, layer, n_steps):
    _, e, r, c = w.shape
    total = e * r
    rows = 16
    while rows * n_steps < total or total % rows:
        rows *= 2
    last = total // rows - 1
    slab = lambda d, i: jnp.minimum(d * (n_steps // 2) + i, last)
    return (w.reshape(w.shape[0], total, c),
            pl.BlockSpec((1, rows, c), lambda d, i: (layer, slab(d, i), 0)),
            pl.BlockSpec((rows, c), lambda d, i: (slab(d, i), 0)),
            jax.ShapeDtypeStruct((total, c), BF16))


def _cast_slabs(in_refs, out_refs):
    for wi, wo in zip(in_refs, out_refs):
        wo[...] = wi[0].astype(wo.dtype)


def _head_block_diag(width):
    idx = np.arange(width) // HEAD
    return jnp.asarray(idx[:, None] == idx[None, :], dtype=BF16)


def _rw_prep_kernel(cur_ref, prev_ref, next_ref, mu_ref, w0_ref, w2_ref, a0_ref, a2_ref, g2_ref,
                    kkw_ref, ka_ref, bd_ref,
                    r_out, v_out, kk_out, g_out, lw_out, kd_out, bb_out, *, ncc, nc, width):
    n = pl.program_id(1)
    cur = cur_ref[0].astype(F32)
    prv = prev_ref[0].astype(F32)
    nxt = next_ref[0].astype(F32)
    is_ctx = n < ncc
    row = lax.broadcasted_iota(jnp.int32, (CH, 1), 0)
    zero_row = jnp.zeros((1, cur.shape[1]), F32)
    first_fill = jnp.where(jnp.logical_and(is_ctx, n > 0), prv[CH - 1:CH, :], zero_row)
    last_fill = jnp.where(jnp.logical_and(is_ctx, n < ncc - 1), nxt[0:1, :], zero_row)
    nb_prev = jnp.where(row == 0, first_fill, pltpu.roll(cur, 1, 0))
    nb_next = jnp.where(row == CH - 1, last_fill, pltpu.roll(cur, CH - 1, 0))
    mu = mu_ref[...]
    zero_mu = jnp.zeros_like(mu[0:1])
    w_up = jnp.where(jnp.logical_and(jnp.logical_not(is_ctx), n > ncc), mu[2:3], zero_mu)
    w_down = jnp.where(jnp.logical_and(jnp.logical_not(is_ctx), n < nc - 1), mu[3:4], zero_mu)
    w_cur = 1.0 - mu[0:1] - mu[1:2] - jnp.where(is_ctx, zero_mu, mu[2:3] + mu[3:4])
    x = w_cur * cur + mu[0:1] * nb_prev + mu[1:2] * nb_next + w_up * prv + w_down * nxt

    w = width
    r = x[:, 0:w]
    k = x[:, w:2 * w]
    v = x[:, 2 * w:3 * w]
    wd = x[:, 3 * w:3 * w + 2 * RW_LORA]
    ad = x[:, 3 * w + 2 * RW_LORA:3 * w + 4 * RW_LORA]
    gd = x[:, 3 * w + 4 * RW_LORA:]

    r_out[0] = r.astype(r_out.dtype)
    v_out[0] = v.astype(v_out.dtype)
    g_out[0] = _dot_hp(_sigmoid(gd), g2_ref[...]).astype(g_out.dtype)
    kk = k * kkw_ref[...]
    kk2 = kk * kk
    kk2_hi = kk2.astype(BF16)
    kk2_lo = (kk2 - kk2_hi.astype(F32)).astype(BF16)
    ss = _dot(kk2_hi, bd_ref[...]) + _dot(kk2_lo, bd_ref[...])
    kk = kk / jnp.maximum(jnp.sqrt(ss), 1e-12)
    kk_out[0] = kk.astype(kk_out.dtype)
    ka = ka_ref[...]
    for d in range(2):
        w_raw = w0_ref[d:d + 1, :] + _dot_hp(jnp.tanh(wd[:, d * RW_LORA:(d + 1) * RW_LORA]), w2_ref[d])
        nw = -w_raw
        softplus = jnp.maximum(nw, 0.0) + jnp.log(1.0 + jnp.exp(-jnp.abs(nw)))
        lw_out[d, 0] = -jnp.exp(-softplus - 0.5)
        a = _sigmoid(a0_ref[d:d + 1, :] + _dot_hp(ad[:, d * RW_LORA:(d + 1) * RW_LORA], a2_ref[d]))
        kd_out[d, 0] = (k * (1.0 + (a - 1.0) * ka)).astype(kd_out.dtype)
        bb_out[d, 0] = (kk * a).astype(bb_out.dtype)


def _rw_prep(p_rw, n_ctx, mu, w0, w2, a0, a2, g2, k_k, k_a):
    bsz, t, _ = p_rw.shape
    win = mu.shape[1]
    width = k_k.shape[0]
    nc = t // CH
    ncc = n_ctx // CH
    tok = lambda b, n: (b, n, 0)
    full2 = lambda b, n: (0, 0)
    full3 = lambda b, n: (0, 0, 0)
    blk_in = (1, CH, win)
    out1 = jax.ShapeDtypeStruct((bsz, t, width), BF16)
    out2 = jax.ShapeDtypeStruct((2, bsz, t, width), BF16)
    out_lw = jax.ShapeDtypeStruct((2, bsz, t, width), F32)
    spec1 = pl.BlockSpec((1, CH, width), tok)
    spec2 = pl.BlockSpec((2, 1, CH, width), lambda b, n: (0, b, n, 0))
    return pl.pallas_call(
        functools.partial(_rw_prep_kernel, ncc=ncc, nc=nc, width=width),
        grid=(bsz, nc),
        in_specs=[
            pl.BlockSpec(blk_in, tok),
            pl.BlockSpec(blk_in, lambda b, n: (b, jnp.maximum(n - 1, 0), 0)),
            pl.BlockSpec(blk_in, lambda b, n: (b, jnp.minimum(n + 1, nc - 1), 0)),
            pl.BlockSpec(mu.shape, full2),
            pl.BlockSpec(w0.shape, full2),
            pl.BlockSpec(w2.shape, full3),
            pl.BlockSpec(a0.shape, full2),
            pl.BlockSpec(a2.shape, full3),
            pl.BlockSpec(g2.shape, full2),
            pl.BlockSpec((1, width), full2),
            pl.BlockSpec((1, width), full2),
            pl.BlockSpec((width, width), full2),
        ],
        out_specs=[spec1, spec1, spec1, spec1, spec2, spec2, spec2],
        out_shape=[out1, out1, out1, out1, out_lw, out2, out2],
        compiler_params=pltpu.CompilerParams(
            dimension_semantics=("arbitrary", "arbitrary"), vmem_limit_bytes=VMEM_LIMIT),
        name="rwkv_prep",
    )(p_rw, p_rw, p_rw, mu, w0, w2, a0, a2, g2, k_k.reshape(1, width), k_a.reshape(1, width),
      _head_block_diag(width))


def _rw_scan_kernel(r_ref, lw_ref, kd_ref, v_ref, kk_ref, bb_ref, *rest, heads, bsz, n_cast):
    y_ref, s_ref = rest[n_cast], rest[-1]
    _cast_slabs(rest[:n_cast], rest[n_cast + 1:-1])
    d = pl.program_id(0)
    i = pl.program_id(1)

    @pl.when(i == 0)
    def _():
        s_ref[...] = jnp.zeros_like(s_ref)

    row = lax.broadcasted_iota(jnp.int32, (CH, CH), 0)
    col = lax.broadcasted_iota(jnp.int32, (CH, CH), 1)
    sgn = jnp.where(d == 0, 1, -1)
    strict = (row - col) * sgn > 0
    diag = row == col
    incl = jnp.logical_or(strict, diag)
    tri = jnp.where(incl, 1.0, 0.0).astype(BF16)
    eye = jnp.where(diag, 1.0, 0.0)
    halves = _half_block_masks(row, col, sgn)

    hs = range(bsz * heads)
    sls = [slice(h * HEAD, (h + 1) * HEAD) for h in range(heads)] * bsz
    s0 = [s_ref[h] for h in hs]
    v, kkr, bk, b_w, k_w, wtot = [], [], [], [], [], []
    for b in range(bsz):
        lw = lw_ref[0, b]
        cw = _dot_sel(tri, lw)
        ctot = jnp.sum(lw, axis=0, keepdims=True)
        e_ncw = jnp.exp(-cw)
        e_rem = jnp.exp(ctot - cw)
        r_t = (r_ref[b].astype(F32) * jnp.exp(cw)).astype(BF16)
        kk_t = (kk_ref[b].astype(F32) * jnp.exp(cw - lw)).astype(BF16)
        bb = bb_ref[0, b].astype(F32)
        kd = kd_ref[0, b].astype(F32)
        b_t = (bb * e_ncw).astype(BF16)
        k_t = (kd * e_ncw).astype(BF16)
        bw_all = (bb * e_rem).astype(BF16)
        kw_all = (kd * e_rem).astype(BF16)
        wt_all = jnp.exp(ctot)
        v_all = v_ref[b]
        for sl in sls[:heads]:
            v.append(v_all[:, sl])
            kkr.append(jnp.concatenate([kk_t[:, sl], r_t[:, sl]], axis=0))
            bk.append(jnp.concatenate([b_t[:, sl], k_t[:, sl]], axis=0))
            b_w.append(bw_all[:, sl])
            k_w.append(kw_all[:, sl])
            wtot.append(wt_all[:, sl])
    att = [_dot_nt(kkr[h], bk[h]) for h in hs]
    proj = [_dot_nt(kkr[h], s0[h].astype(BF16)) for h in hs]
    a_ab = [jnp.where(strict, att[h][:CH, :CH], 0.0) for h in hs]
    a_ak = [jnp.where(strict, att[h][:CH, CH:], 0.0).astype(BF16) for h in hs]
    a_rb = [jnp.where(incl, att[h][CH:, :CH], 0.0).astype(BF16) for h in hs]
    a_rk = [jnp.where(incl, att[h][CH:, CH:], 0.0).astype(BF16) for h in hs]
    inv = [eye - jnp.where(halves[0], a_ab[h], 0.0) for h in hs]
    for lvl in range(1, len(halves)):
        inv_b = [inv[h].astype(BF16) for h in hs]
        a_l = [jnp.where(halves[lvl], a_ab[h], 0.0).astype(BF16) for h in hs]
        left = [_dot(inv_b[h], a_l[h]).astype(BF16) for h in hs]
        inv = [inv[h] - _dot(left[h], inv_b[h]) for h in hs]
    rhs = [(proj[h][:CH] + _dot(a_ak[h], v[h])).astype(BF16) for h in hs]
    u_b = [(-_dot(inv[h].astype(BF16), rhs[h])).astype(BF16) for h in hs]
    y = [proj[h][CH:] + _dot(a_rb[h], u_b[h]) + _dot(a_rk[h], v[h]) for h in hs]
    s_new = [s0[h] * wtot[h] + _dot_tn(u_b[h], b_w[h]) + _dot_tn(v[h], k_w[h]) for h in hs]
    for h in hs:
        y_ref[0, h // heads, :, sls[h]] = y[h].astype(y_ref.dtype)
        s_ref[h] = s_new[h]


def _rw_scan(r, lw, kd, v, kk, bb, n_ctx, casts=()):
    bsz, t, width = r.shape
    heads = width // HEAD
    nc = t // CH
    ncc = n_ctx // CH
    s1 = pl.BlockSpec((bsz, CH, width), lambda d, i: (0, _scan_chunk(d, i, ncc, nc), 0))
    s2 = pl.BlockSpec((1, bsz, CH, width), lambda d, i: (d, 0, _scan_chunk(d, i, ncc, nc), 0))
    plans = [_cast_plan(w, layer, 2 * nc) for w, layer in casts]
    return pl.pallas_call(
        functools.partial(_rw_scan_kernel, heads=heads, bsz=bsz, n_cast=len(plans)),
        grid=(2, nc),
        in_specs=[s1, s2, s2, s1, s1, s2] + [p[1] for p in plans],
        out_specs=[s2] + [p[2] for p in plans],
        out_shape=[jax.ShapeDtypeStruct((2, bsz, t, width), BF16)] + [p[3] for p in plans],
        scratch_shapes=[pltpu.VMEM((bsz * heads, HEAD, HEAD), F32)],
        compiler_params=pltpu.CompilerParams(
            dimension_semantics=("arbitrary", "arbitrary"), vmem_limit_bytes=VMEM_LIMIT),
        name="rwkv_scan",
    )(r, lw, kd, v, kk, bb, *[p[0] for p in plans])


HG_LEVELS = 6


def _hg_decay_matrices():
    out = np.zeros((2, HG_LEVELS + 2, CH, CH), np.float32)
    pos = np.arange(CH)
    for d in range(2):
        for t in range(CH):
            if d == 0:
                out[d, 0, t, :t + 1] = 1.0
                out[d, 1, t, t + 1:] = 1.0
            else:
                out[d, 0, t, t:] = 1.0
                out[d, 1, t, :t] = 1.0
            for l in range(1, HG_LEVELS + 1):
                n = 1 << l
                mid = (t // n) * n + n // 2
                if d == 0:
                    sel = (pos >= mid) & (pos <= t) if t >= mid else (pos > t) & (pos < mid)
                else:
                    sel = (pos >= t) & (pos < mid) if t < mid else (pos >= mid) & (pos < t)
                out[d, 1 + l, t, sel] = 1.0
    return jnp.asarray(out.reshape(2, (HG_LEVELS + 2) * CH, CH), dtype=BF16)


def _hg_scan_kernel(q_ref, f_ref, i_ref, lb_ref, dm_ref, *rest, heads, bsz, n_cast):
    o_ref, s_ref = rest[n_cast], rest[-1]
    _cast_slabs(rest[:n_cast], rest[n_cast + 1:-1])
    d = pl.program_id(0)
    step = pl.program_id(1)

    @pl.when(step == 0)
    def _():
        s_ref[...] = jnp.zeros_like(s_ref)

    row = lax.broadcasted_iota(jnp.int32, (CH, CH), 0)
    col = lax.broadcasted_iota(jnp.int32, (CH, CH), 1)
    sgn = jnp.where(d == 0, 1, -1)

    lb = lb_ref[...]
    masks = _half_block_masks(row, col, sgn)
    lvl_masks = [row == col] + masks
    hs = range(bsz * heads)
    sls = [slice(h * HEAD, (h + 1) * HEAD) for h in range(heads)] * bsz
    q_lvl, k_lvl, q_in, k_out, etot, v_all = [], [], [], [], [], []
    for b in range(bsz):
        f = f_ref[b].astype(F32)
        q = q_ref[b].astype(F32)
        q = q * _sigmoid(q)
        log_sig = jnp.minimum(f, 0.0) - jnp.log(1.0 + jnp.exp(-jnp.abs(f)))
        hi_term = jnp.log(1.0 - lb) + log_sig
        lo_term = jnp.log(jnp.where(lb > 0.0, lb, 1.0))
        mx = jnp.maximum(hi_term, lo_term)
        lf_mix = mx + jnp.log(1.0 + jnp.exp(-jnp.abs(hi_term - lo_term)))
        lf = jnp.where(lb > 0.0, lf_mix, hi_term)
        key = (1.0 - lb) * _sigmoid(-f)
        ex = jnp.exp(_dot_sel(dm_ref[0], lf))
        q_in.append((q * ex[0:CH]).astype(BF16))
        k_out.append((key * ex[CH:2 * CH]).astype(BF16))
        etot.append(jnp.exp(jnp.sum(lf, axis=0, keepdims=True)))
        v_all.append(i_ref[b].astype(BF16))
        q_lvl.append([q.astype(BF16)] + [(q * ex[(1 + l) * CH:(2 + l) * CH]).astype(BF16)
                                         for l in range(1, HG_LEVELS + 1)])
        k_lvl.append([key.astype(BF16)] + [(key * ex[(1 + l) * CH:(2 + l) * CH]).astype(BF16)
                                           for l in range(1, HG_LEVELS + 1)])

    bi = [h // heads for h in hs]
    s0 = [s_ref[h] for h in hs]
    sc = [jnp.zeros((CH, CH), F32) for _ in hs]
    for l in range(HG_LEVELS + 1):
        sc = [sc[h] + jnp.where(lvl_masks[l], _dot_nt(q_lvl[bi[h]][l][:, sls[h]], k_lvl[bi[h]][l][:, sls[h]]), 0.0)
              for h in hs]
    o = [_dot(sc[h].astype(BF16), v_all[bi[h]][:, sls[h]]) + _dot_nt(q_in[bi[h]][:, sls[h]], s0[h].astype(BF16))
         for h in hs]
    s_new = [s0[h] * etot[bi[h]][:, sls[h]] + _dot_tn(v_all[bi[h]][:, sls[h]], k_out[bi[h]][:, sls[h]]) for h in hs]
    for h in hs:
        o_ref[0, bi[h], :, sls[h]] = o[h].astype(o_ref.dtype)
        s_ref[h] = s_new[h]


def _hg_scan(p_hg, lb, n_ctx, col0=0, casts=()):
    bsz, t, _ = p_hg.shape
    width = lb.shape[0]
    heads = width // HEAD
    nc = t // CH
    ncc = n_ctx // CH
    dm = _hg_decay_matrices()
    chunk = lambda d, i: _scan_chunk(d, i, ncc, nc)
    blk = (bsz, CH, width)
    plans = [_cast_plan(w, layer, 2 * nc) for w, layer in casts]
    return pl.pallas_call(
        functools.partial(_hg_scan_kernel, heads=heads, bsz=bsz, n_cast=len(plans)),
        grid=(2, nc),
        in_specs=[
            pl.BlockSpec(blk, lambda d, i: (0, chunk(d, i), col0)),
            pl.BlockSpec(blk, lambda d, i: (0, chunk(d, i), col0 + 1 + d)),
            pl.BlockSpec(blk, lambda d, i: (0, chunk(d, i), col0 + 3)),
            pl.BlockSpec((1, width), lambda d, i: (0, 0)),
            pl.BlockSpec((1,) + dm.shape[1:], lambda d, i: (d, 0, 0)),
        ] + [p[1] for p in plans],
        out_specs=[pl.BlockSpec((1, bsz, CH, width), lambda d, i: (d, 0, chunk(d, i), 0))] + [p[2] for p in plans],
        out_shape=[jax.ShapeDtypeStruct((2, bsz, t, width), BF16)] + [p[3] for p in plans],
        scratch_shapes=[pltpu.VMEM((bsz * heads, HEAD, HEAD), F32)],
        compiler_params=pltpu.CompilerParams(
            dimension_semantics=("arbitrary", "arbitrary"), vmem_limit_bytes=VMEM_LIMIT),
        name="hgrn2_scan",
    )(p_hg, p_hg, p_hg, lb.reshape(1, width), dm, *[p[0] for p in plans])


def _s5_operators(lam_re, lam_im, log_dt, b_re, b_im, c_re, c_im):
    c = S5_CHUNK
    hp = lax.Precision.HIGHEST
    dt = jnp.exp(log_dt.astype(F32))[..., None]
    lam_re = lam_re.astype(F32)
    lam_im = lam_im.astype(F32)
    mag = jnp.exp(lam_re * dt)
    ang = lam_im * dt
    a_re, a_im = mag * jnp.cos(ang), mag * jnp.sin(ang)
    den = lam_re * lam_re + lam_im * lam_im
    f_re = ((a_re - 1.0) * lam_re + a_im * lam_im) / den
    f_im = (a_im * lam_re - (a_re - 1.0) * lam_im) / den
    bb_re = f_re[..., None] * b_re - f_im[..., None] * b_im
    bb_im = f_re[..., None] * b_im + f_im[..., None] * b_re
    j = jnp.arange(c + 1, dtype=F32)[None, None, :, None]
    pmag = jnp.exp((lam_re * dt)[:, :, None, :] * j)
    pang = ang[:, :, None, :] * j
    pw_re, pw_im = pmag * jnp.cos(pang), pmag * jnp.sin(pang)
    ca_re = c_re[:, :, None] * pw_re[:, :, :, None] - c_im[:, :, None] * pw_im[:, :, :, None]
    ca_im = c_re[:, :, None] * pw_im[:, :, :, None] + c_im[:, :, None] * pw_re[:, :, :, None]
    kern = (jnp.einsum('dgjcp,dgpk->dgjck', ca_re, bb_re, precision=hp)
            - jnp.einsum('dgjcp,dgpk->dgjck', ca_im, bb_im, precision=hp))
    g = kern.shape[1]
    by_lag = jnp.concatenate([kern[1][:, c - 1:0:-1], kern[0][:, 0:1] + kern[1][:, 0:1], kern[0][:, 1:c]], axis=1)
    tpos = np.arange(c)
    lag = tpos[None, :] - tpos[:, None]
    onehot = jnp.asarray((lag[None] + c - 1) == np.arange(2 * c - 1)[:, None, None], dtype=F32)
    tmat = jnp.einsum('gjoc,jab->gacbo', by_lag, onehot, precision=hp).reshape(g, c * S5_CH, c * S5_CH)

    def in_to_state(pr, pi, d):
        hr = pr[:, :, None, :] * jnp.swapaxes(bb_re[d], 1, 2)[:, None] - pi[:, :, None, :] * jnp.swapaxes(bb_im[d], 1, 2)[:, None]
        hi = pr[:, :, None, :] * jnp.swapaxes(bb_im[d], 1, 2)[:, None] + pi[:, :, None, :] * jnp.swapaxes(bb_re[d], 1, 2)[:, None]
        return hr.reshape(g, c * S5_CH, -1), hi.reshape(g, c * S5_CH, -1)

    hf_re, hf_im = in_to_state(pw_re[0, :, c - 1::-1][:, :c], pw_im[0, :, c - 1::-1][:, :c], 0)
    hb_re, hb_im = in_to_state(pw_re[1, :, :c], pw_im[1, :, :c], 1)
    hmat = jnp.concatenate([hf_re, hb_re, hf_im, hb_im], axis=-1)

    def state_to_out(cr, ci):
        return (jnp.transpose(cr, (0, 3, 1, 2)).reshape(g, -1, c * S5_CH),
                -jnp.transpose(ci, (0, 3, 1, 2)).reshape(g, -1, c * S5_CH))

    gf_re, gf_im = state_to_out(ca_re[0, :, 1:c + 1], ca_im[0, :, 1:c + 1])
    gb_re, gb_im = state_to_out(ca_re[1, :, c:0:-1], ca_im[1, :, c:0:-1])
    z = jnp.zeros_like(gf_re)
    gf = jnp.concatenate([gf_re, z, gf_im, z], axis=1)
    gb = jnp.concatenate([z, gb_re, z, gb_im], axis=1)
    ac_re = jnp.concatenate([pw_re[0, :, c], pw_re[1, :, c]], axis=-1)[:, None, :]
    ac_im = jnp.concatenate([pw_im[0, :, c], pw_im[1, :, c]], axis=-1)[:, None, :]
    return tmat.astype(BF16), hmat.astype(BF16), gf.astype(BF16), gb.astype(BF16), ac_re, ac_im


def _s5_local_kernel(u_ref, h_ref, loc_ref):
    loc_ref[0] = _dot(u_ref[0].astype(BF16), h_ref[0])


def _s5_state_kernel(loc_ref, ar_ref, ai_ref, sf_ref, sb_ref, *, nc, ncc, bsz):
    ar = ar_ref[0]
    ai = ai_ref[0]
    half = ar.shape[-1]
    is_fwd = lax.broadcasted_iota(jnp.int32, (1, half), 1) < half // 2

    def body(i, carry):
        nb = jnp.where(i < ncc, ncc - 1 - i, nc + ncc - 1 - i)
        new = []
        for b in range(bsz):
            re, im = carry[b]
            st = jnp.concatenate([re, im], axis=1)
            sf_ref[0, b, pl.ds(i, 1), :] = st
            sb_ref[0, b, pl.ds(nb, 1), :] = st
            lf = loc_ref[0, b, pl.ds(i, 1), :]
            lk = loc_ref[0, b, pl.ds(nb, 1), :]
            l_re = jnp.where(is_fwd, lf[:, :half], lk[:, :half])
            l_im = jnp.where(is_fwd, lf[:, half:], lk[:, half:])
            new.append((ar * re - ai * im + l_re, ar * im + ai * re + l_im))
        return tuple(new)

    zero = jnp.zeros((1, half), F32)
    lax.fori_loop(0, nc, body, tuple((zero, zero) for _ in range(bsz)))


def _s5_out_kernel(u_ref, t_ref, sf_ref, gf_ref, sb_ref, gb_ref, d_ref, y_ref):
    u = u_ref[0]
    y = (_dot(u.astype(BF16), t_ref[0]) + _dot(sf_ref[0].astype(BF16), gf_ref[0])
         + _dot(sb_ref[0].astype(BF16), gb_ref[0]) + d_ref[0] * u.astype(F32))
    gelu = 0.5 * y * (1.0 + jnp.tanh(math.sqrt(2.0 / math.pi) * (y + 0.044715 * (y * y * y))))
    y_ref[0] = gelu.astype(y_ref.dtype)


def _s5_scan(p_s5, n_ctx, lam_re, lam_im, log_dt, b_re, b_im, c_re, c_im, d_skip):
    bsz, t, width = p_s5.shape
    g = width // S5_CH
    c = S5_CHUNK
    nc, ncc = t // c, n_ctx // c
    blk = c * S5_CH
    rows = bsz * nc
    tmat, hmat, gf, gb, ac_re, ac_im = _s5_operators(lam_re, lam_im, log_dt, b_re, b_im, c_re, c_im)
    u = jnp.transpose(p_s5.astype(F32).reshape(bsz, nc, c, g, S5_CH), (3, 0, 1, 2, 4)).reshape(g, rows, blk)
    per_g = lambda gi: (gi, 0, 0)
    cparams = pltpu.CompilerParams(dimension_semantics=("arbitrary",), vmem_limit_bytes=VMEM_LIMIT)
    loc = pl.pallas_call(
        _s5_local_kernel, grid=(g,),
        in_specs=[pl.BlockSpec((1, rows, blk), per_g), pl.BlockSpec((1, blk, 256), per_g)],
        out_specs=pl.BlockSpec((1, rows, 256), per_g),
        out_shape=jax.ShapeDtypeStruct((g, rows, 256), F32),
        compiler_params=cparams, name="s5_local",
    )(u, hmat)
    per_g4 = lambda gi: (gi, 0, 0, 0)
    st_shape = jax.ShapeDtypeStruct((g, bsz, nc, 256), F32)
    sf, sb = pl.pallas_call(
        functools.partial(_s5_state_kernel, nc=nc, ncc=ncc, bsz=bsz), grid=(g,),
        in_specs=[pl.BlockSpec((1, bsz, nc, 256), per_g4), pl.BlockSpec((1, 1, 128), per_g),
                  pl.BlockSpec((1, 1, 128), per_g)],
        out_specs=[pl.BlockSpec((1, bsz, nc, 256), per_g4)] * 2,
        out_shape=[st_shape, st_shape],
        compiler_params=cparams, name="s5_state",
    )(loc.reshape(g, bsz, nc, 256), ac_re, ac_im)
    d_flat = jnp.tile(d_skip.astype(F32).reshape(g, 1, S5_CH), (1, c, 1)).reshape(g, 1, blk)
    y = pl.pallas_call(
        _s5_out_kernel, grid=(g,),
        in_specs=[pl.BlockSpec((1, rows, blk), per_g), pl.BlockSpec((1, blk, blk), per_g),
                  pl.BlockSpec((1, rows, 256), per_g), pl.BlockSpec((1, 256, blk), per_g),
                  pl.BlockSpec((1, rows, 256), per_g), pl.BlockSpec((1, 256, blk), per_g),
                  pl.BlockSpec((1, 1, blk), per_g)],
        out_specs=pl.BlockSpec((1, rows, blk), per_g),
        out_shape=jax.ShapeDtypeStruct((g, rows, blk), F32),
        compiler_params=cparams, name="s5_out",
    )(u, tmat, sf.reshape(g, rows, 256), gf, sb.reshape(g, rows, 256), gb, d_flat)
    return jnp.transpose(y.reshape(g, bsz, nc, c, S5_CH), (1, 2, 3, 0, 4)).reshape(bsz, t, width)


IN_BM = 1280
IN_BN = 768
OUT_BM = 256
ROUTER_PAD = 128


def _row_block(t, target):
    bm = target
    while t % bm:
        bm -= CH
    return bm


def _norm_mod(x, mod_ref, first_row, n_ctx):
    xn = x * lax.rsqrt(jnp.mean(x * x, axis=-1, keepdims=True) + NORM_EPS)
    is_ctx = (first_row + lax.broadcasted_iota(jnp.int32, (x.shape[0], 1), 0)) < n_ctx
    scale = jnp.where(is_ctx, mod_ref[0, 0, 0:1, :], mod_ref[0, 1, 0:1, :])
    shift = jnp.where(is_ctx, mod_ref[0, 0, 1:2, :], mod_ref[0, 1, 1:2, :])
    return xn * scale + shift


def _in_proj_kernel(x_ref, mod_ref, w_ref, p_ref, h_ref, *, n_ctx, bm):
    @pl.when(pl.program_id(2) == 0)
    def _():
        sub = _row_block(bm, 256)

        def body(k, carry):
            rows = pl.ds(pl.multiple_of(k * sub, sub), sub)
            h = _norm_mod(x_ref[0, rows, :], mod_ref, pl.program_id(1) * bm + k * sub, n_ctx)
            h_ref[rows, :] = h.astype(BF16)
            return carry

        lax.fori_loop(0, bm // sub, body, 0)

    p_ref[0] = _dot(h_ref[...], w_ref[...]).astype(p_ref.dtype)


def _in_proj(tok, mod, w, n_ctx):
    bsz, t, d = tok.shape
    n = w.shape[1]
    bm = _row_block(t, IN_BM)
    return pl.pallas_call(
        functools.partial(_in_proj_kernel, n_ctx=n_ctx, bm=bm),
        grid=(bsz, t // bm, n // IN_BN),
        in_specs=[pl.BlockSpec((1, bm, d), lambda b, i, j: (b, i, 0), pipeline_mode=pl.Buffered(1)),
                  pl.BlockSpec((1, 2, 2, d), lambda b, i, j: (b, 0, 0, 0)),
                  pl.BlockSpec((d, IN_BN), lambda b, i, j: (0, j))],
        out_specs=pl.BlockSpec((1, bm, IN_BN), lambda b, i, j: (b, i, j)),
        out_shape=jax.ShapeDtypeStruct((bsz, t, n), F32),
        scratch_shapes=[pltpu.VMEM((bm, d), BF16)],
        compiler_params=pltpu.CompilerParams(
            dimension_semantics=("arbitrary", "arbitrary", "arbitrary"), vmem_limit_bytes=VMEM_LIMIT),
        name="in_proj",
    )(tok, mod, w)


def _out_proj_kernel(s5_ref, wglu_ref, bglu_ref,
                     y2_ref, r_ref, kd_ref, v_ref, g_ref, rk_ref, lnw_ref, lnb_ref,
                     o2_ref, og_ref, hgn_ref, bd_ref,
                     x_ref, gate_ref, wout_ref, modf_ref, wr_ref, br_ref,
                     xo_ref, h_ref, lg_ref, *, n_ctx, bm):
    first = pl.program_id(1) * bm
    bd = bd_ref[...]
    inv_n = 1.0 / HEAD

    def head_sum(z):
        return _dot(z.astype(BF16), bd)

    ys = s5_ref[0].astype(F32)
    ys = ys * _sigmoid(_dot(ys.astype(BF16), wglu_ref[...]) + bglu_ref[...])
    y = y2_ref[0, 0].astype(F32) + y2_ref[1, 0].astype(F32)
    yc = y - head_sum(y) * inv_n
    var = head_sum(yc * yc) * inv_n
    k_bonus = 0.5 * (kd_ref[0, 0].astype(F32) + kd_ref[1, 0].astype(F32))
    bonus = head_sum(r_ref[0].astype(F32) * k_bonus * rk_ref[...]) * v_ref[0].astype(F32)
    yr = (yc * lax.rsqrt(var + RW_GN_EPS) * lnw_ref[...] + lnb_ref[...] + bonus) * g_ref[0].astype(F32)
    o = o2_ref[0, 0].astype(F32) + o2_ref[1, 0].astype(F32)
    yh = o * lax.rsqrt(head_sum(o * o) * inv_n + NORM_EPS) * hgn_ref[...] * _sigmoid(og_ref[0].astype(F32))

    m = _dot(jnp.concatenate([ys, yr, yh], axis=-1).astype(BF16), wout_ref[...])
    is_ctx = (first + lax.broadcasted_iota(jnp.int32, (bm, 1), 0)) < n_ctx
    x_new = x_ref[0] + jnp.where(is_ctx, gate_ref[0, 0:1, :], gate_ref[0, 1:2, :]) * m
    xo_ref[0] = x_new
    h = _norm_mod(x_new, modf_ref, first, n_ctx)
    h_ref[0] = h
    lg_ref[0] = _dot(h.astype(BF16), wr_ref[...]) + br_ref[...]


def _out_proj(s5y, w_glu, b_glu, y2, r, kd, v, g, r_k, ln_w, ln_b, o2, p, og_col, hg_norm,
              tok, gate, w_out, mod_f, w_router, b_router, n_ctx):
    bsz, t, d = tok.shape
    bm = _row_block(t, OUT_BM)
    s5w = s5y.shape[-1]
    w = r.shape[-1]
    row = lambda b, i: (b, i, 0)
    row2 = lambda b, i: (0, b, i, 0)
    c2 = lambda b, i: (0, 0)
    vec = lambda z: z.reshape(1, -1).astype(F32)
    spec_w = pl.BlockSpec((1, bm, w), row)
    spec_2w = pl.BlockSpec((2, 1, bm, w), row2)
    spec_vw = pl.BlockSpec((1, w), c2)
    return pl.pallas_call(
        functools.partial(_out_proj_kernel, n_ctx=n_ctx, bm=bm),
        grid=(bsz, t // bm),
        in_specs=[
            pl.BlockSpec((1, bm, s5w), row), pl.BlockSpec((s5w, s5w), c2), pl.BlockSpec((1, s5w), c2),
            spec_2w, spec_w, spec_2w, spec_w, spec_w, spec_vw, spec_vw, spec_vw,
            spec_2w, pl.BlockSpec((1, bm, w), lambda b, i: (b, i, og_col)), spec_vw, pl.BlockSpec((w, w), c2),
            pl.BlockSpec((1, bm, d), row), pl.BlockSpec((1, 2, d), lambda b, i: (b, 0, 0)),
            pl.BlockSpec(w_out.shape, c2), pl.BlockSpec((1, 2, 2, d), lambda b, i: (b, 0, 0, 0)),
            pl.BlockSpec((d, ROUTER_PAD), c2), pl.BlockSpec((1, ROUTER_PAD), c2),
        ],
        out_specs=[pl.BlockSpec((1, bm, d), row), pl.BlockSpec((1, bm, d), row),
                   pl.BlockSpec((1, bm, ROUTER_PAD), row)],
        out_shape=[jax.ShapeDtypeStruct((bsz, t, d), F32), jax.ShapeDtypeStruct((bsz, t, d), F32),
                   jax.ShapeDtypeStruct((bsz, t, ROUTER_PAD), F32)],
        compiler_params=pltpu.CompilerParams(
            dimension_semantics=("arbitrary", "arbitrary"), vmem_limit_bytes=VMEM_LIMIT),
        name="out_proj",
    )(s5y, w_glu.astype(BF16), vec(b_glu), y2, r, kd, v, g, vec(r_k), vec(ln_w), vec(ln_b),
      o2, p, vec(hg_norm), _head_block_diag(w), tok, gate, w_out.astype(BF16), mod_f,
      w_router.astype(BF16), b_router)


def _final_norm_kernel(x_ref, g_ref, o_ref):
    x = x_ref[0]
    o_ref[0] = x * lax.rsqrt(jnp.mean(x * x, axis=-1, keepdims=True) + NORM_EPS) * g_ref[...]


def _final_norm(x, g):
    bsz, t, d = x.shape
    bm = _row_block(t, 512)
    return pl.pallas_call(
        _final_norm_kernel, grid=(bsz, t // bm),
        in_specs=[pl.BlockSpec((1, bm, d), lambda b, i: (b, i, 0)), pl.BlockSpec((1, d), lambda b, i: (0, 0))],
        out_specs=pl.BlockSpec((1, bm, d), lambda b, i: (b, i, 0)),
        out_shape=jax.ShapeDtypeStruct((bsz, t, d), F32),
        compiler_params=pltpu.CompilerParams(dimension_semantics=("arbitrary", "arbitrary")),
        name="final_norm",
    )(x, g.reshape(1, d).astype(F32))


def _moe_kernel(meta_ref, rows_cur, rows_nxt, h_hbm, w1_ref, w3_ref, w2_ref, y_ref, xbuf, sem, *, n_blocks):
    i = pl.program_id(0)
    n_used = meta_ref[n_blocks]
    slot = lax.rem(i, 2)

    def gather(rows_ref, s):
        def body(r, carry):
            tok = rows_ref[0, 0, r]
            pltpu.make_async_copy(h_hbm.at[pl.ds(tok, 1), :], xbuf.at[s, pl.ds(r, 1), :], sem.at[s]).start()
            return carry
        lax.fori_loop(0, MOE_BLOCK, body, 0, unroll=8)

    @pl.when(jnp.logical_and(i == 0, n_used > 0))
    def _():
        gather(rows_cur, 0)

    @pl.when(i + 1 < n_used)
    def _():
        gather(rows_nxt, 1 - slot)

    @pl.when(i < n_used)
    def _():
        pltpu.make_async_copy(xbuf.at[slot], xbuf.at[slot], sem.at[slot]).wait()
        x = xbuf[slot].astype(BF16)
        a = _dot(x, w1_ref[0])
        b = _dot(x, w3_ref[0])
        y_ref[...] = _dot((a * _sigmoid(a) * b).astype(BF16), w2_ref[0]).astype(y_ref.dtype)

    @pl.when(i >= n_used)
    def _():
        y_ref[...] = jnp.zeros_like(y_ref)


def _moe_experts(h, row_tok, meta, w1, w3, w2):
    n_rows = row_tok.shape[0]
    d = h.shape[1]
    n_blocks = n_rows // MOE_BLOCK
    de = w1.shape[2]
    rows3 = row_tok.reshape(n_blocks, 1, MOE_BLOCK)
    smem_rows = lambda f: pl.BlockSpec((1, 1, MOE_BLOCK), f, memory_space=pltpu.SMEM)
    return pl.pallas_call(
        functools.partial(_moe_kernel, n_blocks=n_blocks),
        grid_spec=pltpu.PrefetchScalarGridSpec(
            num_scalar_prefetch=1, grid=(n_blocks,),
            in_specs=[smem_rows(lambda i, m: (i, 0, 0)),
                      smem_rows(lambda i, m: (jnp.minimum(i + 1, n_blocks - 1), 0, 0)),
                      pl.BlockSpec(memory_space=pl.ANY),
                      pl.BlockSpec((1, d, de), lambda i, m: (m[i], 0, 0)),
                      pl.BlockSpec((1, d, de), lambda i, m: (m[i], 0, 0)),
                      pl.BlockSpec((1, de, d), lambda i, m: (m[i], 0, 0))],
            out_specs=pl.BlockSpec((MOE_BLOCK, d), lambda i, m: (i, 0)),
            scratch_shapes=[pltpu.VMEM((2, MOE_BLOCK, d), F32), pltpu.SemaphoreType.DMA((2,))]),
        out_shape=jax.ShapeDtypeStruct((n_rows, d), F32),
        compiler_params=pltpu.CompilerParams(dimension_semantics=("arbitrary",), vmem_limit_bytes=VMEM_LIMIT),
        name="moe_experts",
    )(meta, rows3, rows3, h, w1, w3, w2)


def _moe_ffn(h_all, tok_row, logits, n_groups, n_experts, w1, w3, w2):
    n = logits.shape[0]
    epg = n_experts // n_groups
    group_logits = logits[:, :n_groups]
    g_sel = jnp.argmax(group_logits, axis=-1)
    g_gate = jnp.take_along_axis(jax.nn.softmax(group_logits, axis=-1), g_sel[:, None], axis=1)
    exp_logits = logits[:, n_groups:n_groups + n_experts].reshape(n, n_groups, epg)
    within = jnp.take_along_axis(exp_logits, g_sel[:, None, None], axis=1)[:, 0]
    top_val, top_idx = lax.top_k(within, TOP_K)
    gate = jax.nn.softmax(top_val, axis=-1) * g_gate
    e_flat = (g_sel[:, None] * epg + top_idx).reshape(-1).astype(jnp.int32)
    n_assign = n * TOP_K
    order = jnp.argsort(e_flat)
    rank = jnp.argsort(order).astype(jnp.int32)
    counts = jnp.sum(e_flat[:, None] == jnp.arange(n_experts, dtype=jnp.int32)[None, :], axis=0, dtype=jnp.int32)
    starts = jnp.cumsum(counts) - counts
    padded = (counts + MOE_BLOCK - 1) // MOE_BLOCK * MOE_BLOCK
    pad_ends = jnp.cumsum(padded)
    pad_starts = pad_ends - padded
    n_blocks = -(-n_assign // MOE_BLOCK) + n_experts
    n_rows = n_blocks * MOE_BLOCK
    block_start = jnp.arange(n_blocks, dtype=jnp.int32) * MOE_BLOCK
    block_exp = jnp.minimum(jnp.sum(pad_ends[None, :] <= block_start[:, None], axis=1, dtype=jnp.int32), n_experts - 1)
    rows = jnp.arange(n_rows, dtype=jnp.int32)
    row_exp = jnp.repeat(block_exp, MOE_BLOCK)
    offset = rows - pad_starts[row_exp]
    src = jnp.clip(starts[row_exp] + offset, 0, n_assign - 1)
    row_tok = tok_row[jnp.where(offset < counts[row_exp], order[src].astype(jnp.int32) // TOP_K, 0)]
    meta = jnp.concatenate([block_exp, (pad_ends[-1:] // MOE_BLOCK).astype(jnp.int32)])
    y_rows = _moe_experts(h_all, row_tok, meta, w1, w3, w2)
    pos = (pad_starts[e_flat] + rank - starts[e_flat]).reshape(n, TOP_K)
    return gate[:, 0:1] * y_rows[pos[:, 0]] + gate[:, 1:2] * y_rows[pos[:, 1]]


def kernel(x, c, ctx, c_ctx, ada_w, ada_b, norm_mix, norm_ffn, w_in, w_out, s5_lam_re, s5_lam_im, s5_log_dt, s5_b_re, s5_b_im, s5_c_re, s5_c_im, s5_d, s5_w_glu, s5_b_glu, rw_mu, rw_w0, rw_w2, rw_a0, rw_a2, rw_g2, rw_k_k, rw_k_a, rw_r_k, rw_ln_w, rw_ln_b, hg_lb_logits, hg_norm, moe_w_group, moe_b_group, moe_w_expert, moe_b_expert, moe_w1, moe_w3, moe_w2, final_norm):
    bsz, n_lat, d = x.shape
    n_ctx = ctx.shape[1]
    depth = w_in.shape[0]
    s5_width = s5_d.shape[1]
    rw_width = rw_k_k.shape[1]
    rw_in = rw_mu.shape[2]
    hg_width = hg_norm.shape[1]
    n_groups = moe_w_group.shape[2]
    n_experts = moe_w_expert.shape[2]
    t = n_ctx + n_lat
    hg_col0 = -(-rw_in // hg_width)
    s5_off = (hg_col0 + 5) * hg_width
    n_proj = -(-(s5_off + s5_width) // IN_BN) * IN_BN
    pad1 = hg_col0 * hg_width - rw_in
    pad2 = n_proj - s5_off - s5_width

    silu_c = jax.nn.silu(c)
    silu_cc = jax.nn.silu(c_ctx)
    lb_all = jnp.cumsum(jax.nn.softmax(hg_lb_logits.astype(F32), axis=0), axis=0)
    lb_all = lb_all - lb_all[0:1]
    tok = jnp.concatenate([ctx, x], axis=1)
    for l in range(depth):
        last = l == depth - 1
        mod_lat = jnp.split(silu_c @ ada_w[l] + ada_b[l], 6, axis=-1)
        mod_ctx = [jnp.broadcast_to(z, (bsz, d)) for z in jnp.split(silu_cc @ ada_w[l] + ada_b[l], 6, axis=-1)]
        sh_m, sc_m, gt_m, sh_f, sc_f, gt_f = [jnp.stack([zc, zl], axis=1) for zc, zl in zip(mod_ctx, mod_lat)]
        mod_m = jnp.stack([norm_mix[l] * (1.0 + sc_m), sh_m], axis=2)
        mod_f = jnp.stack([norm_ffn[l] * (1.0 + sc_f), sh_f], axis=2)

        w_cols = [w_in[l][:, s5_width:s5_width + rw_in], jnp.zeros((d, pad1), F32),
                  w_in[l][:, s5_width + rw_in:], w_in[l][:, :s5_width], jnp.zeros((d, pad2), F32)]
        p = _in_proj(tok, mod_m, jnp.concatenate(w_cols, axis=1).astype(BF16), n_ctx)

        r, v, kk, g, lw, kd, bb = _rw_prep(p, n_ctx, rw_mu[l], rw_w0[l], rw_w2[l], rw_a0[l], rw_a2[l],
                                           rw_g2[l], rw_k_k[l], rw_k_a[l])
        y2, w1_b, w3_b = _rw_scan(r, lw, kd, v, kk, bb, n_ctx, casts=((moe_w1, l), (moe_w3, l)))
        o2, w2_b = _hg_scan(p, lb_all[l], n_ctx, col0=hg_col0, casts=((moe_w2, l),))
        experts = (w1_b.reshape(moe_w1.shape[1:]), w3_b.reshape(moe_w3.shape[1:]), w2_b.reshape(moe_w2.shape[1:]))
        s5y = _s5_scan(p[:, :, s5_off:s5_off + s5_width], n_ctx, s5_lam_re[l], s5_lam_im[l], s5_log_dt[l],
                       s5_b_re[l], s5_b_im[l], s5_c_re[l], s5_c_im[l], s5_d[l])

        w_router = jnp.concatenate([moe_w_group[l], moe_w_expert[l],
                                    jnp.zeros((d, ROUTER_PAD - n_groups - n_experts), F32)], axis=1)
        b_router = jnp.concatenate([moe_b_group[l], moe_b_expert[l],
                                    jnp.zeros((ROUTER_PAD - n_groups - n_experts,), F32)]).reshape(1, ROUTER_PAD)
        tok, h_ffn, logits = _out_proj(
            s5y, s5_w_glu[l], s5_b_glu[l], y2, r, kd, v, g, rw_r_k[l], rw_ln_w[l], rw_ln_b[l],
            o2, p, hg_col0 + 4, hg_norm[l], tok, gt_m, w_out[l], mod_f, w_router, b_router, n_ctx)

        h_rows = h_ffn.reshape(bsz * t, d)
        if last:
            lat_rows = (jnp.arange(bsz, dtype=jnp.int32)[:, None] * t + n_ctx
                        + jnp.arange(n_lat, dtype=jnp.int32)[None, :]).reshape(-1)
            f = _moe_ffn(h_rows, lat_rows, logits[:, n_ctx:].reshape(bsz * n_lat, ROUTER_PAD),
                         n_groups, n_experts, *experts).reshape(bsz, n_lat, d)
            x = tok[:, n_ctx:] + gt_f[:, 1:2] * f
        else:
            f = _moe_ffn(h_rows, jnp.arange(bsz * t, dtype=jnp.int32), logits.reshape(bsz * t, ROUTER_PAD),
                         n_groups, n_experts, *experts).reshape(bsz, t, d)
            tok = jnp.concatenate([tok[:, :n_ctx] + gt_f[:, 0:1] * f[:, :n_ctx],
                                   tok[:, n_ctx:] + gt_f[:, 1:2] * f[:, n_ctx:]], axis=1)
    return _final_norm(x, final_norm)
```

```python
import functools
import math

import numpy as np
import jax
import jax.numpy as jnp
from jax import lax
from jax.experimental import pallas as pl
from jax.experimental.pallas import tpu as pltpu

F32 = jnp.float32
BF16 = jnp.bfloat16

GRID_W = 64
CH = GRID_W
HEAD = 64
S5_CH = 16
S5_STATE = 64
S5_CHUNK = 32
RW_LORA = 64
RW_GN_EPS = 64e-5
NORM_EPS = 1e-6
TOP_K = 2
MOE_BLOCK = 256
VMEM_LIMIT = 48 * 1024 * 1024


def _dot(a, b):
    return jnp.dot(a, b, preferred_element_type=F32)


def _dot_nt(a, b):
    return lax.dot_general(a, b, (((1,), (1,)), ((), ())), preferred_element_type=F32)


def _dot_tn(a, b):
    return lax.dot_general(a, b, (((0,), (0,)), ((), ())), preferred_element_type=F32)


def _split3(x):
    hi = x.astype(BF16)
    r1 = x - hi.astype(F32)
    mid = r1.astype(BF16)
    lo = (r1 - mid.astype(F32)).astype(BF16)
    return hi, mid, lo


def _dot_sel(m01, x):
    hi, mid, lo = _split3(x)
    return _dot(m01, hi) + _dot(m01, mid) + _dot(m01, lo)


def _dot_hp(a, b):
    a_hi = a.astype(BF16)
    a_lo = (a - a_hi.astype(F32)).astype(BF16)
    b_hi = b.astype(BF16)
    b_lo = (b - b_hi.astype(F32)).astype(BF16)
    return _dot(a_hi, b_hi) + _dot(a_hi, b_lo) + _dot(a_lo, b_hi)


def _sigmoid(x):
    return 1.0 / (1.0 + jnp.exp(-x))


def _scan_chunk(d, i, ncc, nc):
    back = jnp.where(i < ncc, ncc - 1 - i, nc + ncc - 1 - i)
    return jnp.where(d == 0, i, back)


def _half_block_masks(row, col, sgn):
    masks = []
    n = 2
    while n <= CH:
        same = (row // n) == (col // n)
        t_hi = ((row % n) >= n // 2).astype(jnp.int32)
        s_hi = ((col % n) >= n // 2).astype(jnp.int32)
        masks.append(jnp.logical_and(same, (t_hi - s_hi) * sgn == 1))
        n *= 2
    return masks


def _cast_plan(w, layer, n_steps):
    _, e, r, c = w.shape
    total = e * r
    rows = 16
    while rows * n_steps < total or total % rows:
        rows *= 2
    last = total // rows - 1
    slab = lambda d, i: jnp.minimum(d * (n_steps // 2) + i, last)
    return (w.reshape(w.shape[0], total, c),
            pl.BlockSpec((1, rows, c), lambda d, i: (layer, slab(d, i), 0)),
            pl.BlockSpec((rows, c), lambda d, i: (slab(d, i), 0)),
            jax.ShapeDtypeStruct((total, c), BF16))


def _cast_slabs(in_refs, out_refs):
    for wi, wo in zip(in_refs, out_refs):
        wo[...] = wi[0].astype(wo.dtype)


def _head_block_diag(width):
    idx = np.arange(width) // HEAD
    return jnp.asarray(idx[:, None] == idx[None, :], dtype=BF16)


def _rw_prep_kernel(cur_ref, prev_ref, next_ref, mu_ref, w0_ref, w2_ref, a0_ref, a2_ref, g2_ref,
                    kkw_ref, ka_ref, bd_ref,
                    r_out, v_out, kk_out, g_out, lw_out, kd_out, bb_out, *, ncc, nc, width):
    n = pl.program_id(1)
    cur = cur_ref[0].astype(F32)
    prv = prev_ref[0].astype(F32)
    nxt = next_ref[0].astype(F32)
    is_ctx = n < ncc
    row = lax.broadcasted_iota(jnp.int32, (CH, 1), 0)
    zero_row = jnp.zeros((1, cur.shape[1]), F32)
    first_fill = jnp.where(jnp.logical_and(is_ctx, n > 0), prv[CH - 1:CH, :], zero_row)
    last_fill = jnp.where(jnp.logical_and(is_ctx, n < ncc - 1), nxt[0:1, :], zero_row)
    nb_prev = jnp.where(row == 0, first_fill, pltpu.roll(cur, 1, 0))
    nb_next = jnp.where(row == CH - 1, last_fill, pltpu.roll(cur, CH - 1, 0))
    mu = mu_ref[...]
    zero_mu = jnp.zeros_like(mu[0:1])
    w_up = jnp.where(jnp.logical_and(jnp.logical_not(is_ctx), n > ncc), mu[2:3], zero_mu)
    w_down = jnp.where(jnp.logical_and(jnp.logical_not(is_ctx), n < nc - 1), mu[3:4], zero_mu)
    w_cur = 1.0 - mu[0:1] - mu[1:2] - jnp.where(is_ctx, zero_mu, mu[2:3] + mu[3:4])
    x = w_cur * cur + mu[0:1] * nb_prev + mu[1:2] * nb_next + w_up * prv + w_down * nxt

    w = width
    r = x[:, 0:w]
    k = x[:, w:2 * w]
    v = x[:, 2 * w:3 * w]
    wd = x[:, 3 * w:3 * w + 2 * RW_LORA]
    ad = x[:, 3 * w + 2 * RW_LORA:3 * w + 4 * RW_LORA]
    gd = x[:, 3 * w + 4 * RW_LORA:]

    r_out[0] = r.astype(r_out.dtype)
    v_out[0] = v.astype(v_out.dtype)
    g_out[0] = _dot_hp(_sigmoid(gd), g2_ref[...]).astype(g_out.dtype)
    kk = k * kkw_ref[...]
    kk2 = kk * kk
    kk2_hi = kk2.astype(BF16)
    kk2_lo = (kk2 - kk2_hi.astype(F32)).astype(BF16)
    ss = _dot(kk2_hi, bd_ref[...]) + _dot(kk2_lo, bd_ref[...])
    kk = kk / jnp.maximum(jnp.sqrt(ss), 1e-12)
    kk_out[0] = kk.astype(kk_out.dtype)
    ka = ka_ref[...]
    for d in range(2):
        w_raw = w0_ref[d:d + 1, :] + _dot_hp(jnp.tanh(wd[:, d * RW_LORA:(d + 1) * RW_LORA]), w2_ref[d])
        nw = -w_raw
        softplus = jnp.maximum(nw, 0.0) + jnp.log(1.0 + jnp.exp(-jnp.abs(nw)))
        lw_out[d, 0] = -jnp.exp(-softplus - 0.5)
        a = _sigmoid(a0_ref[d:d + 1, :] + _dot_hp(ad[:, d * RW_LORA:(d + 1) * RW_LORA], a2_ref[d]))
        kd_out[d, 0] = (k * (1.0 + (a - 1.0) * ka)).astype(kd_out.dtype)
        bb_out[d, 0] = (kk * a).astype(bb_out.dtype)


def _rw_prep(p_rw, n_ctx, mu, w0, w2, a0, a2, g2, k_k, k_a):
    bsz, t, _ = p_rw.shape
    win = mu.shape[1]
    width = k_k.shape[0]
    nc = t // CH
    ncc = n_ctx // CH
    tok = lambda b, n: (b, n, 0)
    full2 = lambda b, n: (0, 0)
    full3 = lambda b, n: (0, 0, 0)
    blk_in = (1, CH, win)
    out1 = jax.ShapeDtypeStruct((bsz, t, width), BF16)
    out2 = jax.ShapeDtypeStruct((2, bsz, t, width), BF16)
    out_lw = jax.ShapeDtypeStruct((2, bsz, t, width), F32)
    spec1 = pl.BlockSpec((1, CH, width), tok)
    spec2 = pl.BlockSpec((2, 1, CH, width), lambda b, n: (0, b, n, 0))
    return pl.pallas_call(
        functools.partial(_rw_prep_kernel, ncc=ncc, nc=nc, width=width),
        grid=(bsz, nc),
        in_specs=[
            pl.BlockSpec(blk_in, tok),
            pl.BlockSpec(blk_in, lambda b, n: (b, jnp.maximum(n - 1, 0), 0)),
            pl.BlockSpec(blk_in, lambda b, n: (b, jnp.minimum(n + 1, nc - 1), 0)),
            pl.BlockSpec(mu.shape, full2),
            pl.BlockSpec(w0.shape, full2),
            pl.BlockSpec(w2.shape, full3),
            pl.BlockSpec(a0.shape, full2),
            pl.BlockSpec(a2.shape, full3),
            pl.BlockSpec(g2.shape, full2),
            pl.BlockSpec((1, width), full2),
            pl.BlockSpec((1, width), full2),
            pl.BlockSpec((width, width), full2),
        ],
        out_specs=[spec1, spec1, spec1, spec1, spec2, spec2, spec2],
        out_shape=[out1, out1, out1, out1, out_lw, out2, out2],
        compiler_params=pltpu.CompilerParams(
            dimension_semantics=("arbitrary", "arbitrary"), vmem_limit_bytes=VMEM_LIMIT),
        name="rwkv_prep",
    )(p_rw, p_rw, p_rw, mu, w0, w2, a0, a2, g2, k_k.reshape(1, width), k_a.reshape(1, width),
      _head_block_diag(width))


def _rw_scan_kernel(r_ref, lw_ref, kd_ref, v_ref, kk_ref, bb_ref, *rest, heads, bsz, n_cast):
    y_ref, s_ref = rest[n_cast], rest[-1]
    _cast_slabs(rest[:n_cast], rest[n_cast + 1:-1])
    d = pl.program_id(0)
    i = pl.program_id(1)

    @pl.when(i == 0)
    def _():
        s_ref[...] = jnp.zeros_like(s_ref)

    row = lax.broadcasted_iota(jnp.int32, (CH, CH), 0)
    col = lax.broadcasted_iota(jnp.int32, (CH, CH), 1)
    sgn = jnp.where(d == 0, 1, -1)
    strict = (row - col) * sgn > 0
    diag = row == col
    incl = jnp.logical_or(strict, diag)
    tri = jnp.where(incl, 1.0, 0.0).astype(BF16)
    eye = jnp.where(diag, 1.0, 0.0)
    halves = _half_block_masks(row, col, sgn)

    hs = range(bsz * heads)
    sls = [slice(h * HEAD, (h + 1) * HEAD) for h in range(heads)] * bsz
    s0 = [s_ref[h] for h in hs]
    v, kkr, bk, b_w, k_w, wtot = [], [], [], [], [], []
    for b in range(bsz):
        lw = lw_ref[0, b]
        cw = _dot_sel(tri, lw)
        ctot = jnp.sum(lw, axis=0, keepdims=True)
        e_ncw = jnp.exp(-cw)
        e_rem = jnp.exp(ctot - cw)
        r_t = (r_ref[b].astype(F32) * jnp.exp(cw)).astype(BF16)
        kk_t = (kk_ref[b].astype(F32) * jnp.exp(cw - lw)).astype(BF16)
        bb = bb_ref[0, b].astype(F32)
        kd = kd_ref[0, b].astype(F32)
        b_t = (bb * e_ncw).astype(BF16)
        k_t = (kd * e_ncw).astype(BF16)
        bw_all = (bb * e_rem).astype(BF16)
        kw_all = (kd * e_rem).astype(BF16)
        wt_all = jnp.exp(ctot)
        v_all = v_ref[b]
        for sl in sls[:heads]:
            v.append(v_all[:, sl])
            kkr.append(jnp.concatenate([kk_t[:, sl], r_t[:, sl]], axis=0))
            bk.append(jnp.concatenate([b_t[:, sl], k_t[:, sl]], axis=0))
            b_w.append(bw_all[:, sl])
            k_w.append(kw_all[:, sl])
            wtot.append(wt_all[:, sl])
    att = [_dot_nt(kkr[h], bk[h]) for h in hs]
    proj = [_dot_nt(kkr[h], s0[h].astype(BF16)) for h in hs]
    a_ab = [jnp.where(strict, att[h][:CH, :CH], 0.0) for h in hs]
    a_ak = [jnp.where(strict, att[h][:CH, CH:], 0.0).astype(BF16) for h in hs]
    a_rb = [jnp.where(incl, att[h][CH:, :CH], 0.0).astype(BF16) for h in hs]
    a_rk = [jnp.where(incl, att[h][CH:, CH:], 0.0).astype(BF16) for h in hs]
    inv = [eye - jnp.where(halves[0], a_ab[h], 0.0) for h in hs]
    for lvl in range(1, len(halves)):
        inv_b = [inv[h].astype(BF16) for h in hs]
        a_l = [jnp.where(halves[lvl], a_ab[h], 0.0).astype(BF16) for h in hs]
        left = [_dot(inv_b[h], a_l[h]).astype(BF16) for h in hs]
        inv = [inv[h] - _dot(left[h], inv_b[h]) for h in hs]
    rhs = [(proj[h][:CH] + _dot(a_ak[h], v[h])).astype(BF16) for h in hs]
    u_b = [(-_dot(inv[h].astype(BF16), rhs[h])).astype(BF16) for h in hs]
    y = [proj[h][CH:] + _dot(a_rb[h], u_b[h]) + _dot(a_rk[h], v[h]) for h in hs]
    s_new = [s0[h] * wtot[h] + _dot_tn(u_b[h], b_w[h]) + _dot_tn(v[h], k_w[h]) for h in hs]
    for h in hs:
        y_ref[0, h // heads, :, sls[h]] = y[h].astype(y_ref.dtype)
        s_ref[h] = s_new[h]


def _rw_scan(r, lw, kd, v, kk, bb, n_ctx, casts=()):
    bsz, t, width = r.shape
    heads = width // HEAD
    nc = t // CH
    ncc = n_ctx // CH
    s1 = pl.BlockSpec((bsz, CH, width), lambda d, i: (0, _scan_chunk(d, i, ncc, nc), 0))
    s2 = pl.BlockSpec((1, bsz, CH, width), lambda d, i: (d, 0, _scan_chunk(d, i, ncc, nc), 0))
    plans = [_cast_plan(w, layer, 2 * nc) for w, layer in casts]
    return pl.pallas_call(
        functools.partial(_rw_scan_kernel, heads=heads, bsz=bsz, n_cast=len(plans)),
        grid=(2, nc),
        in_specs=[s1, s2, s2, s1, s1, s2] + [p[1] for p in plans],
        out_specs=[s2] + [p[2] for p in plans],
        out_shape=[jax.ShapeDtypeStruct((2, bsz, t, width), BF16)] + [p[3] for p in plans],
        scratch_shapes=[pltpu.VMEM((bsz * heads, HEAD, HEAD), F32)],
        compiler_params=pltpu.CompilerParams(
            dimension_semantics=("arbitrary", "arbitrary"), vmem_limit_bytes=VMEM_LIMIT),
        name="rwkv_scan",
    )(r, lw, kd, v, kk, bb, *[p[0] for p in plans])


HG_LEVELS = 6


def _hg_decay_matrices():
    out = np.zeros((2, HG_LEVELS + 2, CH, CH), np.float32)
    pos = np.arange(CH)
    for d in range(2):
        for t in range(CH):
            if d == 0:
                out[d, 0, t, :t + 1] = 1.0
                out[d, 1, t, t + 1:] = 1.0
            else:
                out[d, 0, t, t:] = 1.0
                out[d, 1, t, :t] = 1.0
            for l in range(1, HG_LEVELS + 1):
                n = 1 << l
                mid = (t // n) * n + n // 2
                if d == 0:
                    sel = (pos >= mid) & (pos <= t) if t >= mid else (pos > t) & (pos < mid)
                else:
                    sel = (pos >= t) & (pos < mid) if t < mid else (pos >= mid) & (pos < t)
                out[d, 1 + l, t, sel] = 1.0
    return jnp.asarray(out.reshape(2, (HG_LEVELS + 2) * CH, CH), dtype=BF16)


def _hg_scan_kernel(q_ref, f_ref, i_ref, lb_ref, dm_ref, *rest, heads, bsz, n_cast):
    o_ref, s_ref = rest[n_cast], rest[-1]
    _cast_slabs(rest[:n_cast], rest[n_cast + 1:-1])
    d = pl.program_id(0)
    step = pl.program_id(1)

    @pl.when(step == 0)
    def _():
        s_ref[...] = jnp.zeros_like(s_ref)

    row = lax.broadcasted_iota(jnp.int32, (CH, CH), 0)
    col = lax.broadcasted_iota(jnp.int32, (CH, CH), 1)
    sgn = jnp.where(d == 0, 1, -1)

    lb = lb_ref[...]
    masks = _half_block_masks(row, col, sgn)
    lvl_masks = [row == col] + masks
    hs = range(bsz * heads)
    sls = [slice(h * HEAD, (h + 1) * HEAD) for h in range(heads)] * bsz
    q_lvl, k_lvl, q_in, k_out, etot, v_all = [], [], [], [], [], []
    for b in range(bsz):
        f = f_ref[b].astype(F32)
        q = q_ref[b].astype(F32)
        q = q * _sigmoid(q)
        log_sig = jnp.minimum(f, 0.0) - jnp.log(1.0 + jnp.exp(-jnp.abs(f)))
        hi_term = jnp.log(1.0 - lb) + log_sig
        lo_term = jnp.log(jnp.where(lb > 0.0, lb, 1.0))
        mx = jnp.maximum(hi_term, lo_term)
        lf_mix = mx + jnp.log(1.0 + jnp.exp(-jnp.abs(hi_term - lo_term)))
        lf = jnp.where(lb > 0.0, lf_mix, hi_term)
        key = (1.0 - lb) * _sigmoid(-f)
        ex = jnp.exp(_dot_sel(dm_ref[0], lf))
        q_in.append((q * ex[0:CH]).astype(BF16))
        k_out.append((key * ex[CH:2 * CH]).astype(BF16))
        etot.append(jnp.exp(jnp.sum(lf, axis=0, keepdims=True)))
        v_all.append(i_ref[b].astype(BF16))
        q_lvl.append([q.astype(BF16)] + [(q * ex[(1 + l) * CH:(2 + l) * CH]).astype(BF16)
                                         for l in range(1, HG_LEVELS + 1)])
        k_lvl.append([key.astype(BF16)] + [(key * ex[(1 + l) * CH:(2 + l) * CH]).astype(BF16)
                                           for l in range(1, HG_LEVELS + 1)])

    bi = [h // heads for h in hs]
    s0 = [s_ref[h] for h in hs]
    sc = [jnp.zeros((CH, CH), F32) for _ in hs]
    for l in range(HG_LEVELS + 1):
        sc = [sc[h] + jnp.where(lvl_masks[l], _dot_nt(q_lvl[bi[h]][l][:, sls[h]], k_lvl[bi[h]][l][:, sls[h]]), 0.0)
              for h in hs]
    o = [_dot(sc[h].astype(BF16), v_all[bi[h]][:, sls[h]]) + _dot_nt(q_in[bi[h]][:, sls[h]], s0[h].astype(BF16))
         for h in hs]
    s_new = [s0[h] * etot[bi[h]][:, sls[h]] + _dot_tn(v_all[bi[h]][:, sls[h]], k_out[bi[h]][:, sls[h]]) for h in hs]
    for h in hs:
        o_ref[0, bi[h], :, sls[h]] = o[h].astype(o_ref.dtype)
        s_ref[h] = s_new[h]


def _hg_scan(p_hg, lb, n_ctx, col0=0, casts=()):
    bsz, t, _ = p_hg.shape
    width = lb.shape[0]
    heads = width // HEAD
    nc = t // CH
    ncc = n_ctx // CH
    dm = _hg_decay_matrices()
    chunk = lambda d, i: _scan_chunk(d, i, ncc, nc)
    blk = (bsz, CH, width)
    plans = [_cast_plan(w, layer, 2 * nc) for w, layer in casts]
    return pl.pallas_call(
        functools.partial(_hg_scan_kernel, heads=heads, bsz=bsz, n_cast=len(plans)),
        grid=(2, nc),
        in_specs=[
            pl.BlockSpec(blk, lambda d, i: (0, chunk(d, i), col0)),
            pl.BlockSpec(blk, lambda d, i: (0, chunk(d, i), col0 + 1 + d)),
            pl.BlockSpec(blk, lambda d, i: (0, chunk(d, i), col0 + 3)),
            pl.BlockSpec((1, width), lambda d, i: (0, 0)),
            pl.BlockSpec((1,) + dm.shape[1:], lambda d, i: (d, 0, 0)),
        ] + [p[1] for p in plans],
        out_specs=[pl.BlockSpec((1, bsz, CH, width), lambda d, i: (d, 0, chunk(d, i), 0))] + [p[2] for p in plans],
        out_shape=[jax.ShapeDtypeStruct((2, bsz, t, width), BF16)] + [p[3] for p in plans],
        scratch_shapes=[pltpu.VMEM((bsz * heads, HEAD, HEAD), F32)],
        compiler_params=pltpu.CompilerParams(
            dimension_semantics=("arbitrary", "arbitrary"), vmem_limit_bytes=VMEM_LIMIT),
        name="hgrn2_scan",
    )(p_hg, p_hg, p_hg, lb.reshape(1, width), dm, *[p[0] for p in plans])


def _s5_operators(lam_re, lam_im, log_dt, b_re, b_im, c_re, c_im):
    c = S5_CHUNK
    hp = lax.Precision.HIGHEST
    dt = jnp.exp(log_dt.astype(F32))[..., None]
    lam_re = lam_re.astype(F32)
    lam_im = lam_im.astype(F32)
    mag = jnp.exp(lam_re * dt)
    ang = lam_im * dt
    a_re, a_im = mag * jnp.cos(ang), mag * jnp.sin(ang)
    den = lam_re * lam_re + lam_im * lam_im
    f_re = ((a_re - 1.0) * lam_re + a_im * lam_im) / den
    f_im = (a_im * lam_re - (a_re - 1.0) * lam_im) / den
    bb_re = f_re[..., None] * b_re - f_im[..., None] * b_im
    bb_im = f_re[..., None] * b_im + f_im[..., None] * b_re
    j = jnp.arange(c + 1, dtype=F32)[None, None, :, None]
    pmag = jnp.exp((lam_re * dt)[:, :, None, :] * j)
    pang = ang[:, :, None, :] * j
    pw_re, pw_im = pmag * jnp.cos(pang), pmag * jnp.sin(pang)
    ca_re = c_re[:, :, None] * pw_re[:, :, :, None] - c_im[:, :, None] * pw_im[:, :, :, None]
    ca_im = c_re[:, :, None] * pw_im[:, :, :, None] + c_im[:, :, None] * pw_re[:, :, :, None]
    kern = (jnp.einsum('dgjcp,dgpk->dgjck', ca_re, bb_re, precision=hp)
            - jnp.einsum('dgjcp,dgpk->dgjck', ca_im, bb_im, precision=hp))
    g = kern.shape[1]
    by_lag = jnp.concatenate([kern[1][:, c - 1:0:-1], kern[0][:, 0:1] + kern[1][:, 0:1], kern[0][:, 1:c]], axis=1)
    tpos = np.arange(c)
    lag = tpos[None, :] - tpos[:, None]
    onehot = jnp.asarray((lag[None] + c - 1) == np.arange(2 * c - 1)[:, None, None], dtype=F32)
    tmat = jnp.einsum('gjoc,jab->gacbo', by_lag, onehot, precision=hp).reshape(g, c * S5_CH, c * S5_CH)

    def in_to_state(pr, pi, d):
        hr = pr[:, :, None, :] * jnp.swapaxes(bb_re[d], 1, 2)[:, None] - pi[:, :, None, :] * jnp.swapaxes(bb_im[d], 1, 2)[:, None]
        hi = pr[:, :, None, :] * jnp.swapaxes(bb_im[d], 1, 2)[:, None] + pi[:, :, None, :] * jnp.swapaxes(bb_re[d], 1, 2)[:, None]
        return hr.reshape(g, c * S5_CH, -1), hi.reshape(g, c * S5_CH, -1)

    hf_re, hf_im = in_to_state(pw_re[0, :, c - 1::-1][:, :c], pw_im[0, :, c - 1::-1][:, :c], 0)
    hb_re, hb_im = in_to_state(pw_re[1, :, :c], pw_im[1, :, :c], 1)
    hmat = jnp.concatenate([hf_re, hb_re, hf_im, hb_im], axis=-1)

    def state_to_out(cr, ci):
        return (jnp.transpose(cr, (0, 3, 1, 2)).reshape(g, -1, c * S5_CH),
                -jnp.transpose(ci, (0, 3, 1, 2)).reshape(g, -1, c * S5_CH))

    gf_re, gf_im = state_to_out(ca_re[0, :, 1:c + 1], ca_im[0, :, 1:c + 1])
    gb_re, gb_im = state_to_out(ca_re[1, :, c:0:-1], ca_im[1, :, c:0:-1])
    z = jnp.zeros_like(gf_re)
    gf = jnp.concatenate([gf_re, z, gf_im, z], axis=1)
    gb = jnp.concatenate([z, gb_re, z, gb_im], axis=1)
    ac_re = jnp.concatenate([pw_re[0, :, c], pw_re[1, :, c]], axis=-1)[:, None, :]
    ac_im = jnp.concatenate([pw_im[0, :, c], pw_im[1, :, c]], axis=-1)[:, None, :]
    return tmat.astype(BF16), hmat.astype(BF16), gf.astype(BF16), gb.astype(BF16), ac_re, ac_im


def _s5_local_kernel(u_ref, h_ref, loc_ref):
    loc_ref[0] = _dot(u_ref[0].astype(BF16), h_ref[0])


def _s5_state_kernel(loc_ref, ar_ref, ai_ref, sf_ref, sb_ref, *, nc, ncc, bsz):
    ar = ar_ref[0]
    ai = ai_ref[0]
    half = ar.shape[-1]
    is_fwd = lax.broadcasted_iota(jnp.int32, (1, half), 1) < half // 2

    def body(i, carry):
        nb = jnp.where(i < ncc, ncc - 1 - i, nc + ncc - 1 - i)
        new = []
        for b in range(bsz):
            re, im = carry[b]
            st = jnp.concatenate([re, im], axis=1)
            sf_ref[0, b, pl.ds(i, 1), :] = st
            sb_ref[0, b, pl.ds(nb, 1), :] = st
            lf = loc_ref[0, b, pl.ds(i, 1), :]
            lk = loc_ref[0, b, pl.ds(nb, 1), :]
            l_re = jnp.where(is_fwd, lf[:, :half], lk[:, :half])
            l_im = jnp.where(is_fwd, lf[:, half:], lk[:, half:])
            new.append((ar * re - ai * im + l_re, ar * im + ai * re + l_im))
        return tuple(new)

    zero = jnp.zeros((1, half), F32)
    lax.fori_loop(0, nc, body, tuple((zero, zero) for _ in range(bsz)))


def _s5_out_kernel(u_ref, t_ref, sf_ref, gf_ref, sb_ref, gb_ref, d_ref, y_ref):
    u = u_ref[0]
    y = (_dot(u.astype(BF16), t_ref[0]) + _dot(sf_ref[0].astype(BF16), gf_ref[0])
         + _dot(sb_ref[0].astype(BF16), gb_ref[0]) + d_ref[0] * u.astype(F32))
    gelu = 0.5 * y * (1.0 + jnp.tanh(math.sqrt(2.0 / math.pi) * (y + 0.044715 * (y * y * y))))
    y_ref[0] = gelu.astype(y_ref.dtype)


def _s5_scan(p_s5, n_ctx, lam_re, lam_im, log_dt, b_re, b_im, c_re, c_im, d_skip):
    bsz, t, width = p_s5.shape
    g = width // S5_CH
    c = S5_CHUNK
    nc, ncc = t // c, n_ctx // c
    blk = c * S5_CH
    rows = bsz * nc
    tmat, hmat, gf, gb, ac_re, ac_im = _s5_operators(lam_re, lam_im, log_dt, b_re, b_im, c_re, c_im)
    u = jnp.transpose(p_s5.astype(F32).reshape(bsz, nc, c, g, S5_CH), (3, 0, 1, 2, 4)).reshape(g, rows, blk)
    per_g = lambda gi: (gi, 0, 0)
    cparams = pltpu.CompilerParams(dimension_semantics=("arbitrary",), vmem_limit_bytes=VMEM_LIMIT)
    loc = pl.pallas_call(
        _s5_local_kernel, grid=(g,),
        in_specs=[pl.BlockSpec((1, rows, blk), per_g), pl.BlockSpec((1, blk, 256), per_g)],
        out_specs=pl.BlockSpec((1, rows, 256), per_g),
        out_shape=jax.ShapeDtypeStruct((g, rows, 256), F32),
        compiler_params=cparams, name="s5_local",
    )(u, hmat)
    per_g4 = lambda gi: (gi, 0, 0, 0)
    st_shape = jax.ShapeDtypeStruct((g, bsz, nc, 256), F32)
    sf, sb = pl.pallas_call(
        functools.partial(_s5_state_kernel, nc=nc, ncc=ncc, bsz=bsz), grid=(g,),
        in_specs=[pl.BlockSpec((1, bsz, nc, 256), per_g4), pl.BlockSpec((1, 1, 128), per_g),
                  pl.BlockSpec((1, 1, 128), per_g)],
        out_specs=[pl.BlockSpec((1, bsz, nc, 256), per_g4)] * 2,
        out_shape=[st_shape, st_shape],
        compiler_params=cparams, name="s5_state",
    )(loc.reshape(g, bsz, nc, 256), ac_re, ac_im)
    d_flat = jnp.tile(d_skip.astype(F32).reshape(g, 1, S5_CH), (1, c, 1)).reshape(g, 1, blk)
    y = pl.pallas_call(
        _s5_out_kernel, grid=(g,),
        in_specs=[pl.BlockSpec((1, rows, blk), per_g), pl.BlockSpec((1, blk, blk), per_g),
                  pl.BlockSpec((1, rows, 256), per_g), pl.BlockSpec((1, 256, blk), per_g),
                  pl.BlockSpec((1, rows, 256), per_g), pl.BlockSpec((1, 256, blk), per_g),
                  pl.BlockSpec((1, 1, blk), per_g)],
        out_specs=pl.BlockSpec((1, rows, blk), per_g),
        out_shape=jax.ShapeDtypeStruct((g, rows, blk), F32),
        compiler_params=cparams, name="s5_out",
    )(u, tmat, sf.reshape(g, rows, 256), gf, sb.reshape(g, rows, 256), gb, d_flat)
    return jnp.transpose(y.reshape(g, bsz, nc, c, S5_CH), (1, 2, 3, 0, 4)).reshape(bsz, t, width)


IN_BM = 1280
IN_BN = 768
OUT_BM = 256
ROUTER_PAD = 128


def _row_block(t, target):
    bm = target
    while t % bm:
        bm -= CH
    return bm


def _norm_mod(x, mod_ref, first_row, n_ctx):
    xn = x * lax.rsqrt(jnp.mean(x * x, axis=-1, keepdims=True) + NORM_EPS)
    is_ctx = (first_row + lax.broadcasted_iota(jnp.int32, (x.shape[0], 1), 0)) < n_ctx
    scale = jnp.where(is_ctx, mod_ref[0, 0, 0:1, :], mod_ref[0, 1, 0:1, :])
    shift = jnp.where(is_ctx, mod_ref[0, 0, 1:2, :], mod_ref[0, 1, 1:2, :])
    return xn * scale + shift


def _in_proj_kernel(x_ref, mod_ref, w_ref, p_ref, h_ref, *, n_ctx, bm):
    @pl.when(pl.program_id(2) == 0)
    def _():
        sub = _row_block(bm, 256)

        def body(k, carry):
            rows = pl.ds(pl.multiple_of(k * sub, sub), sub)
            h = _norm_mod(x_ref[0, rows, :], mod_ref, pl.program_id(1) * bm + k * sub, n_ctx)
            h_ref[rows, :] = h.astype(BF16)
            return carry

        lax.fori_loop(0, bm // sub, body, 0)

    p_ref[0] = _dot(h_ref[...], w_ref[...]).astype(p_ref.dtype)


def _in_proj(tok, mod, w, n_ctx):
    bsz, t, d = tok.shape
    n = w.shape[1]
    bm = _row_block(t, IN_BM)
    return pl.pallas_call(
        functools.partial(_in_proj_kernel, n_ctx=n_ctx, bm=bm),
        grid=(bsz, t // bm, n // IN_BN),
        in_specs=[pl.BlockSpec((1, bm, d), lambda b, i, j: (b, i, 0), pipeline_mode=pl.Buffered(1)),
                  pl.BlockSpec((1, 2, 2, d), lambda b, i, j: (b, 0, 0, 0)),
                  pl.BlockSpec((d, IN_BN), lambda b, i, j: (0, j))],
        out_specs=pl.BlockSpec((1, bm, IN_BN), lambda b, i, j: (b, i, j)),
        out_shape=jax.ShapeDtypeStruct((bsz, t, n), F32),
        scratch_shapes=[pltpu.VMEM((bm, d), BF16)],
        compiler_params=pltpu.CompilerParams(
            dimension_semantics=("arbitrary", "arbitrary", "arbitrary"), vmem_limit_bytes=VMEM_LIMIT),
        name="in_proj",
    )(tok, mod, w)


def _out_proj_kernel(s5_ref, wglu_ref, bglu_ref,
                     y2_ref, r_ref, kd_ref, v_ref, g_ref, rk_ref, lnw_ref, lnb_ref,
                     o2_ref, og_ref, hgn_ref, bd_ref,
                     x_ref, gate_ref, wout_ref, modf_ref, wr_ref, br_ref,
                     xo_ref, h_ref, lg_ref, *, n_ctx, bm):
    first = pl.program_id(1) * bm
    bd = bd_ref[...]
    inv_n = 1.0 / HEAD

    def head_sum(z):
        return _dot(z.astype(BF16), bd)

    ys = s5_ref[0].astype(F32)
    ys = ys * _sigmoid(_dot(ys.astype(BF16), wglu_ref[...]) + bglu_ref[...])
    y = y2_ref[0, 0].astype(F32) + y2_ref[1, 0].astype(F32)
    yc = y - head_sum(y) * inv_n
    var = head_sum(yc * yc) * inv_n
    k_bonus = 0.5 * (kd_ref[0, 0].astype(F32) + kd_ref[1, 0].astype(F32))
    bonus = head_sum(r_ref[0].astype(F32) * k_bonus * rk_ref[...]) * v_ref[0].astype(F32)
    yr = (yc * lax.rsqrt(var + RW_GN_EPS) * lnw_ref[...] + lnb_ref[...] + bonus) * g_ref[0].astype(F32)
    o = o2_ref[0, 0].astype(F32) + o2_ref[1, 0].astype(F32)
    yh = o * lax.rsqrt(head_sum(o * o) * inv_n + NORM_EPS) * hgn_ref[...] * _sigmoid(og_ref[0].astype(F32))

    m = _dot(jnp.concatenate([ys, yr, yh], axis=-1).astype(BF16), wout_ref[...])
    is_ctx = (first + lax.broadcasted_iota(jnp.int32, (bm, 1), 0)) < n_ctx
    x_new = x_ref[0] + jnp.where(is_ctx, gate_ref[0, 0:1, :], gate_ref[0, 1:2, :]) * m
    xo_ref[0] = x_new
    h = _norm_mod(x_new, modf_ref, first, n_ctx)
    h_ref[0] = h
    lg_ref[0] = _dot(h.astype(BF16), wr_ref[...]) + br_ref[...]


def _out_proj(s5y, w_glu, b_glu, y2, r, kd, v, g, r_k, ln_w, ln_b, o2, p, og_col, hg_norm,
              tok, gate, w_out, mod_f, w_router, b_router, n_ctx):
    bsz, t, d = tok.shape
    bm = _row_block(t, OUT_BM)
    s5w = s5y.shape[-1]
    w = r.shape[-1]
    row = lambda b, i: (b, i, 0)
    row2 = lambda b, i: (0, b, i, 0)
    c2 = lambda b, i: (0, 0)
    vec = lambda z: z.reshape(1, -1).astype(F32)
    spec_w = pl.BlockSpec((1, bm, w), row)
    spec_2w = pl.BlockSpec((2, 1, bm, w), row2)
    spec_vw = pl.BlockSpec((1, w), c2)
    return pl.pallas_call(
        functools.partial(_out_proj_kernel, n_ctx=n_ctx, bm=bm),
        grid=(bsz, t // bm),
        in_specs=[
            pl.BlockSpec((1, bm, s5w), row), pl.BlockSpec((s5w, s5w), c2), pl.BlockSpec((1, s5w), c2),
            spec_2w, spec_w, spec_2w, spec_w, spec_w, spec_vw, spec_vw, spec_vw,
            spec_2w, pl.BlockSpec((1, bm, w), lambda b, i: (b, i, og_col)), spec_vw, pl.BlockSpec((w, w), c2),
            pl.BlockSpec((1, bm, d), row), pl.BlockSpec((1, 2, d), lambda b, i: (b, 0, 0)),
            pl.BlockSpec(w_out.shape, c2), pl.BlockSpec((1, 2, 2, d), lambda b, i: (b, 0, 0, 0)),
            pl.BlockSpec((d, ROUTER_PAD), c2), pl.BlockSpec((1, ROUTER_PAD), c2),
        ],
        out_specs=[pl.BlockSpec((1, bm, d), row), pl.BlockSpec((1, bm, d), row),
                   pl.BlockSpec((1, bm, ROUTER_PAD), row)],
        out_shape=[jax.ShapeDtypeStruct((bsz, t, d), F32), jax.ShapeDtypeStruct((bsz, t, d), F32),
                   jax.ShapeDtypeStruct((bsz, t, ROUTER_PAD), F32)],
        compiler_params=pltpu.CompilerParams(
            dimension_semantics=("arbitrary", "arbitrary"), vmem_limit_bytes=VMEM_LIMIT),
        name="out_proj",
    )(s5y, w_glu.astype(BF16), vec(b_glu), y2, r, kd, v, g, vec(r_k), vec(ln_w), vec(ln_b),
      o2, p, vec(hg_norm), _head_block_diag(w), tok, gate, w_out.astype(BF16), mod_f,
      w_router.astype(BF16), b_router)


def _final_norm_kernel(x_ref, g_ref, o_ref):
    x = x_ref[0]
    o_ref[0] = x * lax.rsqrt(jnp.mean(x * x, axis=-1, keepdims=True) + NORM_EPS) * g_ref[...]


def _final_norm(x, g):
    bsz, t, d = x.shape
    bm = _row_block(t, 512)
    return pl.pallas_call(
        _final_norm_kernel, grid=(bsz, t // bm),
        in_specs=[pl.BlockSpec((1, bm, d), lambda b, i: (b, i, 0)), pl.BlockSpec((1, d), lambda b, i: (0, 0))],
        out_specs=pl.BlockSpec((1, bm, d), lambda b, i: (b, i, 0)),
        out_shape=jax.ShapeDtypeStruct((bsz, t, d), F32),
        compiler_params=pltpu.CompilerParams(dimension_semantics=("arbitrary", "arbitrary")),
        name="final_norm",
    )(x, g.reshape(1, d).astype(F32))


def _moe_kernel(meta_ref, rows_cur, rows_nxt, h_hbm, w1_ref, w3_ref, w2_ref, y_ref, xbuf, sem, *, n_blocks):
    i = pl.program_id(0)
    n_used = meta_ref[n_blocks]
    slot = lax.rem(i, 2)

    def gather(rows_ref, s):
        def body(r, carry):
            tok = rows_ref[0, 0, r]
            pltpu.make_async_copy(h_hbm.at[pl.ds(tok, 1), :], xbuf.at[s, pl.ds(r, 1), :], sem.at[s]).start()
            return carry
        lax.fori_loop(0, MOE_BLOCK, body, 0, unroll=8)

    def wait_slot(s):
        pltpu.make_async_copy(xbuf.at[s], xbuf.at[s], sem.at[s]).wait()

    @pl.when(jnp.logical_and(i == 0, n_used > 0))
    def _():
        gather(rows_cur, 0)

    @pl.when(i < n_used)
    def _():
        wait_slot(slot)
        x = xbuf[slot].astype(BF16)
        for r in range(MOE_BLOCK):
            tok = rows_nxt[0, 0, r]
            pltpu.make_async_copy(h_hbm.at[pl.ds(tok, 1), :], xbuf.at[1 - slot, pl.ds(r, 1), :],
                                  sem.at[1 - slot]).start()
        a = _dot(x, w1_ref[0])
        b = _dot(x, w3_ref[0])
        y_ref[...] = _dot((a * _sigmoid(a) * b).astype(BF16), w2_ref[0]).astype(y_ref.dtype)

        @pl.when(i + 1 >= n_used)
        def _():
            wait_slot(1 - slot)

    @pl.when(i >= n_used)
    def _():
        y_ref[...] = jnp.zeros_like(y_ref)


def _moe_experts(h, row_tok, meta, w1, w3, w2):
    n_rows = row_tok.shape[0]
    d = h.shape[1]
    n_blocks = n_rows // MOE_BLOCK
    de = w1.shape[2]
    rows3 = row_tok.reshape(n_blocks, 1, MOE_BLOCK)
    smem_rows = lambda f: pl.BlockSpec((1, 1, MOE_BLOCK), f, memory_space=pltpu.SMEM)
    return pl.pallas_call(
        functools.partial(_moe_kernel, n_blocks=n_blocks),
        grid_spec=pltpu.PrefetchScalarGridSpec(
            num_scalar_prefetch=1, grid=(n_blocks,),
            in_specs=[smem_rows(lambda i, m: (i, 0, 0)),
                      smem_rows(lambda i, m: (jnp.minimum(i + 1, n_blocks - 1), 0, 0)),
                      pl.BlockSpec(memory_space=pl.ANY),
                      pl.BlockSpec((1, d, de), lambda i, m: (m[i], 0, 0)),
                      pl.BlockSpec((1, d, de), lambda i, m: (m[i], 0, 0)),
                      pl.BlockSpec((1, de, d), lambda i, m: (m[i], 0, 0))],
            out_specs=pl.BlockSpec((MOE_BLOCK, d), lambda i, m: (i, 0)),
            scratch_shapes=[pltpu.VMEM((2, MOE_BLOCK, d), F32), pltpu.SemaphoreType.DMA((2,))]),
        out_shape=jax.ShapeDtypeStruct((n_rows, d), F32),
        compiler_params=pltpu.CompilerParams(dimension_semantics=("arbitrary",), vmem_limit_bytes=VMEM_LIMIT),
        name="moe_experts",
    )(meta, rows3, rows3, h, w1, w3, w2)


def _moe_ffn(h_all, tok_row, logits, n_groups, n_experts, w1, w3, w2):
    n = logits.shape[0]
    epg = n_experts // n_groups
    group_logits = logits[:, :n_groups]
    g_sel = jnp.argmax(group_logits, axis=-1)
    g_gate = jnp.take_along_axis(jax.nn.softmax(group_logits, axis=-1), g_sel[:, None], axis=1)
    exp_logits = logits[:, n_groups:n_groups + n_experts].reshape(n, n_groups, epg)
    within = jnp.take_along_axis(exp_logits, g_sel[:, None, None], axis=1)[:, 0]
    top_val, top_idx = lax.top_k(within, TOP_K)
    gate = jax.nn.softmax(top_val, axis=-1) * g_gate
    e_flat = (g_sel[:, None] * epg + top_idx).reshape(-1).astype(jnp.int32)
    n_assign = n * TOP_K
    order = jnp.argsort(e_flat)
    rank = jnp.argsort(order).astype(jnp.int32)
    counts = jnp.sum(e_flat[:, None] == jnp.arange(n_experts, dtype=jnp.int32)[None, :], axis=0, dtype=jnp.int32)
    starts = jnp.cumsum(counts) - counts
    padded = (counts + MOE_BLOCK - 1) // MOE_BLOCK * MOE_BLOCK
    pad_ends = jnp.cumsum(padded)
    pad_starts = pad_ends - padded
    n_blocks = -(-n_assign // MOE_BLOCK) + n_experts
    n_rows = n_blocks * MOE_BLOCK
    block_start = jnp.arange(n_blocks, dtype=jnp.int32) * MOE_BLOCK
    block_exp = jnp.minimum(jnp.sum(pad_ends[None, :] <= block_start[:, None], axis=1, dtype=jnp.int32), n_experts - 1)
    rows = jnp.arange(n_rows, dtype=jnp.int32)
    row_exp = jnp.repeat(block_exp, MOE_BLOCK)
    offset = rows - pad_starts[row_exp]
    src = jnp.clip(starts[row_exp] + offset, 0, n_assign - 1)
    row_tok = tok_row(jnp.where(offset < counts[row_exp], order[src].astype(jnp.int32) // TOP_K, 0))
    meta = jnp.concatenate([block_exp, (pad_ends[-1:] // MOE_BLOCK).astype(jnp.int32)])
    y_rows = _moe_experts(h_all, row_tok, meta, w1, w3, w2)
    pos = (pad_starts[e_flat] + rank - starts[e_flat]).reshape(n, TOP_K)
    return gate[:, 0:1] * y_rows[pos[:, 0]] + gate[:, 1:2] * y_rows[pos[:, 1]]


def kernel(x, c, ctx, c_ctx, ada_w, ada_b, norm_mix, norm_ffn, w_in, w_out, s5_lam_re, s5_lam_im, s5_log_dt, s5_b_re, s5_b_im, s5_c_re, s5_c_im, s5_d, s5_w_glu, s5_b_glu, rw_mu, rw_w0, rw_w2, rw_a0, rw_a2, rw_g2, rw_k_k, rw_k_a, rw_r_k, rw_ln_w, rw_ln_b, hg_lb_logits, hg_norm, moe_w_group, moe_b_group, moe_w_expert, moe_b_expert, moe_w1, moe_w3, moe_w2, final_norm):
    bsz, n_lat, d = x.shape
    n_ctx = ctx.shape[1]
    depth = w_in.shape[0]
    s5_width = s5_d.shape[1]
    rw_width = rw_k_k.shape[1]
    rw_in = rw_mu.shape[2]
    hg_width = hg_norm.shape[1]
    n_groups = moe_w_group.shape[2]
    n_experts = moe_w_expert.shape[2]
    t = n_ctx + n_lat
    hg_col0 = -(-rw_in // hg_width)
    s5_off = (hg_col0 + 5) * hg_width
    n_proj = -(-(s5_off + s5_width) // IN_BN) * IN_BN
    pad1 = hg_col0 * hg_width - rw_in
    pad2 = n_proj - s5_off - s5_width

    silu_c = jax.nn.silu(c)
    silu_cc = jax.nn.silu(c_ctx)
    lb_all = jnp.cumsum(jax.nn.softmax(hg_lb_logits.astype(F32), axis=0), axis=0)
    lb_all = lb_all - lb_all[0:1]
    tok = jnp.concatenate([ctx, x], axis=1)
    for l in range(depth):
        last = l == depth - 1
        mod_lat = jnp.split(silu_c @ ada_w[l] + ada_b[l], 6, axis=-1)
        mod_ctx = [jnp.broadcast_to(z, (bsz, d)) for z in jnp.split(silu_cc @ ada_w[l] + ada_b[l], 6, axis=-1)]
        sh_m, sc_m, gt_m, sh_f, sc_f, gt_f = [jnp.stack([zc, zl], axis=1) for zc, zl in zip(mod_ctx, mod_lat)]
        mod_m = jnp.stack([norm_mix[l] * (1.0 + sc_m), sh_m], axis=2)
        mod_f = jnp.stack([norm_ffn[l] * (1.0 + sc_f), sh_f], axis=2)

        w_cols = [w_in[l][:, s5_width:s5_width + rw_in], jnp.zeros((d, pad1), F32),
                  w_in[l][:, s5_width + rw_in:], w_in[l][:, :s5_width], jnp.zeros((d, pad2), F32)]
        p = _in_proj(tok, mod_m, jnp.concatenate(w_cols, axis=1).astype(BF16), n_ctx)

        r, v, kk, g, lw, kd, bb = _rw_prep(p, n_ctx, rw_mu[l], rw_w0[l], rw_w2[l], rw_a0[l], rw_a2[l],
                                           rw_g2[l], rw_k_k[l], rw_k_a[l])
        y2, w1_b, w3_b = _rw_scan(r, lw, kd, v, kk, bb, n_ctx, casts=((moe_w1, l), (moe_w3, l)))
        o2, w2_b = _hg_scan(p, lb_all[l], n_ctx, col0=hg_col0, casts=((moe_w2, l),))
        experts = (w1_b.reshape(moe_w1.shape[1:]), w3_b.reshape(moe_w3.shape[1:]), w2_b.reshape(moe_w2.shape[1:]))
        s5y = _s5_scan(p[:, :, s5_off:s5_off + s5_width], n_ctx, s5_lam_re[l], s5_lam_im[l], s5_log_dt[l],
                       s5_b_re[l], s5_b_im[l], s5_c_re[l], s5_c_im[l], s5_d[l])

        w_router = jnp.concatenate([moe_w_group[l], moe_w_expert[l],
                                    jnp.zeros((d, ROUTER_PAD - n_groups - n_experts), F32)], axis=1)
        b_router = jnp.concatenate([moe_b_group[l], moe_b_expert[l],
                                    jnp.zeros((ROUTER_PAD - n_groups - n_experts,), F32)]).reshape(1, ROUTER_PAD)
        tok, h_ffn, logits = _out_proj(
            s5y, s5_w_glu[l], s5_b_glu[l], y2, r, kd, v, g, rw_r_k[l], rw_ln_w[l], rw_ln_b[l],
            o2, p, hg_col0 + 4, hg_norm[l], tok, gt_m, w_out[l], mod_f, w_router, b_router, n_ctx)

        h_rows = h_ffn.reshape(bsz * t, d)
        if last:
            lat_row = lambda n: n + (n // n_lat + 1) * n_ctx
            f = _moe_ffn(h_rows, lat_row, logits[:, n_ctx:].reshape(bsz * n_lat, ROUTER_PAD),
                         n_groups, n_experts, *experts).reshape(bsz, n_lat, d)
            x = tok[:, n_ctx:] + gt_f[:, 1:2] * f
        else:
            f = _moe_ffn(h_rows, lambda n: n, logits.reshape(bsz * t, ROUTER_PAD),
                         n_groups, n_experts, *experts).reshape(bsz, t, d)
            tok = jnp.concatenate([tok[:, :n_ctx] + gt_f[:, 0:1] * f[:, :n_ctx],
                                   tok[:, n_ctx:] + gt_f[:, 1:2] * f[:, n_ctx:]], axis=1)
    return _final_norm(x, final_norm)
```

```python
import functools
import math

import numpy as np
import jax
import jax.numpy as jnp
from jax import lax
from jax.experimental import pallas as pl
from jax.experimental.pallas import tpu as pltpu

F32 = jnp.float32
BF16 = jnp.bfloat16

GRID_W = 64
CH = GRID_W
HEAD = 64
S5_CH = 16
S5_STATE = 64
S5_CHUNK = 32
RW_LORA = 64
RW_GN_EPS = 64e-5
NORM_EPS = 1e-6
TOP_K = 2
MOE_BLOCK = 256
VMEM_LIMIT = 48 * 1024 * 1024


def _dot(a, b):
    return jnp.dot(a, b, preferred_element_type=F32)


def _dot_nt(a, b):
    return lax.dot_general(a, b, (((1,), (1,)), ((), ())), preferred_element_type=F32)


def _dot_tn(a, b):
    return lax.dot_general(a, b, (((0,), (0,)), ((), ())), preferred_element_type=F32)


def _split3(x):
    hi = x.astype(BF16)
    r1 = x - hi.astype(F32)
    mid = r1.astype(BF16)
    lo = (r1 - mid.astype(F32)).astype(BF16)
    return hi, mid, lo


def _dot_sel(m01, x):
    hi, mid, lo = _split3(x)
    return _dot(m01, hi) + _dot(m01, mid) + _dot(m01, lo)


def _dot_hp(a, b):
    a_hi = a.astype(BF16)
    a_lo = (a - a_hi.astype(F32)).astype(BF16)
    b_hi = b.astype(BF16)
    b_lo = (b - b_hi.astype(F32)).astype(BF16)
    return _dot(a_hi, b_hi) + _dot(a_hi, b_lo) + _dot(a_lo, b_hi)


def _sigmoid(x):
    return 1.0 / (1.0 + jnp.exp(-x))


def _scan_chunk(d, i, ncc, nc):
    back = jnp.where(i < ncc, ncc - 1 - i, nc + ncc - 1 - i)
    return jnp.where(d == 0, i, back)


def _half_block_masks(row, col, sgn):
    masks = []
    n = 2
    while n <= CH:
        same = (row // n) == (col // n)
        t_hi = ((row % n) >= n // 2).astype(jnp.int32)
        s_hi = ((col % n) >= n // 2).astype(jnp.int32)
        masks.append(jnp.logical_and(same, (t_hi - s_hi) * sgn == 1))
        n *= 2
    return masks


def _cast_plan(w, layer, n_steps):
    _, e, r, c = w.shape
    total = e * r
    rows = 16
    while rows * n_steps < total or total % rows:
        rows *= 2
    last = total // rows - 1
    slab = lambda d, i: jnp.minimum(d * (n_steps // 2) + i, last)
    return (w.reshape(w.shape[0], total, c),
            pl.BlockSpec((1, rows, c), lambda d, i: (layer, slab(d, i), 0)),
            pl.BlockSpec((rows, c), lambda d, i: (slab(d, i), 0)),
            jax.ShapeDtypeStruct((total, c), BF16))


def _cast_slabs(in_refs, out_refs):
    for wi, wo in zip(in_refs, out_refs):
        wo[...] = wi[0].astype(wo.dtype)


def _head_block_diag(width):
    idx = np.arange(width) // HEAD
    return jnp.asarray(idx[:, None] == idx[None, :], dtype=BF16)


def _rw_prep_kernel(cur_ref, prev_ref, next_ref, mu_ref, w0_ref, w2_ref, a0_ref, a2_ref, g2_ref,
                    kkw_ref, ka_ref, bd_ref,
                    r_out, v_out, kk_out, g_out, lw_out, kd_out, bb_out, *, ncc, nc, width):
    n = pl.program_id(1)
    cur = cur_ref[0].astype(F32)
    prv = prev_ref[0].astype(F32)
    nxt = next_ref[0].astype(F32)
    is_ctx = n < ncc
    row = lax.broadcasted_iota(jnp.int32, (CH, 1), 0)
    zero_row = jnp.zeros((1, cur.shape[1]), F32)
    first_fill = jnp.where(jnp.logical_and(is_ctx, n > 0), prv[CH - 1:CH, :], zero_row)
    last_fill = jnp.where(jnp.logical_and(is_ctx, n < ncc - 1), nxt[0:1, :], zero_row)
    nb_prev = jnp.where(row == 0, first_fill, pltpu.roll(cur, 1, 0))
    nb_next = jnp.where(row == CH - 1, last_fill, pltpu.roll(cur, CH - 1, 0))
    mu = mu_ref[...]
    zero_mu = jnp.zeros_like(mu[0:1])
    w_up = jnp.where(jnp.logical_and(jnp.logical_not(is_ctx), n > ncc), mu[2:3], zero_mu)
    w_down = jnp.where(jnp.logical_and(jnp.logical_not(is_ctx), n < nc - 1), mu[3:4], zero_mu)
    w_cur = 1.0 - mu[0:1] - mu[1:2] - jnp.where(is_ctx, zero_mu, mu[2:3] + mu[3:4])
    x = w_cur * cur + mu[0:1] * nb_prev + mu[1:2] * nb_next + w_up * prv + w_down * nxt

    w = width
    r = x[:, 0:w]
    k = x[:, w:2 * w]
    v = x[:, 2 * w:3 * w]
    wd = x[:, 3 * w:3 * w + 2 * RW_LORA]
    ad = x[:, 3 * w + 2 * RW_LORA:3 * w + 4 * RW_LORA]
    gd = x[:, 3 * w + 4 * RW_LORA:]

    r_out[0] = r.astype(r_out.dtype)
    v_out[0] = v.astype(v_out.dtype)
    g_out[0] = _dot_hp(_sigmoid(gd), g2_ref[...]).astype(g_out.dtype)
    kk = k * kkw_ref[...]
    kk2 = kk * kk
    kk2_hi = kk2.astype(BF16)
    kk2_lo = (kk2 - kk2_hi.astype(F32)).astype(BF16)
    ss = _dot(kk2_hi, bd_ref[...]) + _dot(kk2_lo, bd_ref[...])
    kk = kk / jnp.maximum(jnp.sqrt(ss), 1e-12)
    kk_out[0] = kk.astype(kk_out.dtype)
    ka = ka_ref[...]
    for d in range(2):
        w_raw = w0_ref[d:d + 1, :] + _dot_hp(jnp.tanh(wd[:, d * RW_LORA:(d + 1) * RW_LORA]), w2_ref[d])
        nw = -w_raw
        softplus = jnp.maximum(nw, 0.0) + jnp.log(1.0 + jnp.exp(-jnp.abs(nw)))
        lw_out[d, 0] = -jnp.exp(-softplus - 0.5)
        a = _sigmoid(a0_ref[d:d + 1, :] + _dot_hp(ad[:, d * RW_LORA:(d + 1) * RW_LORA], a2_ref[d]))
        kd_out[d, 0] = (k * (1.0 + (a - 1.0) * ka)).astype(kd_out.dtype)
        bb_out[d, 0] = (kk * a).astype(bb_out.dtype)


def _rw_prep(p_rw, n_ctx, mu, w0, w2, a0, a2, g2, k_k, k_a):
    bsz, t, _ = p_rw.shape
    win = mu.shape[1]
    width = k_k.shape[0]
    nc = t // CH
    ncc = n_ctx // CH
    tok = lambda b, n: (b, n, 0)
    full2 = lambda b, n: (0, 0)
    full3 = lambda b, n: (0, 0, 0)
    blk_in = (1, CH, win)
    out1 = jax.ShapeDtypeStruct((bsz, t, width), BF16)
    out2 = jax.ShapeDtypeStruct((2, bsz, t, width), BF16)
    out_lw = jax.ShapeDtypeStruct((2, bsz, t, width), F32)
    spec1 = pl.BlockSpec((1, CH, width), tok)
    spec2 = pl.BlockSpec((2, 1, CH, width), lambda b, n: (0, b, n, 0))
    return pl.pallas_call(
        functools.partial(_rw_prep_kernel, ncc=ncc, nc=nc, width=width),
        grid=(bsz, nc),
        in_specs=[
            pl.BlockSpec(blk_in, tok),
            pl.BlockSpec(blk_in, lambda b, n: (b, jnp.maximum(n - 1, 0), 0)),
            pl.BlockSpec(blk_in, lambda b, n: (b, jnp.minimum(n + 1, nc - 1), 0)),
            pl.BlockSpec(mu.shape, full2),
            pl.BlockSpec(w0.shape, full2),
            pl.BlockSpec(w2.shape, full3),
            pl.BlockSpec(a0.shape, full2),
            pl.BlockSpec(a2.shape, full3),
            pl.BlockSpec(g2.shape, full2),
            pl.BlockSpec((1, width), full2),
            pl.BlockSpec((1, width), full2),
            pl.BlockSpec((width, width), full2),
        ],
        out_specs=[spec1, spec1, spec1, spec1, spec2, spec2, spec2],
        out_shape=[out1, out1, out1, out1, out_lw, out2, out2],
        compiler_params=pltpu.CompilerParams(
            dimension_semantics=("arbitrary", "arbitrary"), vmem_limit_bytes=VMEM_LIMIT),
        name="rwkv_prep",
    )(p_rw, p_rw, p_rw, mu, w0, w2, a0, a2, g2, k_k.reshape(1, width), k_a.reshape(1, width),
      _head_block_diag(width))


def _rw_scan_kernel(r_ref, lw_ref, kd_ref, v_ref, kk_ref, bb_ref, *rest, heads, bsz, n_cast):
    y_ref, s_ref = rest[n_cast], rest[-1]
    _cast_slabs(rest[:n_cast], rest[n_cast + 1:-1])
    d = pl.program_id(0)
    i = pl.program_id(1)

    @pl.when(i == 0)
    def _():
        s_ref[...] = jnp.zeros_like(s_ref)

    row = lax.broadcasted_iota(jnp.int32, (CH, CH), 0)
    col = lax.broadcasted_iota(jnp.int32, (CH, CH), 1)
    sgn = jnp.where(d == 0, 1, -1)
    strict = (row - col) * sgn > 0
    diag = row == col
    incl = jnp.logical_or(strict, diag)
    tri = jnp.where(incl, 1.0, 0.0).astype(BF16)
    eye = jnp.where(diag, 1.0, 0.0)
    halves = _half_block_masks(row, col, sgn)

    hs = range(bsz * heads)
    sls = [slice(h * HEAD, (h + 1) * HEAD) for h in range(heads)] * bsz
    s0 = [s_ref[h] for h in hs]
    v, kkr, bk, b_w, k_w, wtot = [], [], [], [], [], []
    for b in range(bsz):
        lw = lw_ref[0, b]
        cw = _dot_sel(tri, lw)
        ctot = jnp.sum(lw, axis=0, keepdims=True)
        e_ncw = jnp.exp(-cw)
        e_rem = jnp.exp(ctot - cw)
        r_t = (r_ref[b].astype(F32) * jnp.exp(cw)).astype(BF16)
        kk_t = (kk_ref[b].astype(F32) * jnp.exp(cw - lw)).astype(BF16)
        bb = bb_ref[0, b].astype(F32)
        kd = kd_ref[0, b].astype(F32)
        b_t = (bb * e_ncw).astype(BF16)
        k_t = (kd * e_ncw).astype(BF16)
        bw_all = (bb * e_rem).astype(BF16)
        kw_all = (kd * e_rem).astype(BF16)
        wt_all = jnp.exp(ctot)
        v_all = v_ref[b]
        for sl in sls[:heads]:
            v.append(v_all[:, sl])
            kkr.append(jnp.concatenate([kk_t[:, sl], r_t[:, sl]], axis=0))
            bk.append(jnp.concatenate([b_t[:, sl], k_t[:, sl]], axis=0))
            b_w.append(bw_all[:, sl])
            k_w.append(kw_all[:, sl])
            wtot.append(wt_all[:, sl])
    att = [_dot_nt(kkr[h], bk[h]) for h in hs]
    proj = [_dot_nt(kkr[h], s0[h].astype(BF16)) for h in hs]
    a_ab = [jnp.where(strict, att[h][:CH, :CH], 0.0) for h in hs]
    a_ak = [jnp.where(strict, att[h][:CH, CH:], 0.0).astype(BF16) for h in hs]
    a_rb = [jnp.where(incl, att[h][CH:, :CH], 0.0).astype(BF16) for h in hs]
    a_rk = [jnp.where(incl, att[h][CH:, CH:], 0.0).astype(BF16) for h in hs]
    inv = [eye - jnp.where(halves[0], a_ab[h], 0.0) for h in hs]
    for lvl in range(1, len(halves)):
        inv_b = [inv[h].astype(BF16) for h in hs]
        a_l = [jnp.where(halves[lvl], a_ab[h], 0.0).astype(BF16) for h in hs]
        left = [_dot(inv_b[h], a_l[h]).astype(BF16) for h in hs]
        inv = [inv[h] - _dot(left[h], inv_b[h]) for h in hs]
    rhs = [(proj[h][:CH] + _dot(a_ak[h], v[h])).astype(BF16) for h in hs]
    u_b = [(-_dot(inv[h].astype(BF16), rhs[h])).astype(BF16) for h in hs]
    y = [proj[h][CH:] + _dot(a_rb[h], u_b[h]) + _dot(a_rk[h], v[h]) for h in hs]
    s_new = [s0[h] * wtot[h] + _dot_tn(u_b[h], b_w[h]) + _dot_tn(v[h], k_w[h]) for h in hs]
    for h in hs:
        y_ref[0, h // heads, :, sls[h]] = y[h].astype(y_ref.dtype)
        s_ref[h] = s_new[h]


def _rw_scan(r, lw, kd, v, kk, bb, n_ctx, casts=()):
    bsz, t, width = r.shape
    heads = width // HEAD
    nc = t // CH
    ncc = n_ctx // CH
    s1 = pl.BlockSpec((bsz, CH, width), lambda d, i: (0, _scan_chunk(d, i, ncc, nc), 0))
    s2 = pl.BlockSpec((1, bsz, CH, width), lambda d, i: (d, 0, _scan_chunk(d, i, ncc, nc), 0))
    plans = [_cast_plan(w, layer, 2 * nc) for w, layer in casts]
    return pl.pallas_call(
        functools.partial(_rw_scan_kernel, heads=heads, bsz=bsz, n_cast=len(plans)),
        grid=(2, nc),
        in_specs=[s1, s2, s2, s1, s1, s2] + [p[1] for p in plans],
        out_specs=[s2] + [p[2] for p in plans],
        out_shape=[jax.ShapeDtypeStruct((2, bsz, t, width), BF16)] + [p[3] for p in plans],
        scratch_shapes=[pltpu.VMEM((bsz * heads, HEAD, HEAD), F32)],
        compiler_params=pltpu.CompilerParams(
            dimension_semantics=("arbitrary", "arbitrary"), vmem_limit_bytes=VMEM_LIMIT),
        name="rwkv_scan",
    )(r, lw, kd, v, kk, bb, *[p[0] for p in plans])


HG_LEVELS = 6


def _hg_decay_matrices():
    out = np.zeros((2, HG_LEVELS + 2, CH, CH), np.float32)
    pos = np.arange(CH)
    for d in range(2):
        for t in range(CH):
            if d == 0:
                out[d, 0, t, :t + 1] = 1.0
                out[d, 1, t, t + 1:] = 1.0
            else:
                out[d, 0, t, t:] = 1.0
                out[d, 1, t, :t] = 1.0
            for l in range(1, HG_LEVELS + 1):
                n = 1 << l
                mid = (t // n) * n + n // 2
                if d == 0:
                    sel = (pos >= mid) & (pos <= t) if t >= mid else (pos > t) & (pos < mid)
                else:
                    sel = (pos >= t) & (pos < mid) if t < mid else (pos >= mid) & (pos < t)
                out[d, 1 + l, t, sel] = 1.0
    return jnp.asarray(out.reshape(2, (HG_LEVELS + 2) * CH, CH), dtype=BF16)


def _hg_scan_kernel(q_ref, f_ref, i_ref, lb_ref, dm_ref, *rest, heads, bsz, n_cast):
    o_ref, s_ref = rest[n_cast], rest[-1]
    _cast_slabs(rest[:n_cast], rest[n_cast + 1:-1])
    d = pl.program_id(0)
    step = pl.program_id(1)

    @pl.when(step == 0)
    def _():
        s_ref[...] = jnp.zeros_like(s_ref)

    row = lax.broadcasted_iota(jnp.int32, (CH, CH), 0)
    col = lax.broadcasted_iota(jnp.int32, (CH, CH), 1)
    sgn = jnp.where(d == 0, 1, -1)

    lb = lb_ref[...]
    masks = _half_block_masks(row, col, sgn)
    lvl_masks = [row == col] + masks
    hs = range(bsz * heads)
    sls = [slice(h * HEAD, (h + 1) * HEAD) for h in range(heads)] * bsz
    q_lvl, k_lvl, q_in, k_out, etot, v_all = [], [], [], [], [], []
    for b in range(bsz):
        f = f_ref[b].astype(F32)
        q = q_ref[b].astype(F32)
        q = q * _sigmoid(q)
        log_sig = jnp.minimum(f, 0.0) - jnp.log(1.0 + jnp.exp(-jnp.abs(f)))
        hi_term = jnp.log(1.0 - lb) + log_sig
        lo_term = jnp.log(jnp.where(lb > 0.0, lb, 1.0))
        mx = jnp.maximum(hi_term, lo_term)
        lf_mix = mx + jnp.log(1.0 + jnp.exp(-jnp.abs(hi_term - lo_term)))
        lf = jnp.where(lb > 0.0, lf_mix, hi_term)
        key = (1.0 - lb) * _sigmoid(-f)
        ex = jnp.exp(_dot_sel(dm_ref[0], lf))
        q_in.append((q * ex[0:CH]).astype(BF16))
        k_out.append((key * ex[CH:2 * CH]).astype(BF16))
        etot.append(jnp.exp(jnp.sum(lf, axis=0, keepdims=True)))
        v_all.append(i_ref[b].astype(BF16))
        q_lvl.append([q.astype(BF16)] + [(q * ex[(1 + l) * CH:(2 + l) * CH]).astype(BF16)
                                         for l in range(1, HG_LEVELS + 1)])
        k_lvl.append([key.astype(BF16)] + [(key * ex[(1 + l) * CH:(2 + l) * CH]).astype(BF16)
                                           for l in range(1, HG_LEVELS + 1)])

    bi = [h // heads for h in hs]
    s0 = [s_ref[h] for h in hs]
    sc = [jnp.zeros((CH, CH), F32) for _ in hs]
    for l in range(HG_LEVELS + 1):
        sc = [sc[h] + jnp.where(lvl_masks[l], _dot_nt(q_lvl[bi[h]][l][:, sls[h]], k_lvl[bi[h]][l][:, sls[h]]), 0.0)
              for h in hs]
    o = [_dot(sc[h].astype(BF16), v_all[bi[h]][:, sls[h]]) + _dot_nt(q_in[bi[h]][:, sls[h]], s0[h].astype(BF16))
         for h in hs]
    s_new = [s0[h] * etot[bi[h]][:, sls[h]] + _dot_tn(v_all[bi[h]][:, sls[h]], k_out[bi[h]][:, sls[h]]) for h in hs]
    for h in hs:
        o_ref[0, bi[h], :, sls[h]] = o[h].astype(o_ref.dtype)
        s_ref[h] = s_new[h]


def _hg_scan(p_hg, lb, n_ctx, col0=0, casts=()):
    bsz, t, _ = p_hg.shape
    width = lb.shape[0]
    heads = width // HEAD
    nc = t // CH
    ncc = n_ctx // CH
    dm = _hg_decay_matrices()
    chunk = lambda d, i: _scan_chunk(d, i, ncc, nc)
    blk = (bsz, CH, width)
    plans = [_cast_plan(w, layer, 2 * nc) for w, layer in casts]
    return pl.pallas_call(
        functools.partial(_hg_scan_kernel, heads=heads, bsz=bsz, n_cast=len(plans)),
        grid=(2, nc),
        in_specs=[
            pl.BlockSpec(blk, lambda d, i: (0, chunk(d, i), col0)),
            pl.BlockSpec(blk, lambda d, i: (0, chunk(d, i), col0 + 1 + d)),
            pl.BlockSpec(blk, lambda d, i: (0, chunk(d, i), col0 + 3)),
            pl.BlockSpec((1, width), lambda d, i: (0, 0)),
            pl.BlockSpec((1,) + dm.shape[1:], lambda d, i: (d, 0, 0)),
        ] + [p[1] for p in plans],
        out_specs=[pl.BlockSpec((1, bsz, CH, width), lambda d, i: (d, 0, chunk(d, i), 0))] + [p[2] for p in plans],
        out_shape=[jax.ShapeDtypeStruct((2, bsz, t, width), BF16)] + [p[3] for p in plans],
        scratch_shapes=[pltpu.VMEM((bsz * heads, HEAD, HEAD), F32)],
        compiler_params=pltpu.CompilerParams(
            dimension_semantics=("arbitrary", "arbitrary"), vmem_limit_bytes=VMEM_LIMIT),
        name="hgrn2_scan",
    )(p_hg, p_hg, p_hg, lb.reshape(1, width), dm, *[p[0] for p in plans])


def _s5_operators(lam_re, lam_im, log_dt, b_re, b_im, c_re, c_im):
    c = S5_CHUNK
    hp = lax.Precision.HIGHEST
    dt = jnp.exp(log_dt.astype(F32))[..., None]
    lam_re = lam_re.astype(F32)
    lam_im = lam_im.astype(F32)
    mag = jnp.exp(lam_re * dt)
    ang = lam_im * dt
    a_re, a_im = mag * jnp.cos(ang), mag * jnp.sin(ang)
    den = lam_re * lam_re + lam_im * lam_im
    f_re = ((a_re - 1.0) * lam_re + a_im * lam_im) / den
    f_im = (a_im * lam_re - (a_re - 1.0) * lam_im) / den
    bb_re = f_re[..., None] * b_re - f_im[..., None] * b_im
    bb_im = f_re[..., None] * b_im + f_im[..., None] * b_re
    j = jnp.arange(c + 1, dtype=F32)[None, None, :, None]
    pmag = jnp.exp((lam_re * dt)[:, :, None, :] * j)
    pang = ang[:, :, None, :] * j
    pw_re, pw_im = pmag * jnp.cos(pang), pmag * jnp.sin(pang)
    ca_re = c_re[:, :, None] * pw_re[:, :, :, None] - c_im[:, :, None] * pw_im[:, :, :, None]
    ca_im = c_re[:, :, None] * pw_im[:, :, :, None] + c_im[:, :, None] * pw_re[:, :, :, None]
    kern = (jnp.einsum('dgjcp,dgpk->dgjck', ca_re, bb_re, precision=hp)
            - jnp.einsum('dgjcp,dgpk->dgjck', ca_im, bb_im, precision=hp))
    g = kern.shape[1]
    by_lag = jnp.concatenate([kern[1][:, c - 1:0:-1], kern[0][:, 0:1] + kern[1][:, 0:1], kern[0][:, 1:c]], axis=1)
    tpos = np.arange(c)
    lag = tpos[None, :] - tpos[:, None]
    onehot = jnp.asarray((lag[None] + c - 1) == np.arange(2 * c - 1)[:, None, None], dtype=F32)
    tmat = jnp.einsum('gjoc,jab->gacbo', by_lag, onehot, precision=hp).reshape(g, c * S5_CH, c * S5_CH)

    def in_to_state(pr, pi, d):
        hr = pr[:, :, None, :] * jnp.swapaxes(bb_re[d], 1, 2)[:, None] - pi[:, :, None, :] * jnp.swapaxes(bb_im[d], 1, 2)[:, None]
        hi = pr[:, :, None, :] * jnp.swapaxes(bb_im[d], 1, 2)[:, None] + pi[:, :, None, :] * jnp.swapaxes(bb_re[d], 1, 2)[:, None]
        return hr.reshape(g, c * S5_CH, -1), hi.reshape(g, c * S5_CH, -1)

    hf_re, hf_im = in_to_state(pw_re[0, :, c - 1::-1][:, :c], pw_im[0, :, c - 1::-1][:, :c], 0)
    hb_re, hb_im = in_to_state(pw_re[1, :, :c], pw_im[1, :, :c], 1)
    hmat = jnp.concatenate([hf_re, hb_re, hf_im, hb_im], axis=-1)

    def state_to_out(cr, ci):
        return (jnp.transpose(cr, (0, 3, 1, 2)).reshape(g, -1, c * S5_CH),
                -jnp.transpose(ci, (0, 3, 1, 2)).reshape(g, -1, c * S5_CH))

    gf_re, gf_im = state_to_out(ca_re[0, :, 1:c + 1], ca_im[0, :, 1:c + 1])
    gb_re, gb_im = state_to_out(ca_re[1, :, c:0:-1], ca_im[1, :, c:0:-1])
    z = jnp.zeros_like(gf_re)
    gf = jnp.concatenate([gf_re, z, gf_im, z], axis=1)
    gb = jnp.concatenate([z, gb_re, z, gb_im], axis=1)
    ac_re = jnp.concatenate([pw_re[0, :, c], pw_re[1, :, c]], axis=-1)[:, None, :]
    ac_im = jnp.concatenate([pw_im[0, :, c], pw_im[1, :, c]], axis=-1)[:, None, :]
    return tmat.astype(BF16), hmat.astype(BF16), gf.astype(BF16), gb.astype(BF16), ac_re, ac_im


def _s5_local_kernel(u_ref, h_ref, loc_ref):
    loc_ref[0] = _dot(u_ref[0].astype(BF16), h_ref[0])


def _s5_state_kernel(loc_ref, ar_ref, ai_ref, sf_ref, sb_ref, *, nc, ncc, bsz):
    ar = ar_ref[0]
    ai = ai_ref[0]
    half = ar.shape[-1]
    is_fwd = lax.broadcasted_iota(jnp.int32, (1, half), 1) < half // 2

    def body(i, carry):
        nb = jnp.where(i < ncc, ncc - 1 - i, nc + ncc - 1 - i)
        new = []
        for b in range(bsz):
            re, im = carry[b]
            st = jnp.concatenate([re, im], axis=1)
            sf_ref[0, b, pl.ds(i, 1), :] = st
            sb_ref[0, b, pl.ds(nb, 1), :] = st
            lf = loc_ref[0, b, pl.ds(i, 1), :]
            lk = loc_ref[0, b, pl.ds(nb, 1), :]
            l_re = jnp.where(is_fwd, lf[:, :half], lk[:, :half])
            l_im = jnp.where(is_fwd, lf[:, half:], lk[:, half:])
            new.append((ar * re - ai * im + l_re, ar * im + ai * re + l_im))
        return tuple(new)

    zero = jnp.zeros((1, half), F32)
    lax.fori_loop(0, nc, body, tuple((zero, zero) for _ in range(bsz)))


def _s5_out_kernel(u_ref, t_ref, sf_ref, gf_ref, sb_ref, gb_ref, d_ref, y_ref):
    u = u_ref[0]
    y = (_dot(u.astype(BF16), t_ref[0]) + _dot(sf_ref[0].astype(BF16), gf_ref[0])
         + _dot(sb_ref[0].astype(BF16), gb_ref[0]) + d_ref[0] * u.astype(F32))
    gelu = 0.5 * y * (1.0 + jnp.tanh(math.sqrt(2.0 / math.pi) * (y + 0.044715 * (y * y * y))))
    y_ref[0] = gelu.astype(y_ref.dtype)


def _s5_scan(p_s5, n_ctx, lam_re, lam_im, log_dt, b_re, b_im, c_re, c_im, d_skip):
    bsz, t, width = p_s5.shape
    g = width // S5_CH
    c = S5_CHUNK
    nc, ncc = t // c, n_ctx // c
    blk = c * S5_CH
    rows = bsz * nc
    tmat, hmat, gf, gb, ac_re, ac_im = _s5_operators(lam_re, lam_im, log_dt, b_re, b_im, c_re, c_im)
    u = jnp.transpose(p_s5.astype(F32).reshape(bsz, nc, c, g, S5_CH), (3, 0, 1, 2, 4)).reshape(g, rows, blk)
    per_g = lambda gi: (gi, 0, 0)
    cparams = pltpu.CompilerParams(dimension_semantics=("arbitrary",), vmem_limit_bytes=VMEM_LIMIT)
    loc = pl.pallas_call(
        _s5_local_kernel, grid=(g,),
        in_specs=[pl.BlockSpec((1, rows, blk), per_g), pl.BlockSpec((1, blk, 256), per_g)],
        out_specs=pl.BlockSpec((1, rows, 256), per_g),
        out_shape=jax.ShapeDtypeStruct((g, rows, 256), F32),
        compiler_params=cparams, name="s5_local",
    )(u, hmat)
    per_g4 = lambda gi: (gi, 0, 0, 0)
    st_shape = jax.ShapeDtypeStruct((g, bsz, nc, 256), F32)
    sf, sb = pl.pallas_call(
        functools.partial(_s5_state_kernel, nc=nc, ncc=ncc, bsz=bsz), grid=(g,),
        in_specs=[pl.BlockSpec((1, bsz, nc, 256), per_g4), pl.BlockSpec((1, 1, 128), per_g),
                  pl.BlockSpec((1, 1, 128), per_g)],
        out_specs=[pl.BlockSpec((1, bsz, nc, 256), per_g4)] * 2,
        out_shape=[st_shape, st_shape],
        compiler_params=cparams, name="s5_state",
    )(loc.reshape(g, bsz, nc, 256), ac_re, ac_im)
    d_flat = jnp.tile(d_skip.astype(F32).reshape(g, 1, S5_CH), (1, c, 1)).reshape(g, 1, blk)
    y = pl.pallas_call(
        _s5_out_kernel, grid=(g,),
        in_specs=[pl.BlockSpec((1, rows, blk), per_g), pl.BlockSpec((1, blk, blk), per_g),
                  pl.BlockSpec((1, rows, 256), per_g), pl.BlockSpec((1, 256, blk), per_g),
                  pl.BlockSpec((1, rows, 256), per_g), pl.BlockSpec((1, 256, blk), per_g),
                  pl.BlockSpec((1, 1, blk), per_g)],
        out_specs=pl.BlockSpec((1, rows, blk), per_g),
        out_shape=jax.ShapeDtypeStruct((g, rows, blk), F32),
        compiler_params=cparams, name="s5_out",
    )(u, tmat, sf.reshape(g, rows, 256), gf, sb.reshape(g, rows, 256), gb, d_flat)
    return jnp.transpose(y.reshape(g, bsz, nc, c, S5_CH), (1, 2, 3, 0, 4)).reshape(bsz, t, width)


IN_BM = 1280
IN_BN = 768
OUT_BM = 256
ROUTER_PAD = 128


def _row_block(t, target):
    bm = target
    while t % bm:
        bm -= CH
    return bm


def _norm_mod(x, mod_ref, first_row, n_ctx):
    xn = x * lax.rsqrt(jnp.mean(x * x, axis=-1, keepdims=True) + NORM_EPS)
    is_ctx = (first_row + lax.broadcasted_iota(jnp.int32, (x.shape[0], 1), 0)) < n_ctx
    scale = jnp.where(is_ctx, mod_ref[0, 0, 0:1, :], mod_ref[0, 1, 0:1, :])
    shift = jnp.where(is_ctx, mod_ref[0, 0, 1:2, :], mod_ref[0, 1, 1:2, :])
    return xn * scale + shift


def _in_proj_kernel(x_ref, mod_ref, w_ref, p_ref, h_ref, *, n_ctx, bm):
    @pl.when(pl.program_id(2) == 0)
    def _():
        sub = _row_block(bm, 256)

        def body(k, carry):
            rows = pl.ds(pl.multiple_of(k * sub, sub), sub)
            h = _norm_mod(x_ref[0, rows, :], mod_ref, pl.program_id(1) * bm + k * sub, n_ctx)
            h_ref[rows, :] = h.astype(BF16)
            return carry

        lax.fori_loop(0, bm // sub, body, 0)

    p_ref[0] = _dot(h_ref[...], w_ref[...]).astype(p_ref.dtype)


def _in_proj(tok, mod, w, n_ctx):
    bsz, t, d = tok.shape
    n = w.shape[1]
    bm = _row_block(t, IN_BM)
    return pl.pallas_call(
        functools.partial(_in_proj_kernel, n_ctx=n_ctx, bm=bm),
        grid=(bsz, t // bm, n // IN_BN),
        in_specs=[pl.BlockSpec((1, bm, d), lambda b, i, j: (b, i, 0), pipeline_mode=pl.Buffered(1)),
                  pl.BlockSpec((1, 2, 2, d), lambda b, i, j: (b, 0, 0, 0)),
                  pl.BlockSpec((d, IN_BN), lambda b, i, j: (0, j))],
        out_specs=pl.BlockSpec((1, bm, IN_BN), lambda b, i, j: (b, i, j)),
        out_shape=jax.ShapeDtypeStruct((bsz, t, n), F32),
        scratch_shapes=[pltpu.VMEM((bm, d), BF16)],
        compiler_params=pltpu.CompilerParams(
            dimension_semantics=("arbitrary", "arbitrary", "arbitrary"), vmem_limit_bytes=VMEM_LIMIT),
        name="in_proj",
    )(tok, mod, w)


def _out_proj_kernel(s5_ref, wglu_ref, bglu_ref,
                     y2_ref, r_ref, kd_ref, v_ref, g_ref, rk_ref, lnw_ref, lnb_ref,
                     o2_ref, og_ref, hgn_ref, bd_ref,
                     x_ref, gate_ref, wout_ref, modf_ref, wr_ref, br_ref,
                     xo_ref, h_ref, lg_ref, *, n_ctx, bm):
    first = pl.program_id(1) * bm
    bd = bd_ref[...]
    inv_n = 1.0 / HEAD

    def head_sum(z):
        return _dot(z.astype(BF16), bd)

    ys = s5_ref[0].astype(F32)
    ys = ys * _sigmoid(_dot(ys.astype(BF16), wglu_ref[...]) + bglu_ref[...])
    y = y2_ref[0, 0].astype(F32) + y2_ref[1, 0].astype(F32)
    yc = y - head_sum(y) * inv_n
    var = head_sum(yc * yc) * inv_n
    k_bonus = 0.5 * (kd_ref[0, 0].astype(F32) + kd_ref[1, 0].astype(F32))
    bonus = head_sum(r_ref[0].astype(F32) * k_bonus * rk_ref[...]) * v_ref[0].astype(F32)
    yr = (yc * lax.rsqrt(var + RW_GN_EPS) * lnw_ref[...] + lnb_ref[...] + bonus) * g_ref[0].astype(F32)
    o = o2_ref[0, 0].astype(F32) + o2_ref[1, 0].astype(F32)
    yh = o * lax.rsqrt(head_sum(o * o) * inv_n + NORM_EPS) * hgn_ref[...] * _sigmoid(og_ref[0].astype(F32))

    m = _dot(jnp.concatenate([ys, yr, yh], axis=-1).astype(BF16), wout_ref[...])
    is_ctx = (first + lax.broadcasted_iota(jnp.int32, (bm, 1), 0)) < n_ctx
    x_new = x_ref[0] + jnp.where(is_ctx, gate_ref[0, 0:1, :], gate_ref[0, 1:2, :]) * m
    xo_ref[0] = x_new
    h = _norm_mod(x_new, modf_ref, first, n_ctx)
    h_ref[0] = h
    lg_ref[0] = _dot(h.astype(BF16), wr_ref[...]) + br_ref[...]


def _out_proj(s5y, w_glu, b_glu, y2, r, kd, v, g, r_k, ln_w, ln_b, o2, p, og_col, hg_norm,
              tok, gate, w_out, mod_f, w_router, b_router, n_ctx):
    bsz, t, d = tok.shape
    bm = _row_block(t, OUT_BM)
    s5w = s5y.shape[-1]
    w = r.shape[-1]
    row = lambda b, i: (b, i, 0)
    row2 = lambda b, i: (0, b, i, 0)
    c2 = lambda b, i: (0, 0)
    vec = lambda z: z.reshape(1, -1).astype(F32)
    spec_w = pl.BlockSpec((1, bm, w), row)
    spec_2w = pl.BlockSpec((2, 1, bm, w), row2)
    spec_vw = pl.BlockSpec((1, w), c2)
    return pl.pallas_call(
        functools.partial(_out_proj_kernel, n_ctx=n_ctx, bm=bm),
        grid=(bsz, t // bm),
        in_specs=[
            pl.BlockSpec((1, bm, s5w), row), pl.BlockSpec((s5w, s5w), c2), pl.BlockSpec((1, s5w), c2),
            spec_2w, spec_w, spec_2w, spec_w, spec_w, spec_vw, spec_vw, spec_vw,
            spec_2w, pl.BlockSpec((1, bm, w), lambda b, i: (b, i, og_col)), spec_vw, pl.BlockSpec((w, w), c2),
            pl.BlockSpec((1, bm, d), row), pl.BlockSpec((1, 2, d), lambda b, i: (b, 0, 0)),
            pl.BlockSpec(w_out.shape, c2), pl.BlockSpec((1, 2, 2, d), lambda b, i: (b, 0, 0, 0)),
            pl.BlockSpec((d, ROUTER_PAD), c2), pl.BlockSpec((1, ROUTER_PAD), c2),
        ],
        out_specs=[pl.BlockSpec((1, bm, d), row), pl.BlockSpec((1, bm, d), row),
                   pl.BlockSpec((1, bm, ROUTER_PAD), row)],
        out_shape=[jax.ShapeDtypeStruct((bsz, t, d), F32), jax.ShapeDtypeStruct((bsz, t, d), F32),
                   jax.ShapeDtypeStruct((bsz, t, ROUTER_PAD), F32)],
        compiler_params=pltpu.CompilerParams(
            dimension_semantics=("arbitrary", "arbitrary"), vmem_limit_bytes=VMEM_LIMIT),
        name="out_proj",
    )(s5y, w_glu.astype(BF16), vec(b_glu), y2, r, kd, v, g, vec(r_k), vec(ln_w), vec(ln_b),
      o2, p, vec(hg_norm), _head_block_diag(w), tok, gate, w_out.astype(BF16), mod_f,
      w_router.astype(BF16), b_router)


def _final_norm_kernel(x_ref, g_ref, o_ref):
    x = x_ref[0]
    o_ref[0] = x * lax.rsqrt(jnp.mean(x * x, axis=-1, keepdims=True) + NORM_EPS) * g_ref[...]


def _final_norm(x, g):
    bsz, t, d = x.shape
    bm = _row_block(t, 512)
    return pl.pallas_call(
        _final_norm_kernel, grid=(bsz, t // bm),
        in_specs=[pl.BlockSpec((1, bm, d), lambda b, i: (b, i, 0)), pl.BlockSpec((1, d), lambda b, i: (0, 0))],
        out_specs=pl.BlockSpec((1, bm, d), lambda b, i: (b, i, 0)),
        out_shape=jax.ShapeDtypeStruct((bsz, t, d), F32),
        compiler_params=pltpu.CompilerParams(dimension_semantics=("arbitrary", "arbitrary")),
        name="final_norm",
    )(x, g.reshape(1, d).astype(F32))


def _moe_kernel(meta_ref, rows_cur, rows_nxt, h_hbm, w1_ref, w3_ref, w2_ref, y_ref, xbuf, sem, *, n_blocks):
    i = pl.program_id(0)
    n_used = meta_ref[n_blocks]
    slot = lax.rem(i, 2)

    def gather(rows_ref, s):
        def body(r, carry):
            tok = rows_ref[0, 0, r]
            pltpu.make_async_copy(h_hbm.at[pl.ds(tok, 1), :], xbuf.at[s, pl.ds(r, 1), :], sem.at[s]).start()
            return carry
        lax.fori_loop(0, MOE_BLOCK, body, 0, unroll=8)

    def wait_slot(s):
        pltpu.make_async_copy(xbuf.at[s], xbuf.at[s], sem.at[s]).wait()

    @pl.when(jnp.logical_and(i == 0, n_used > 0))
    def _():
        gather(rows_cur, 0)

    @pl.when(i + 1 < n_used)
    def _():
        gather(rows_nxt, 1 - slot)

    @pl.when(i < n_used)
    def _():
        wait_slot(slot)
        x = xbuf[slot].astype(BF16)
        a = _dot(x, w1_ref[0])
        b = _dot(x, w3_ref[0])
        y_ref[...] = _dot((a * _sigmoid(a) * b).astype(BF16), w2_ref[0]).astype(y_ref.dtype)

    @pl.when(i >= n_used)
    def _():
        y_ref[...] = jnp.zeros_like(y_ref)


def _moe_experts(h, row_tok, meta, w1, w3, w2):
    n_rows = row_tok.shape[0]
    d = h.shape[1]
    n_blocks = n_rows // MOE_BLOCK
    de = w1.shape[2]
    rows3 = row_tok.reshape(n_blocks, 1, MOE_BLOCK)
    smem_rows = lambda f: pl.BlockSpec((1, 1, MOE_BLOCK), f, memory_space=pltpu.SMEM)
    return pl.pallas_call(
        functools.partial(_moe_kernel, n_blocks=n_blocks),
        grid_spec=pltpu.PrefetchScalarGridSpec(
            num_scalar_prefetch=1, grid=(n_blocks,),
            in_specs=[smem_rows(lambda i, m: (i, 0, 0)),
                      smem_rows(lambda i, m: (jnp.minimum(i + 1, n_blocks - 1), 0, 0)),
                      pl.BlockSpec(memory_space=pl.ANY),
                      pl.BlockSpec((1, d, de), lambda i, m: (m[i], 0, 0)),
                      pl.BlockSpec((1, d, de), lambda i, m: (m[i], 0, 0)),
                      pl.BlockSpec((1, de, d), lambda i, m: (m[i], 0, 0))],
            out_specs=pl.BlockSpec((MOE_BLOCK, d), lambda i, m: (i, 0)),
            scratch_shapes=[pltpu.VMEM((2, MOE_BLOCK, d), F32), pltpu.SemaphoreType.DMA((2,))]),
        out_shape=jax.ShapeDtypeStruct((n_rows, d), F32),
        compiler_params=pltpu.CompilerParams(dimension_semantics=("arbitrary",), vmem_limit_bytes=VMEM_LIMIT),
        name="moe_experts",
    )(meta, rows3, rows3, h, w1, w3, w2)


def _moe_ffn(h_all, tok_row, logits, n_groups, n_experts, w1, w3, w2):
    n = logits.shape[0]
    epg = n_experts // n_groups
    group_logits = logits[:, :n_groups]
    g_sel = jnp.argmax(group_logits, axis=-1)
    g_gate = jnp.take_along_axis(jax.nn.softmax(group_logits, axis=-1), g_sel[:, None], axis=1)
    exp_logits = logits[:, n_groups:n_groups + n_experts].reshape(n, n_groups, epg)
    within = jnp.take_along_axis(exp_logits, g_sel[:, None, None], axis=1)[:, 0]
    top_val, top_idx = lax.top_k(within, TOP_K)
    gate = jax.nn.softmax(top_val, axis=-1) * g_gate
    e_flat = (g_sel[:, None] * epg + top_idx).reshape(-1).astype(jnp.int32)
    n_assign = n * TOP_K
    order = jnp.argsort(e_flat)
    rank = jnp.argsort(order).astype(jnp.int32)
    counts = jnp.sum(e_flat[:, None] == jnp.arange(n_experts, dtype=jnp.int32)[None, :], axis=0, dtype=jnp.int32)
    starts = jnp.cumsum(counts) - counts
    padded = (counts + MOE_BLOCK - 1) // MOE_BLOCK * MOE_BLOCK
    pad_ends = jnp.cumsum(padded)
    pad_starts = pad_ends - padded
    n_blocks = -(-n_assign // MOE_BLOCK) + n_experts
    n_rows = n_blocks * MOE_BLOCK
    block_start = jnp.arange(n_blocks, dtype=jnp.int32) * MOE_BLOCK
    block_exp = jnp.minimum(jnp.sum(pad_ends[None, :] <= block_start[:, None], axis=1, dtype=jnp.int32), n_experts - 1)
    rows = jnp.arange(n_rows, dtype=jnp.int32)
    row_exp = jnp.repeat(block_exp, MOE_BLOCK)
    offset = rows - pad_starts[row_exp]
    src = jnp.clip(starts[row_exp] + offset, 0, n_assign - 1)
    row_tok = tok_row(jnp.where(offset < counts[row_exp], order[src].astype(jnp.int32) // TOP_K, 0))
    meta = jnp.concatenate([block_exp, (pad_ends[-1:] // MOE_BLOCK).astype(jnp.int32)])
    y_rows = _moe_experts(h_all, row_tok, meta, w1, w3, w2)
    pos = (pad_starts[e_flat] + rank - starts[e_flat]).reshape(n, TOP_K)
    return gate[:, 0:1] * y_rows[pos[:, 0]] + gate[:, 1:2] * y_rows[pos[:, 1]]


def kernel(x, c, ctx, c_ctx, ada_w, ada_b, norm_mix, norm_ffn, w_in, w_out, s5_lam_re, s5_lam_im, s5_log_dt, s5_b_re, s5_b_im, s5_c_re, s5_c_im, s5_d, s5_w_glu, s5_b_glu, rw_mu, rw_w0, rw_w2, rw_a0, rw_a2, rw_g2, rw_k_k, rw_k_a, rw_r_k, rw_ln_w, rw_ln_b, hg_lb_logits, hg_norm, moe_w_group, moe_b_group, moe_w_expert, moe_b_expert, moe_w1, moe_w3, moe_w2, final_norm):
    bsz, n_lat, d = x.shape
    n_ctx = ctx.shape[1]
    depth = w_in.shape[0]
    s5_width = s5_d.shape[1]
    rw_width = rw_k_k.shape[1]
    rw_in = rw_mu.shape[2]
    hg_width = hg_norm.shape[1]
    n_groups = moe_w_group.shape[2]
    n_experts = moe_w_expert.shape[2]
    t = n_ctx + n_lat
    hg_col0 = -(-rw_in // hg_width)
    s5_off = (hg_col0 + 5) * hg_width
    n_proj = -(-(s5_off + s5_width) // IN_BN) * IN_BN
    pad1 = hg_col0 * hg_width - rw_in
    pad2 = n_proj - s5_off - s5_width

    silu_c = jax.nn.silu(c)
    silu_cc = jax.nn.silu(c_ctx)
    lb_all = jnp.cumsum(jax.nn.softmax(hg_lb_logits.astype(F32), axis=0), axis=0)
    lb_all = lb_all - lb_all[0:1]
    tok = jnp.concatenate([ctx, x], axis=1)
    for l in range(depth):
        last = l == depth - 1
        mod_lat = jnp.split(silu_c @ ada_w[l] + ada_b[l], 6, axis=-1)
        mod_ctx = [jnp.broadcast_to(z, (bsz, d)) for z in jnp.split(silu_cc @ ada_w[l] + ada_b[l], 6, axis=-1)]
        sh_m, sc_m, gt_m, sh_f, sc_f, gt_f = [jnp.stack([zc, zl], axis=1) for zc, zl in zip(mod_ctx, mod_lat)]
        mod_m = jnp.stack([norm_mix[l] * (1.0 + sc_m), sh_m], axis=2)
        mod_f = jnp.stack([norm_ffn[l] * (1.0 + sc_f), sh_f], axis=2)

        w_cols = [w_in[l][:, s5_width:s5_width + rw_in], jnp.zeros((d, pad1), F32),
                  w_in[l][:, s5_width + rw_in:], w_in[l][:, :s5_width], jnp.zeros((d, pad2), F32)]
        p = _in_proj(tok, mod_m, jnp.concatenate(w_cols, axis=1).astype(BF16), n_ctx)

        r, v, kk, g, lw, kd, bb = _rw_prep(p, n_ctx, rw_mu[l], rw_w0[l], rw_w2[l], rw_a0[l], rw_a2[l],
                                           rw_g2[l], rw_k_k[l], rw_k_a[l])
        y2, w1_b, w3_b = _rw_scan(r, lw, kd, v, kk, bb, n_ctx, casts=((moe_w1, l), (moe_w3, l)))
        o2, w2_b = _hg_scan(p, lb_all[l], n_ctx, col0=hg_col0, casts=((moe_w2, l),))
        experts = (w1_b.reshape(moe_w1.shape[1:]), w3_b.reshape(moe_w3.shape[1:]), w2_b.reshape(moe_w2.shape[1:]))
        s5y = _s5_scan(p[:, :, s5_off:s5_off + s5_width], n_ctx, s5_lam_re[l], s5_lam_im[l], s5_log_dt[l],
                       s5_b_re[l], s5_b_im[l], s5_c_re[l], s5_c_im[l], s5_d[l])

        w_router = jnp.concatenate([moe_w_group[l], moe_w_expert[l],
                                    jnp.zeros((d, ROUTER_PAD - n_groups - n_experts), F32)], axis=1)
        b_router = jnp.concatenate([moe_b_group[l], moe_b_expert[l],
                                    jnp.zeros((ROUTER_PAD - n_groups - n_experts,), F32)]).reshape(1, ROUTER_PAD)
        tok, h_ffn, logits = _out_proj(
            s5y, s5_w_glu[l], s5_b_glu[l], y2, r, kd, v, g, rw_r_k[l], rw_ln_w[l], rw_ln_b[l],
            o2, p, hg_col0 + 4, hg_norm[l], tok, gt_m, w_out[l], mod_f, w_router, b_router, n_ctx)

        h_rows = h_ffn.reshape(bsz * t, d)
        if last:
            lat_row = lambda n: n + (n // n_lat + 1) * n_ctx
            f = _moe_ffn(h_rows, lat_row, logits[:, n_ctx:].reshape(bsz * n_lat, ROUTER_PAD),
                         n_groups, n_experts, *experts).reshape(bsz, n_lat, d)
            x = tok[:, n_ctx:] + gt_f[:, 1:2] * f
        else:
            f = _moe_ffn(h_rows, lambda n: n, logits.reshape(bsz * t, ROUTER_PAD),
                         n_groups, n_experts, *experts).reshape(bsz, t, d)
            tok = jnp.concatenate([tok[:, :n_ctx] + gt_f[:, 0:1] * f[:, :n_ctx],
                                   tok[:, n_ctx:] + gt_f[:, 1:2] * f[:, n_ctx:]], axis=1)
    return _final_norm(x, final_norm)
```

```python
import functools
import math

import numpy as np
import jax
import jax.numpy as jnp
from jax import lax
from jax.experimental import pallas as pl
from jax.experimental.pallas import tpu as pltpu

F32 = jnp.float32
BF16 = jnp.bfloat16

GRID_W = 64
CH = GRID_W
HEAD = 64
S5_CH = 16
S5_STATE = 64
S5_CHUNK = 32
RW_LORA = 64
RW_GN_EPS = 64e-5
NORM_EPS = 1e-6
TOP_K = 2
MOE_BLOCK = 256
VMEM_LIMIT = 48 * 1024 * 1024


def _dot(a, b):
    return jnp.dot(a, b, preferred_element_type=F32)


def _dot_nt(a, b):
    return lax.dot_general(a, b, (((1,), (1,)), ((), ())), preferred_element_type=F32)


def _dot_tn(a, b):
    return lax.dot_general(a, b, (((0,), (0,)), ((), ())), preferred_element_type=F32)


def _split3(x):
    hi = x.astype(BF16)
    r1 = x - hi.astype(F32)
    mid = r1.astype(BF16)
    lo = (r1 - mid.astype(F32)).astype(BF16)
    return hi, mid, lo


def _dot_sel(m01, x):
    hi, mid, lo = _split3(x)
    return _dot(m01, hi) + _dot(m01, mid) + _dot(m01, lo)


def _dot_hp(a, b):
    a_hi = a.astype(BF16)
    a_lo = (a - a_hi.astype(F32)).astype(BF16)
    b_hi = b.astype(BF16)
    b_lo = (b - b_hi.astype(F32)).astype(BF16)
    return _dot(a_hi, b_hi) + _dot(a_hi, b_lo) + _dot(a_lo, b_hi)


def _sigmoid(x):
    return 1.0 / (1.0 + jnp.exp(-x))


def _scan_chunk(d, i, ncc, nc):
    back = jnp.where(i < ncc, ncc - 1 - i, nc + ncc - 1 - i)
    return jnp.where(d == 0, i, back)


def _half_block_masks(row, col, sgn):
    masks = []
    n = 2
    while n <= CH:
        same = (row // n) == (col // n)
        t_hi = ((row % n) >= n // 2).astype(jnp.int32)
        s_hi = ((col % n) >= n // 2).astype(jnp.int32)
        masks.append(jnp.logical_and(same, (t_hi - s_hi) * sgn == 1))
        n *= 2
    return masks


def _cast_plan(w, layer, n_steps):
    _, e, r, c = w.shape
    total = e * r
    rows = 16
    while rows * n_steps < total or total % rows:
        rows *= 2
    last = total // rows - 1
    slab = lambda d, i: jnp.minimum(d * (n_steps // 2) + i, last)
    return (w.reshape(w.shape[0], total, c),
            pl.BlockSpec((1, rows, c), lambda d, i: (layer, slab(d, i), 0)),
            pl.BlockSpec((rows, c), lambda d, i: (slab(d, i), 0)),
            jax.ShapeDtypeStruct((total, c), BF16))


def _cast_slabs(in_refs, out_refs):
    for wi, wo in zip(in_refs, out_refs):
        wo[...] = wi[0].astype(wo.dtype)


def _head_block_diag(width):
    idx = np.arange(width) // HEAD
    return jnp.asarray(idx[:, None] == idx[None, :], dtype=BF16)


def _rw_prep_kernel(cur_ref, prev_ref, next_ref, mu_ref, w0_ref, w2_ref, a0_ref, a2_ref, g2_ref,
                    kkw_ref, ka_ref, bd_ref,
                    r_out, v_out, kk_out, g_out, lw_out, kd_out, bb_out, *, ncc, nc, width):
    n = pl.program_id(1)
    cur = cur_ref[0].astype(F32)
    prv = prev_ref[0].astype(F32)
    nxt = next_ref[0].astype(F32)
    is_ctx = n < ncc
    row = lax.broadcasted_iota(jnp.int32, (CH, 1), 0)
    zero_row = jnp.zeros((1, cur.shape[1]), F32)
    first_fill = jnp.where(jnp.logical_and(is_ctx, n > 0), prv[CH - 1:CH, :], zero_row)
    last_fill = jnp.where(jnp.logical_and(is_ctx, n < ncc - 1), nxt[0:1, :], zero_row)
    nb_prev = jnp.where(row == 0, first_fill, pltpu.roll(cur, 1, 0))
    nb_next = jnp.where(row == CH - 1, last_fill, pltpu.roll(cur, CH - 1, 0))
    mu = mu_ref[...]
    zero_mu = jnp.zeros_like(mu[0:1])
    w_up = jnp.where(jnp.logical_and(jnp.logical_not(is_ctx), n > ncc), mu[2:3], zero_mu)
    w_down = jnp.where(jnp.logical_and(jnp.logical_not(is_ctx), n < nc - 1), mu[3:4], zero_mu)
    w_cur = 1.0 - mu[0:1] - mu[1:2] - jnp.where(is_ctx, zero_mu, mu[2:3] + mu[3:4])
    x = w_cur * cur + mu[0:1] * nb_prev + mu[1:2] * nb_next + w_up * prv + w_down * nxt

    w = width
    r = x[:, 0:w]
    k = x[:, w:2 * w]
    v = x[:, 2 * w:3 * w]
    wd = x[:, 3 * w:3 * w + 2 * RW_LORA]
    ad = x[:, 3 * w + 2 * RW_LORA:3 * w + 4 * RW_LORA]
    gd = x[:, 3 * w + 4 * RW_LORA:]

    r_out[0] = r.astype(r_out.dtype)
    v_out[0] = v.astype(v_out.dtype)
    g_out[0] = _dot_hp(_sigmoid(gd), g2_ref[...]).astype(g_out.dtype)
    kk = k * kkw_ref[...]
    kk2 = kk * kk
    kk2_hi = kk2.astype(BF16)
    kk2_lo = (kk2 - kk2_hi.astype(F32)).astype(BF16)
    ss = _dot(kk2_hi, bd_ref[...]) + _dot(kk2_lo, bd_ref[...])
    kk = kk / jnp.maximum(jnp.sqrt(ss), 1e-12)
    kk_out[0] = kk.astype(kk_out.dtype)
    ka = ka_ref[...]
    for d in range(2):
        w_raw = w0_ref[d:d + 1, :] + _dot_hp(jnp.tanh(wd[:, d * RW_LORA:(d + 1) * RW_LORA]), w2_ref[d])
        nw = -w_raw
        softplus = jnp.maximum(nw, 0.0) + jnp.log(1.0 + jnp.exp(-jnp.abs(nw)))
        lw_out[d, 0] = -jnp.exp(-softplus - 0.5)
        a = _sigmoid(a0_ref[d:d + 1, :] + _dot_hp(ad[:, d * RW_LORA:(d + 1) * RW_LORA], a2_ref[d]))
        kd_out[d, 0] = (k * (1.0 + (a - 1.0) * ka)).astype(kd_out.dtype)
        bb_out[d, 0] = (kk * a).astype(bb_out.dtype)


def _rw_prep(p_rw, n_ctx, mu, w0, w2, a0, a2, g2, k_k, k_a):
    bsz, t, _ = p_rw.shape
    win = mu.shape[1]
    width = k_k.shape[0]
    nc = t // CH
    ncc = n_ctx // CH
    tok = lambda b, n: (b, n, 0)
    full2 = lambda b, n: (0, 0)
    full3 = lambda b, n: (0, 0, 0)
    blk_in = (1, CH, win)
    out1 = jax.ShapeDtypeStruct((bsz, t, width), BF16)
    out2 = jax.ShapeDtypeStruct((2, bsz, t, width), BF16)
    out_lw = jax.ShapeDtypeStruct((2, bsz, t, width), F32)
    spec1 = pl.BlockSpec((1, CH, width), tok)
    spec2 = pl.BlockSpec((2, 1, CH, width), lambda b, n: (0, b, n, 0))
    return pl.pallas_call(
        functools.partial(_rw_prep_kernel, ncc=ncc, nc=nc, width=width),
        grid=(bsz, nc),
        in_specs=[
            pl.BlockSpec(blk_in, tok),
            pl.BlockSpec(blk_in, lambda b, n: (b, jnp.maximum(n - 1, 0), 0)),
            pl.BlockSpec(blk_in, lambda b, n: (b, jnp.minimum(n + 1, nc - 1), 0)),
            pl.BlockSpec(mu.shape, full2),
            pl.BlockSpec(w0.shape, full2),
            pl.BlockSpec(w2.shape, full3),
            pl.BlockSpec(a0.shape, full2),
            pl.BlockSpec(a2.shape, full3),
            pl.BlockSpec(g2.shape, full2),
            pl.BlockSpec((1, width), full2),
            pl.BlockSpec((1, width), full2),
            pl.BlockSpec((width, width), full2),
        ],
        out_specs=[spec1, spec1, spec1, spec1, spec2, spec2, spec2],
        out_shape=[out1, out1, out1, out1, out_lw, out2, out2],
        compiler_params=pltpu.CompilerParams(
            dimension_semantics=("arbitrary", "arbitrary"), vmem_limit_bytes=VMEM_LIMIT),
        name="rwkv_prep",
    )(p_rw, p_rw, p_rw, mu, w0, w2, a0, a2, g2, k_k.reshape(1, width), k_a.reshape(1, width),
      _head_block_diag(width))


def _rw_scan_kernel(r_ref, lw_ref, kd_ref, v_ref, kk_ref, bb_ref, *rest, heads, bsz, n_cast):
    y_ref, s_ref = rest[n_cast], rest[-1]
    _cast_slabs(rest[:n_cast], rest[n_cast + 1:-1])
    d = pl.program_id(0)
    i = pl.program_id(1)

    @pl.when(i == 0)
    def _():
        s_ref[...] = jnp.zeros_like(s_ref)

    row = lax.broadcasted_iota(jnp.int32, (CH, CH), 0)
    col = lax.broadcasted_iota(jnp.int32, (CH, CH), 1)
    sgn = jnp.where(d == 0, 1, -1)
    strict = (row - col) * sgn > 0
    diag = row == col
    incl = jnp.logical_or(strict, diag)
    tri = jnp.where(incl, 1.0, 0.0).astype(BF16)
    eye = jnp.where(diag, 1.0, 0.0)
    halves = _half_block_masks(row, col, sgn)

    hs = range(bsz * heads)
    sls = [slice(h * HEAD, (h + 1) * HEAD) for h in range(heads)] * bsz
    s0 = [s_ref[h] for h in hs]
    v, kkr, bk, b_w, k_w, wtot = [], [], [], [], [], []
    for b in range(bsz):
        lw = lw_ref[0, b]
        cw = _dot_sel(tri, lw)
        ctot = jnp.sum(lw, axis=0, keepdims=True)
        e_ncw = jnp.exp(-cw)
        e_rem = jnp.exp(ctot - cw)
        r_t = (r_ref[b].astype(F32) * jnp.exp(cw)).astype(BF16)
        kk_t = (kk_ref[b].astype(F32) * jnp.exp(cw - lw)).astype(BF16)
        bb = bb_ref[0, b].astype(F32)
        kd = kd_ref[0, b].astype(F32)
        b_t = (bb * e_ncw).astype(BF16)
        k_t = (kd * e_ncw).astype(BF16)
        bw_all = (bb * e_rem).astype(BF16)
        kw_all = (kd * e_rem).astype(BF16)
        wt_all = jnp.exp(ctot)
        v_all = v_ref[b]
        for sl in sls[:heads]:
            v.append(v_all[:, sl])
            kkr.append(jnp.concatenate([kk_t[:, sl], r_t[:, sl]], axis=0))
            bk.append(jnp.concatenate([b_t[:, sl], k_t[:, sl]], axis=0))
            b_w.append(bw_all[:, sl])
            k_w.append(kw_all[:, sl])
            wtot.append(wt_all[:, sl])
    att = [_dot_nt(kkr[h], bk[h]) for h in hs]
    proj = [_dot_nt(kkr[h], s0[h].astype(BF16)) for h in hs]
    a_ab = [jnp.where(strict, att[h][:CH, :CH], 0.0) for h in hs]
    a_ak = [jnp.where(strict, att[h][:CH, CH:], 0.0).astype(BF16) for h in hs]
    a_rb = [jnp.where(incl, att[h][CH:, :CH], 0.0).astype(BF16) for h in hs]
    a_rk = [jnp.where(incl, att[h][CH:, CH:], 0.0).astype(BF16) for h in hs]
    inv = [eye - jnp.where(halves[0], a_ab[h], 0.0) for h in hs]
    for lvl in range(1, len(halves)):
        inv_b = [inv[h].astype(BF16) for h in hs]
        a_l = [jnp.where(halves[lvl], a_ab[h], 0.0).astype(BF16) for h in hs]
        left = [_dot(inv_b[h], a_l[h]).astype(BF16) for h in hs]
        inv = [inv[h] - _dot(left[h], inv_b[h]) for h in hs]
    rhs = [(proj[h][:CH] + _dot(a_ak[h], v[h])).astype(BF16) for h in hs]
    u_b = [(-_dot(inv[h].astype(BF16), rhs[h])).astype(BF16) for h in hs]
    y = [proj[h][CH:] + _dot(a_rb[h], u_b[h]) + _dot(a_rk[h], v[h]) for h in hs]
    s_new = [s0[h] * wtot[h] + _dot_tn(u_b[h], b_w[h]) + _dot_tn(v[h], k_w[h]) for h in hs]
    for h in hs:
        y_ref[0, h // heads, :, sls[h]] = y[h].astype(y_ref.dtype)
        s_ref[h] = s_new[h]


def _rw_scan(r, lw, kd, v, kk, bb, n_ctx, casts=()):
    bsz, t, width = r.shape
    heads = width // HEAD
    nc = t // CH
    ncc = n_ctx // CH
    s1 = pl.BlockSpec((bsz, CH, width), lambda d, i: (0, _scan_chunk(d, i, ncc, nc), 0))
    s2 = pl.BlockSpec((1, bsz, CH, width), lambda d, i: (d, 0, _scan_chunk(d, i, ncc, nc), 0))
    plans = [_cast_plan(w, layer, 2 * nc) for w, layer in casts]
    return pl.pallas_call(
        functools.partial(_rw_scan_kernel, heads=heads, bsz=bsz, n_cast=len(plans)),
        grid=(2, nc),
        in_specs=[s1, s2, s2, s1, s1, s2] + [p[1] for p in plans],
        out_specs=[s2] + [p[2] for p in plans],
        out_shape=[jax.ShapeDtypeStruct((2, bsz, t, width), BF16)] + [p[3] for p in plans],
        scratch_shapes=[pltpu.VMEM((bsz * heads, HEAD, HEAD), F32)],
        compiler_params=pltpu.CompilerParams(
            dimension_semantics=("arbitrary", "arbitrary"), vmem_limit_bytes=VMEM_LIMIT),
        name="rwkv_scan",
    )(r, lw, kd, v, kk, bb, *[p[0] for p in plans])


HG_LEVELS = 6


def _hg_decay_matrices():
    out = np.zeros((2, HG_LEVELS + 2, CH, CH), np.float32)
    pos = np.arange(CH)
    for d in range(2):
        for t in range(CH):
            if d == 0:
                out[d, 0, t, :t + 1] = 1.0
                out[d, 1, t, t + 1:] = 1.0
            else:
                out[d, 0, t, t:] = 1.0
                out[d, 1, t, :t] = 1.0
            for l in range(1, HG_LEVELS + 1):
                n = 1 << l
                mid = (t // n) * n + n // 2
                if d == 0:
                    sel = (pos >= mid) & (pos <= t) if t >= mid else (pos > t) & (pos < mid)
                else:
                    sel = (pos >= t) & (pos < mid) if t < mid else (pos >= mid) & (pos < t)
                out[d, 1 + l, t, sel] = 1.0
    return jnp.asarray(out.reshape(2, (HG_LEVELS + 2) * CH, CH), dtype=BF16)


def _hg_scan_kernel(q_ref, f_ref, i_ref, lb_ref, dm_ref, *rest, heads, bsz, n_cast):
    o_ref, s_ref = rest[n_cast], rest[-1]
    _cast_slabs(rest[:n_cast], rest[n_cast + 1:-1])
    d = pl.program_id(0)
    step = pl.program_id(1)

    @pl.when(step == 0)
    def _():
        s_ref[...] = jnp.zeros_like(s_ref)

    row = lax.broadcasted_iota(jnp.int32, (CH, CH), 0)
    col = lax.broadcasted_iota(jnp.int32, (CH, CH), 1)
    sgn = jnp.where(d == 0, 1, -1)

    lb = lb_ref[...]
    masks = _half_block_masks(row, col, sgn)
    lvl_masks = [row == col] + masks
    hs = range(bsz * heads)
    sls = [slice(h * HEAD, (h + 1) * HEAD) for h in range(heads)] * bsz
    q_lvl, k_lvl, q_in, k_out, etot, v_all = [], [], [], [], [], []
    for b in range(bsz):
        f = f_ref[b].astype(F32)
        q = q_ref[b].astype(F32)
        q = q * _sigmoid(q)
        log_sig = jnp.minimum(f, 0.0) - jnp.log(1.0 + jnp.exp(-jnp.abs(f)))
        hi_term = jnp.log(1.0 - lb) + log_sig
        lo_term = jnp.log(jnp.where(lb > 0.0, lb, 1.0))
        mx = jnp.maximum(hi_term, lo_term)
        lf_mix = mx + jnp.log(1.0 + jnp.exp(-jnp.abs(hi_term - lo_term)))
        lf = jnp.where(lb > 0.0, lf_mix, hi_term)
        key = (1.0 - lb) * _sigmoid(-f)
        ex = jnp.exp(_dot_sel(dm_ref[0], lf))
        q_in.append((q * ex[0:CH]).astype(BF16))
        k_out.append((key * ex[CH:2 * CH]).astype(BF16))
        etot.append(jnp.exp(jnp.sum(lf, axis=0, keepdims=True)))
        v_all.append(i_ref[b].astype(BF16))
        q_lvl.append([q.astype(BF16)] + [(q * ex[(1 + l) * CH:(2 + l) * CH]).astype(BF16)
                                         for l in range(1, HG_LEVELS + 1)])
        k_lvl.append([key.astype(BF16)] + [(key * ex[(1 + l) * CH:(2 + l) * CH]).astype(BF16)
                                           for l in range(1, HG_LEVELS + 1)])

    bi = [h // heads for h in hs]
    s0 = [s_ref[h] for h in hs]
    sc = [jnp.zeros((CH, CH), F32) for _ in hs]
    for l in range(HG_LEVELS + 1):
        sc = [sc[h] + jnp.where(lvl_masks[l], _dot_nt(q_lvl[bi[h]][l][:, sls[h]], k_lvl[bi[h]][l][:, sls[h]]), 0.0)
              for h in hs]
    o = [_dot(sc[h].astype(BF16), v_all[bi[h]][:, sls[h]]) + _dot_nt(q_in[bi[h]][:, sls[h]], s0[h].astype(BF16))
         for h in hs]
    s_new = [s0[h] * etot[bi[h]][:, sls[h]] + _dot_tn(v_all[bi[h]][:, sls[h]], k_out[bi[h]][:, sls[h]]) for h in hs]
    for h in hs:
        o_ref[0, bi[h], :, sls[h]] = o[h].astype(o_ref.dtype)
        s_ref[h] = s_new[h]


def _hg_scan(p_hg, lb, n_ctx, col0=0, casts=()):
    bsz, t, _ = p_hg.shape
    width = lb.shape[0]
    heads = width // HEAD
    nc = t // CH
    ncc = n_ctx // CH
    dm = _hg_decay_matrices()
    chunk = lambda d, i: _scan_chunk(d, i, ncc, nc)
    blk = (bsz, CH, width)
    plans = [_cast_plan(w, layer, 2 * nc) for w, layer in casts]
    return pl.pallas_call(
        functools.partial(_hg_scan_kernel, heads=heads, bsz=bsz, n_cast=len(plans)),
        grid=(2, nc),
        in_specs=[
            pl.BlockSpec(blk, lambda d, i: (0, chunk(d, i), col0)),
            pl.BlockSpec(blk, lambda d, i: (0, chunk(d, i), col0 + 1 + d)),
            pl.BlockSpec(blk, lambda d, i: (0, chunk(d, i), col0 + 3)),
            pl.BlockSpec((1, width), lambda d, i: (0, 0)),
            pl.BlockSpec((1,) + dm.shape[1:], lambda d, i: (d, 0, 0)),
        ] + [p[1] for p in plans],
        out_specs=[pl.BlockSpec((1, bsz, CH, width), lambda d, i: (d, 0, chunk(d, i), 0))] + [p[2] for p in plans],
        out_shape=[jax.ShapeDtypeStruct((2, bsz, t, width), BF16)] + [p[3] for p in plans],
        scratch_shapes=[pltpu.VMEM((bsz * heads, HEAD, HEAD), F32)],
        compiler_params=pltpu.CompilerParams(
            dimension_semantics=("arbitrary", "arbitrary"), vmem_limit_bytes=VMEM_LIMIT),
        name="hgrn2_scan",
    )(p_hg, p_hg, p_hg, lb.reshape(1, width), dm, *[p[0] for p in plans])


def _s5_operators(lam_re, lam_im, log_dt, b_re, b_im, c_re, c_im):
    c = S5_CHUNK
    hp = lax.Precision.HIGHEST
    dt = jnp.exp(log_dt.astype(F32))[..., None]
    lam_re = lam_re.astype(F32)
    lam_im = lam_im.astype(F32)
    mag = jnp.exp(lam_re * dt)
    ang = lam_im * dt
    a_re, a_im = mag * jnp.cos(ang), mag * jnp.sin(ang)
    den = lam_re * lam_re + lam_im * lam_im
    f_re = ((a_re - 1.0) * lam_re + a_im * lam_im) / den
    f_im = (a_im * lam_re - (a_re - 1.0) * lam_im) / den
    bb_re = f_re[..., None] * b_re - f_im[..., None] * b_im
    bb_im = f_re[..., None] * b_im + f_im[..., None] * b_re
    j = jnp.arange(c + 1, dtype=F32)[None, None, :, None]
    pmag = jnp.exp((lam_re * dt)[:, :, None, :] * j)
    pang = ang[:, :, None, :] * j
    pw_re, pw_im = pmag * jnp.cos(pang), pmag * jnp.sin(pang)
    ca_re = c_re[:, :, None] * pw_re[:, :, :, None] - c_im[:, :, None] * pw_im[:, :, :, None]
    ca_im = c_re[:, :, None] * pw_im[:, :, :, None] + c_im[:, :, None] * pw_re[:, :, :, None]
    kern = (jnp.einsum('dgjcp,dgpk->dgjck', ca_re, bb_re, precision=hp)
            - jnp.einsum('dgjcp,dgpk->dgjck', ca_im, bb_im, precision=hp))
    g = kern.shape[1]
    by_lag = jnp.concatenate([kern[1][:, c - 1:0:-1], kern[0][:, 0:1] + kern[1][:, 0:1], kern[0][:, 1:c]], axis=1)
    tpos = np.arange(c)
    lag = tpos[None, :] - tpos[:, None]
    onehot = jnp.asarray((lag[None] + c - 1) == np.arange(2 * c - 1)[:, None, None], dtype=F32)
    tmat = jnp.einsum('gjoc,jab->gacbo', by_lag, onehot, precision=hp).reshape(g, c * S5_CH, c * S5_CH)

    def in_to_state(pr, pi, d):
        hr = pr[:, :, None, :] * jnp.swapaxes(bb_re[d], 1, 2)[:, None] - pi[:, :, None, :] * jnp.swapaxes(bb_im[d], 1, 2)[:, None]
        hi = pr[:, :, None, :] * jnp.swapaxes(bb_im[d], 1, 2)[:, None] + pi[:, :, None, :] * jnp.swapaxes(bb_re[d], 1, 2)[:, None]
        return hr.reshape(g, c * S5_CH, -1), hi.reshape(g, c * S5_CH, -1)

    hf_re, hf_im = in_to_state(pw_re[0, :, c - 1::-1][:, :c], pw_im[0, :, c - 1::-1][:, :c], 0)
    hb_re, hb_im = in_to_state(pw_re[1, :, :c], pw_im[1, :, :c], 1)
    hmat = jnp.concatenate([hf_re, hb_re, hf_im, hb_im], axis=-1)

    def state_to_out(cr, ci):
        return (jnp.transpose(cr, (0, 3, 1, 2)).reshape(g, -1, c * S5_CH),
                -jnp.transpose(ci, (0, 3, 1, 2)).reshape(g, -1, c * S5_CH))

    gf_re, gf_im = state_to_out(ca_re[0, :, 1:c + 1], ca_im[0, :, 1:c + 1])
    gb_re, gb_im = state_to_out(ca_re[1, :, c:0:-1], ca_im[1, :, c:0:-1])
    z = jnp.zeros_like(gf_re)
    gf = jnp.concatenate([gf_re, z, gf_im, z], axis=1)
    gb = jnp.concatenate([z, gb_re, z, gb_im], axis=1)
    ac_re = jnp.concatenate([pw_re[0, :, c], pw_re[1, :, c]], axis=-1)[:, None, :]
    ac_im = jnp.concatenate([pw_im[0, :, c], pw_im[1, :, c]], axis=-1)[:, None, :]
    return tmat.astype(BF16), hmat.astype(BF16), gf.astype(BF16), gb.astype(BF16), ac_re, ac_im


def _s5_local_kernel(u_ref, h_ref, loc_ref):
    loc_ref[0] = _dot(u_ref[0].astype(BF16), h_ref[0])


def _s5_state_kernel(loc_ref, ar_ref, ai_ref, sf_ref, sb_ref, *, nc, ncc, bsz):
    ar = ar_ref[0]
    ai = ai_ref[0]
    half = ar.shape[-1]
    is_fwd = lax.broadcasted_iota(jnp.int32, (1, half), 1) < half // 2

    def body(i, carry):
        nb = jnp.where(i < ncc, ncc - 1 - i, nc + ncc - 1 - i)
        new = []
        for b in range(bsz):
            re, im = carry[b]
            st = jnp.concatenate([re, im], axis=1)
            sf_ref[0, b, pl.ds(i, 1), :] = st
            sb_ref[0, b, pl.ds(nb, 1), :] = st
            lf = loc_ref[0, b, pl.ds(i, 1), :]
            lk = loc_ref[0, b, pl.ds(nb, 1), :]
            l_re = jnp.where(is_fwd, lf[:, :half], lk[:, :half])
            l_im = jnp.where(is_fwd, lf[:, half:], lk[:, half:])
            new.append((ar * re - ai * im + l_re, ar * im + ai * re + l_im))
        return tuple(new)

    zero = jnp.zeros((1, half), F32)
    lax.fori_loop(0, nc, body, tuple((zero, zero) for _ in range(bsz)))


def _s5_out_kernel(u_ref, t_ref, sf_ref, gf_ref, sb_ref, gb_ref, d_ref, y_ref):
    u = u_ref[0]
    y = (_dot(u.astype(BF16), t_ref[0]) + _dot(sf_ref[0].astype(BF16), gf_ref[0])
         + _dot(sb_ref[0].astype(BF16), gb_ref[0]) + d_ref[0] * u.astype(F32))
    gelu = 0.5 * y * (1.0 + jnp.tanh(math.sqrt(2.0 / math.pi) * (y + 0.044715 * (y * y * y))))
    y_ref[0] = gelu.astype(y_ref.dtype)


def _s5_scan(p_s5, n_ctx, lam_re, lam_im, log_dt, b_re, b_im, c_re, c_im, d_skip):
    bsz, t, width = p_s5.shape
    g = width // S5_CH
    c = S5_CHUNK
    nc, ncc = t // c, n_ctx // c
    blk = c * S5_CH
    rows = bsz * nc
    tmat, hmat, gf, gb, ac_re, ac_im = _s5_operators(lam_re, lam_im, log_dt, b_re, b_im, c_re, c_im)
    u = jnp.transpose(p_s5.astype(F32).reshape(bsz, nc, c, g, S5_CH), (3, 0, 1, 2, 4)).reshape(g, rows, blk)
    per_g = lambda gi: (gi, 0, 0)
    cparams = pltpu.CompilerParams(dimension_semantics=("arbitrary",), vmem_limit_bytes=VMEM_LIMIT)
    loc = pl.pallas_call(
        _s5_local_kernel, grid=(g,),
        in_specs=[pl.BlockSpec((1, rows, blk), per_g), pl.BlockSpec((1, blk, 256), per_g)],
        out_specs=pl.BlockSpec((1, rows, 256), per_g),
        out_shape=jax.ShapeDtypeStruct((g, rows, 256), F32),
        compiler_params=cparams, name="s5_local",
    )(u, hmat)
    per_g4 = lambda gi: (gi, 0, 0, 0)
    st_shape = jax.ShapeDtypeStruct((g, bsz, nc, 256), F32)
    sf, sb = pl.pallas_call(
        functools.partial(_s5_state_kernel, nc=nc, ncc=ncc, bsz=bsz), grid=(g,),
        in_specs=[pl.BlockSpec((1, bsz, nc, 256), per_g4), pl.BlockSpec((1, 1, 128), per_g),
                  pl.BlockSpec((1, 1, 128), per_g)],
        out_specs=[pl.BlockSpec((1, bsz, nc, 256), per_g4)] * 2,
        out_shape=[st_shape, st_shape],
        compiler_params=cparams, name="s5_state",
    )(loc.reshape(g, bsz, nc, 256), ac_re, ac_im)
    d_flat = jnp.tile(d_skip.astype(F32).reshape(g, 1, S5_CH), (1, c, 1)).reshape(g, 1, blk)
    y = pl.pallas_call(
        _s5_out_kernel, grid=(g,),
        in_specs=[pl.BlockSpec((1, rows, blk), per_g), pl.BlockSpec((1, blk, blk), per_g),
                  pl.BlockSpec((1, rows, 256), per_g), pl.BlockSpec((1, 256, blk), per_g),
                  pl.BlockSpec((1, rows, 256), per_g), pl.BlockSpec((1, 256, blk), per_g),
                  pl.BlockSpec((1, 1, blk), per_g)],
        out_specs=pl.BlockSpec((1, rows, blk), per_g),
        out_shape=jax.ShapeDtypeStruct((g, rows, blk), F32),
        compiler_params=cparams, name="s5_out",
    )(u, tmat, sf.reshape(g, rows, 256), gf, sb.reshape(g, rows, 256), gb, d_flat)
    return jnp.transpose(y.reshape(g, bsz, nc, c, S5_CH), (1, 2, 3, 0, 4)).reshape(bsz, t, width)


IN_BM = 1280
IN_BN = 768
OUT_BM = 256
ROUTER_PAD = 128


def _row_block(t, target):
    bm = target
    while t % bm:
        bm -= CH
    return bm


def _norm_mod(x, mod_ref, first_row, n_ctx):
    xn = x * lax.rsqrt(jnp.mean(x * x, axis=-1, keepdims=True) + NORM_EPS)
    is_ctx = (first_row + lax.broadcasted_iota(jnp.int32, (x.shape[0], 1), 0)) < n_ctx
    scale = jnp.where(is_ctx, mod_ref[0, 0, 0:1, :], mod_ref[0, 1, 0:1, :])
    shift = jnp.where(is_ctx, mod_ref[0, 0, 1:2, :], mod_ref[0, 1, 1:2, :])
    return xn * scale + shift


def _in_proj_kernel(x_ref, mod_ref, w_ref, p_ref, h_ref, *, n_ctx, bm):
    @pl.when(pl.program_id(2) == 0)
    def _():
        sub = _row_block(bm, 256)

        def body(k, carry):
            rows = pl.ds(pl.multiple_of(k * sub, sub), sub)
            h = _norm_mod(x_ref[0, rows, :], mod_ref, pl.program_id(1) * bm + k * sub, n_ctx)
            h_ref[rows, :] = h.astype(BF16)
            return carry

        lax.fori_loop(0, bm // sub, body, 0)

    p_ref[0] = _dot(h_ref[...], w_ref[...]).astype(p_ref.dtype)


def _in_proj(tok, mod, w, n_ctx):
    bsz, t, d = tok.shape
    n = w.shape[1]
    bm = _row_block(t, IN_BM)
    return pl.pallas_call(
        functools.partial(_in_proj_kernel, n_ctx=n_ctx, bm=bm),
        grid=(bsz, t // bm, n // IN_BN),
        in_specs=[pl.BlockSpec((1, bm, d), lambda b, i, j: (b, i, 0), pipeline_mode=pl.Buffered(1)),
                  pl.BlockSpec((1, 2, 2, d), lambda b, i, j: (b, 0, 0, 0)),
                  pl.BlockSpec((d, IN_BN), lambda b, i, j: (0, j))],
        out_specs=pl.BlockSpec((1, bm, IN_BN), lambda b, i, j: (b, i, j)),
        out_shape=jax.ShapeDtypeStruct((bsz, t, n), F32),
        scratch_shapes=[pltpu.VMEM((bm, d), BF16)],
        compiler_params=pltpu.CompilerParams(
            dimension_semantics=("arbitrary", "arbitrary", "arbitrary"), vmem_limit_bytes=VMEM_LIMIT),
        name="in_proj",
    )(tok, mod, w)


def _out_proj_kernel(s5_ref, wglu_ref, bglu_ref,
                     y2_ref, r_ref, kd_ref, v_ref, g_ref, rk_ref, lnw_ref, lnb_ref,
                     o2_ref, og_ref, hgn_ref, bd_ref,
                     x_ref, gate_ref, wout_ref, modf_ref, wr_ref, br_ref,
                     xo_ref, h_ref, lg_ref, *, n_ctx, bm):
    first = pl.program_id(1) * bm
    bd = bd_ref[...]
    inv_n = 1.0 / HEAD

    def head_sum(z):
        return _dot(z.astype(BF16), bd)

    ys = s5_ref[0].astype(F32)
    ys = ys * _sigmoid(_dot(ys.astype(BF16), wglu_ref[...]) + bglu_ref[...])
    y = y2_ref[0, 0].astype(F32) + y2_ref[1, 0].astype(F32)
    yc = y - head_sum(y) * inv_n
    var = head_sum(yc * yc) * inv_n
    k_bonus = 0.5 * (kd_ref[0, 0].astype(F32) + kd_ref[1, 0].astype(F32))
    bonus = head_sum(r_ref[0].astype(F32) * k_bonus * rk_ref[...]) * v_ref[0].astype(F32)
    yr = (yc * lax.rsqrt(var + RW_GN_EPS) * lnw_ref[...] + lnb_ref[...] + bonus) * g_ref[0].astype(F32)
    o = o2_ref[0, 0].astype(F32) + o2_ref[1, 0].astype(F32)
    yh = o * lax.rsqrt(head_sum(o * o) * inv_n + NORM_EPS) * hgn_ref[...] * _sigmoid(og_ref[0].astype(F32))

    m = _dot(jnp.concatenate([ys, yr, yh], axis=-1).astype(BF16), wout_ref[...])
    is_ctx = (first + lax.broadcasted_iota(jnp.int32, (bm, 1), 0)) < n_ctx
    x_new = x_ref[0] + jnp.where(is_ctx, gate_ref[0, 0:1, :], gate_ref[0, 1:2, :]) * m
    xo_ref[0] = x_new
    h = _norm_mod(x_new, modf_ref, first, n_ctx)
    h_ref[0] = h
    lg_ref[0] = _dot(h.astype(BF16), wr_ref[...]) + br_ref[...]


def _out_proj(s5y, w_glu, b_glu, y2, r, kd, v, g, r_k, ln_w, ln_b, o2, p, og_col, hg_norm,
              tok, gate, w_out, mod_f, w_router, b_router, n_ctx):
    bsz, t, d = tok.shape
    bm = _row_block(t, OUT_BM)
    s5w = s5y.shape[-1]
    w = r.shape[-1]
    row = lambda b, i: (b, i, 0)
    row2 = lambda b, i: (0, b, i, 0)
    c2 = lambda b, i: (0, 0)
    vec = lambda z: z.reshape(1, -1).astype(F32)
    spec_w = pl.BlockSpec((1, bm, w), row)
    spec_2w = pl.BlockSpec((2, 1, bm, w), row2)
    spec_vw = pl.BlockSpec((1, w), c2)
    return pl.pallas_call(
        functools.partial(_out_proj_kernel, n_ctx=n_ctx, bm=bm),
        grid=(bsz, t // bm),
        in_specs=[
            pl.BlockSpec((1, bm, s5w), row), pl.BlockSpec((s5w, s5w), c2), pl.BlockSpec((1, s5w), c2),
            spec_2w, spec_w, spec_2w, spec_w, spec_w, spec_vw, spec_vw, spec_vw,
            spec_2w, pl.BlockSpec((1, bm, w), lambda b, i: (b, i, og_col)), spec_vw, pl.BlockSpec((w, w), c2),
            pl.BlockSpec((1, bm, d), row), pl.BlockSpec((1, 2, d), lambda b, i: (b, 0, 0)),
            pl.BlockSpec(w_out.shape, c2), pl.BlockSpec((1, 2, 2, d), lambda b, i: (b, 0, 0, 0)),
            pl.BlockSpec((d, ROUTER_PAD), c2), pl.BlockSpec((1, ROUTER_PAD), c2),
        ],
        out_specs=[pl.BlockSpec((1, bm, d), row), pl.BlockSpec((1, bm, d), row),
                   pl.BlockSpec((1, bm, ROUTER_PAD), row)],
        out_shape=[jax.ShapeDtypeStruct((bsz, t, d), F32), jax.ShapeDtypeStruct((bsz, t, d), F32),
                   jax.ShapeDtypeStruct((bsz, t, ROUTER_PAD), F32)],
        compiler_params=pltpu.CompilerParams(
            dimension_semantics=("arbitrary", "arbitrary"), vmem_limit_bytes=VMEM_LIMIT),
        name="out_proj",
    )(s5y, w_glu.astype(BF16), vec(b_glu), y2, r, kd, v, g, vec(r_k), vec(ln_w), vec(ln_b),
      o2, p, vec(hg_norm), _head_block_diag(w), tok, gate, w_out.astype(BF16), mod_f,
      w_router.astype(BF16), b_router)


def _final_norm_kernel(x_ref, g_ref, o_ref):
    x = x_ref[0]
    o_ref[0] = x * lax.rsqrt(jnp.mean(x * x, axis=-1, keepdims=True) + NORM_EPS) * g_ref[...]


def _final_norm(x, g):
    bsz, t, d = x.shape
    bm = _row_block(t, 512)
    return pl.pallas_call(
        _final_norm_kernel, grid=(bsz, t // bm),
        in_specs=[pl.BlockSpec((1, bm, d), lambda b, i: (b, i, 0)), pl.BlockSpec((1, d), lambda b, i: (0, 0))],
        out_specs=pl.BlockSpec((1, bm, d), lambda b, i: (b, i, 0)),
        out_shape=jax.ShapeDtypeStruct((bsz, t, d), F32),
        compiler_params=pltpu.CompilerParams(dimension_semantics=("arbitrary", "arbitrary")),
        name="final_norm",
    )(x, g.reshape(1, d).astype(F32))


def _moe_kernel(meta_ref, rows_cur, rows_nxt, h_hbm, w1_ref, w3_ref, w2_ref, y_ref, xbuf, sem, *, n_blocks):
    i = pl.program_id(0)
    n_used = meta_ref[n_blocks]
    slot = lax.rem(i, 2)

    def gather(rows_ref, s):
        def body(k, carry):
            for prio in range(2):
                r = 2 * k + prio
                tok = rows_ref[0, 0, r]
                pltpu.make_async_copy(h_hbm.at[pl.ds(tok, 1), :], xbuf.at[s, pl.ds(r, 1), :],
                                      sem.at[s]).start(priority=prio)
            return carry
        lax.fori_loop(0, MOE_BLOCK // 2, body, 0, unroll=4)

    def wait_slot(s):
        pltpu.make_async_copy(xbuf.at[s], xbuf.at[s], sem.at[s]).wait()

    @pl.when(jnp.logical_and(i == 0, n_used > 0))
    def _():
        gather(rows_cur, 0)

    @pl.when(i + 1 < n_used)
    def _():
        gather(rows_nxt, 1 - slot)

    @pl.when(i < n_used)
    def _():
        wait_slot(slot)
        x = xbuf[slot].astype(BF16)
        a = _dot(x, w1_ref[0])
        b = _dot(x, w3_ref[0])
        y_ref[...] = _dot((a * _sigmoid(a) * b).astype(BF16), w2_ref[0]).astype(y_ref.dtype)

    @pl.when(i >= n_used)
    def _():
        y_ref[...] = jnp.zeros_like(y_ref)


def _moe_experts(h, row_tok, meta, w1, w3, w2):
    n_rows = row_tok.shape[0]
    d = h.shape[1]
    n_blocks = n_rows // MOE_BLOCK
    de = w1.shape[2]
    rows3 = row_tok.reshape(n_blocks, 1, MOE_BLOCK)
    smem_rows = lambda f: pl.BlockSpec((1, 1, MOE_BLOCK), f, memory_space=pltpu.SMEM)
    return pl.pallas_call(
        functools.partial(_moe_kernel, n_blocks=n_blocks),
        grid_spec=pltpu.PrefetchScalarGridSpec(
            num_scalar_prefetch=1, grid=(n_blocks,),
            in_specs=[smem_rows(lambda i, m: (i, 0, 0)),
                      smem_rows(lambda i, m: (jnp.minimum(i + 1, n_blocks - 1), 0, 0)),
                      pl.BlockSpec(memory_space=pl.ANY),
                      pl.BlockSpec((1, d, de), lambda i, m: (m[i], 0, 0)),
                      pl.BlockSpec((1, d, de), lambda i, m: (m[i], 0, 0)),
                      pl.BlockSpec((1, de, d), lambda i, m: (m[i], 0, 0))],
            out_specs=pl.BlockSpec((MOE_BLOCK, d), lambda i, m: (i, 0)),
            scratch_shapes=[pltpu.VMEM((2, MOE_BLOCK, d), F32), pltpu.SemaphoreType.DMA((2,))]),
        out_shape=jax.ShapeDtypeStruct((n_rows, d), F32),
        compiler_params=pltpu.CompilerParams(dimension_semantics=("arbitrary",), vmem_limit_bytes=VMEM_LIMIT),
        name="moe_experts",
    )(meta, rows3, rows3, h, w1, w3, w2)


def _moe_ffn(h_all, tok_row, logits, n_groups, n_experts, w1, w3, w2):
    n = logits.shape[0]
    epg = n_experts // n_groups
    group_logits = logits[:, :n_groups]
    g_sel = jnp.argmax(group_logits, axis=-1)
    g_gate = jnp.take_along_axis(jax.nn.softmax(group_logits, axis=-1), g_sel[:, None], axis=1)
    exp_logits = logits[:, n_groups:n_groups + n_experts].reshape(n, n_groups, epg)
    within = jnp.take_along_axis(exp_logits, g_sel[:, None, None], axis=1)[:, 0]
    top_val, top_idx = lax.top_k(within, TOP_K)
    gate = jax.nn.softmax(top_val, axis=-1) * g_gate
    e_flat = (g_sel[:, None] * epg + top_idx).reshape(-1).astype(jnp.int32)
    n_assign = n * TOP_K
    order = jnp.argsort(e_flat)
    rank = jnp.argsort(order).astype(jnp.int32)
    counts = jnp.sum(e_flat[:, None] == jnp.arange(n_experts, dtype=jnp.int32)[None, :], axis=0, dtype=jnp.int32)
    starts = jnp.cumsum(counts) - counts
    padded = (counts + MOE_BLOCK - 1) // MOE_BLOCK * MOE_BLOCK
    pad_ends = jnp.cumsum(padded)
    pad_starts = pad_ends - padded
    n_blocks = -(-n_assign // MOE_BLOCK) + n_experts
    n_rows = n_blocks * MOE_BLOCK
    block_start = jnp.arange(n_blocks, dtype=jnp.int32) * MOE_BLOCK
    block_exp = jnp.minimum(jnp.sum(pad_ends[None, :] <= block_start[:, None], axis=1, dtype=jnp.int32), n_experts - 1)
    rows = jnp.arange(n_rows, dtype=jnp.int32)
    row_exp = jnp.repeat(block_exp, MOE_BLOCK)
    offset = rows - pad_starts[row_exp]
    src = jnp.clip(starts[row_exp] + offset, 0, n_assign - 1)
    row_tok = tok_row(jnp.where(offset < counts[row_exp], order[src].astype(jnp.int32) // TOP_K, 0))
    meta = jnp.concatenate([block_exp, (pad_ends[-1:] // MOE_BLOCK).astype(jnp.int32)])
    y_rows = _moe_experts(h_all, row_tok, meta, w1, w3, w2)
    pos = (pad_starts[e_flat] + rank - starts[e_flat]).reshape(n, TOP_K)
    return gate[:, 0:1] * y_rows[pos[:, 0]] + gate[:, 1:2] * y_rows[pos[:, 1]]


def kernel(x, c, ctx, c_ctx, ada_w, ada_b, norm_mix, norm_ffn, w_in, w_out, s5_lam_re, s5_lam_im, s5_log_dt, s5_b_re, s5_b_im, s5_c_re, s5_c_im, s5_d, s5_w_glu, s5_b_glu, rw_mu, rw_w0, rw_w2, rw_a0, rw_a2, rw_g2, rw_k_k, rw_k_a, rw_r_k, rw_ln_w, rw_ln_b, hg_lb_logits, hg_norm, moe_w_group, moe_b_group, moe_w_expert, moe_b_expert, moe_w1, moe_w3, moe_w2, final_norm):
    bsz, n_lat, d = x.shape
    n_ctx = ctx.shape[1]
    depth = w_in.shape[0]
    s5_width = s5_d.shape[1]
    rw_width = rw_k_k.shape[1]
    rw_in = rw_mu.shape[2]
    hg_width = hg_norm.shape[1]
    n_groups = moe_w_group.shape[2]
    n_experts = moe_w_expert.shape[2]
    t = n_ctx + n_lat
    hg_col0 = -(-rw_in // hg_width)
    s5_off = (hg_col0 + 5) * hg_width
    n_proj = -(-(s5_off + s5_width) // IN_BN) * IN_BN
    pad1 = hg_col0 * hg_width - rw_in
    pad2 = n_proj - s5_off - s5_width

    silu_c = jax.nn.silu(c)
    silu_cc = jax.nn.silu(c_ctx)
    lb_all = jnp.cumsum(jax.nn.softmax(hg_lb_logits.astype(F32), axis=0), axis=0)
    lb_all = lb_all - lb_all[0:1]
    tok = jnp.concatenate([ctx, x], axis=1)
    for l in range(depth):
        last = l == depth - 1
        mod_lat = jnp.split(silu_c @ ada_w[l] + ada_b[l], 6, axis=-1)
        mod_ctx = [jnp.broadcast_to(z, (bsz, d)) for z in jnp.split(silu_cc @ ada_w[l] + ada_b[l], 6, axis=-1)]
        sh_m, sc_m, gt_m, sh_f, sc_f, gt_f = [jnp.stack([zc, zl], axis=1) for zc, zl in zip(mod_ctx, mod_lat)]
        mod_m = jnp.stack([norm_mix[l] * (1.0 + sc_m), sh_m], axis=2)
        mod_f = jnp.stack([norm_ffn[l] * (1.0 + sc_f), sh_f], axis=2)

        w_cols = [w_in[l][:, s5_width:s5_width + rw_in], jnp.zeros((d, pad1), F32),
                  w_in[l][:, s5_width + rw_in:], w_in[l][:, :s5_width], jnp.zeros((d, pad2), F32)]
        p = _in_proj(tok, mod_m, jnp.concatenate(w_cols, axis=1).astype(BF16), n_ctx)

        r, v, kk, g, lw, kd, bb = _rw_prep(p, n_ctx, rw_mu[l], rw_w0[l], rw_w2[l], rw_a0[l], rw_a2[l],
                                           rw_g2[l], rw_k_k[l], rw_k_a[l])
        y2, w1_b, w3_b = _rw_scan(r, lw, kd, v, kk, bb, n_ctx, casts=((moe_w1, l), (moe_w3, l)))
        o2, w2_b = _hg_scan(p, lb_all[l], n_ctx, col0=hg_col0, casts=((moe_w2, l),))
        experts = (w1_b.reshape(moe_w1.shape[1:]), w3_b.reshape(moe_w3.shape[1:]), w2_b.reshape(moe_w2.shape[1:]))
        s5y = _s5_scan(p[:, :, s5_off:s5_off + s5_width], n_ctx, s5_lam_re[l], s5_lam_im[l], s5_log_dt[l],
                       s5_b_re[l], s5_b_im[l], s5_c_re[l], s5_c_im[l], s5_d[l])

        w_router = jnp.concatenate([moe_w_group[l], moe_w_expert[l],
                                    jnp.zeros((d, ROUTER_PAD - n_groups - n_experts), F32)], axis=1)
        b_router = jnp.concatenate([moe_b_group[l], moe_b_expert[l],
                                    jnp.zeros((ROUTER_PAD - n_groups - n_experts,), F32)]).reshape(1, ROUTER_PAD)
        tok, h_ffn, logits = _out_proj(
            s5y, s5_w_glu[l], s5_b_glu[l], y2, r, kd, v, g, rw_r_k[l], rw_ln_w[l], rw_ln_b[l],
            o2, p, hg_col0 + 4, hg_norm[l], tok, gt_m, w_out[l], mod_f, w_router, b_router, n_ctx)

        h_rows = h_ffn.reshape(bsz * t, d)
        if last:
            lat_row = lambda n: n + (n // n_lat + 1) * n_ctx
            f = _moe_ffn(h_rows, lat_row, logits[:, n_ctx:].reshape(bsz * n_lat, ROUTER_PAD),
                         n_groups, n_experts, *experts).reshape(bsz, n_lat, d)
            x = tok[:, n_ctx:] + gt_f[:, 1:2] * f
        else:
            f = _moe_ffn(h_rows, lambda n: n, logits.reshape(bsz * t, ROUTER_PAD),
                         n_groups, n_experts, *experts).reshape(bsz, t, d)
            tok = jnp.concatenate([tok[:, :n_ctx] + gt_f[:, 0:1] * f[:, :n_ctx],
                                   tok[:, n_ctx:] + gt_f[:, 1:2] * f[:, n_ctx:]], axis=1)
    return _final_norm(x, final_norm)
```
